```python
import jax, jax.numpy as jnp
from jax import lax
import numpy as np

D_MODEL = 1024
BATCH = 16
SEQ = 2048
DEPTH = 2
DEC_BATCH = 2
DEC_SEQ = 16384
PAST_LEN = 128

GRID_W = 64
HEAD_DIM = 64
EPS = 1e-6
A_Q_HEADS = 8
A_KV_HEADS = 2
ROPE_THETA = 10000.0
Q_BLOCK = 128
B_PATTERNS = ((128, 1), (512, 4), (2048, 16))
B_BRANCHES = len(B_PATTERNS)
B_HEADS = 4
C_HEADS = 16
C_WIN_ROWS = 8
C_WIN_COLS = 16
C_COL_BLOCK = 16
C_SLAB = 32
X_HEADS = 4
X_HEAD_DIM = D_MODEL // X_HEADS
MEM_TOKENS = 256
D_FF = ((8 * D_MODEL + 3 * 256 - 1) // (3 * 256)) * 256
A_Q_W = A_Q_HEADS * HEAD_DIM
A_KV_W = A_KV_HEADS * HEAD_DIM
B_QKV_W = 3 * B_BRANCHES * B_HEADS * HEAD_DIM
IN_AB = A_Q_W + 2 * A_KV_W + B_QKV_W
OUT_AB = A_Q_W + B_HEADS * HEAD_DIM
IN_C = 3 * C_HEADS * HEAD_DIM
OUT_C = C_HEADS * HEAD_DIM

kernel_name = "hybrid_bidir_encoder_gqa_dilated_natten"


def _rms_norm(x, g):
    xf = x.astype(jnp.float32)
    y = xf * lax.rsqrt(jnp.mean(xf * xf, axis=-1, keepdims=True) + EPS)
    return (y * g.astype(jnp.float32)).astype(x.dtype)


def _alibi_slopes(n):
    return jnp.exp2(-8.0 * jnp.arange(1, n + 1, dtype=jnp.float32) / n)


def _axial_rope_tables(n_tok):
    t = jnp.arange(n_tok, dtype=jnp.int32)
    row = (t // GRID_W).astype(jnp.float32)
    col = (t % GRID_W).astype(jnp.float32)
    axis_dim = HEAD_DIM // 2
    inv_freq = ROPE_THETA ** (-jnp.arange(0, axis_dim, 2, dtype=jnp.float32) / axis_dim)
    ang = jnp.concatenate([row[:, None] * inv_freq, col[:, None] * inv_freq], axis=-1)
    return jnp.cos(ang), jnp.sin(ang)


def _apply_rope(x, cos, sin):
    x2 = x.reshape(*x.shape[:-1], HEAD_DIM // 2, 2)
    xr, xi = x2[..., 0], x2[..., 1]
    c = cos[None, :, None, :].astype(x.dtype)
    s = sin[None, :, None, :].astype(x.dtype)
    return jnp.stack([xr * c - xi * s, xr * s + xi * c], axis=-1).reshape(x.shape)


def _gqa_block_attention(q, k, v):
    bn, s_len = q.shape[:2]
    grp = A_Q_HEADS // A_KV_HEADS
    nb = s_len // Q_BLOCK
    qb = q.reshape(bn, nb, Q_BLOCK, A_KV_HEADS, grp, HEAD_DIM).transpose(1, 0, 2, 3, 4, 5)
    scale = HEAD_DIM ** -0.5

    def one_block(qi):
        s = jnp.einsum('bqkgd,bskd->bkgqs', qi, k, preferred_element_type=jnp.float32) * scale
        p = jax.nn.softmax(s, axis=-1).astype(v.dtype)
        return jnp.einsum('bkgqs,bskd->bqkgd', p, v)

    o = lax.map(one_block, qb)
    return o.transpose(1, 0, 2, 3, 4, 5).reshape(bn, s_len, A_Q_HEADS * HEAD_DIM)


def _dilated_branch(q, k, v, dil, half, slopes):
    bn, s_len, nh, dh = q.shape
    L = s_len // dil
    nb = -(-L // half)
    Lp = nb * half

    def to_strided(x):
        return x.reshape(bn, L, dil, nh, dh).transpose(0, 2, 1, 3, 4).reshape(bn * dil, L, nh, dh)

    qs, ks, vs = to_strided(q), to_strided(k), to_strided(v)
    qb = jnp.pad(qs, ((0, 0), (0, Lp - L), (0, 0), (0, 0))).reshape(bn * dil, nb, half, nh, dh)

    def bands(x):
        xp = jnp.pad(x, ((0, 0), (half, Lp - L + half), (0, 0), (0, 0)))
        return jnp.concatenate(
            [xp[:, o:o + Lp].reshape(bn * dil, nb, half, nh, dh) for o in (0, half, 2 * half)], axis=2)

    kb, vb = bands(ks), bands(vs)
    qi = np.arange(nb)[:, None, None] * half + np.arange(half)[None, :, None]
    kj = np.arange(nb)[:, None, None] * half + np.arange(3 * half)[None, None, :] - half
    rel = kj - qi
    valid = (np.abs(rel) <= half) & (kj >= 0) & (kj < L)
    dist = (np.abs(rel) * dil).astype(np.float32)
    s = jnp.einsum('znqhd,znkhd->znhqk', qb, kb, preferred_element_type=jnp.float32) * (dh ** -0.5)
    s = s - slopes[None, :, None, None] * dist[:, None]
    s = jnp.where(valid[:, None], s, -jnp.inf)
    lse = jax.nn.logsumexp(s, axis=-1, keepdims=True)
    p = jnp.exp(s - lse).astype(vb.dtype)
    o = jnp.einsum('znhqk,znkhd->znqhd', p, vb)
    lse = jnp.moveaxis(lse[..., 0], 2, 3)

    def from_strided(y):
        rest = y.shape[3:]
        y = y.reshape(bn, dil, Lp, *rest)[:, :, :L]
        return jnp.moveaxis(y, 1, 2).reshape(bn, s_len, *rest)

    return from_strided(o), from_strided(lse)


def _dilated_mixture(zb):
    bn, s_len = zb.shape[:2]
    slopes = _alibi_slopes(B_BRANCHES * B_HEADS).reshape(B_BRANCHES, B_HEADS)
    outs, lses = [], []
    for g, (window, dil) in enumerate(B_PATTERNS):
        o, lse = _dilated_branch(zb[:, :, 0, g], zb[:, :, 1, g], zb[:, :, 2, g],
                                 dil, window // (2 * dil), slopes[g])
        outs.append(o)
        lses.append(lse)
    wts = jax.nn.softmax(jnp.stack(lses), axis=0)
    out = jnp.sum(wts[..., None].astype(outs[0].dtype) * jnp.stack(outs), axis=0)
    return out.reshape(bn, s_len, B_HEADS * HEAD_DIM)


def _neighbourhood_attention(q, k, v, rpb):
    bn, s_len, nh, dh = q.shape
    rows = s_len // GRID_W
    kh = min(C_WIN_ROWS, rows)
    qg = q.reshape(bn, rows, GRID_W, nh, dh)
    kg = k.reshape(bn, rows, GRID_W, nh, dh)
    vg = v.reshape(bn, rows, GRID_W, nh, dh)
    n_cb = GRID_W // C_COL_BLOCK
    slab_start = np.clip(np.arange(n_cb) * C_COL_BLOCK - C_WIN_COLS // 2, 0, GRID_W - C_SLAB)
    qcol = np.arange(n_cb)[:, None] * C_COL_BLOCK + np.arange(C_COL_BLOCK)[None, :]
    cstart = np.clip(qcol - C_WIN_COLS // 2, 0, GRID_W - C_WIN_COLS)
    kcol = slab_start[:, None] + np.arange(C_SLAB)[None, :]
    col_valid = (kcol[:, None, :] >= cstart[:, :, None]) & (kcol[:, None, :] < cstart[:, :, None] + C_WIN_COLS)
    col_idx = np.clip(kcol[:, None, :] - qcol[:, :, None] + C_WIN_COLS - 1, 0, 2 * C_WIN_COLS - 2)
    col_bias = rpb[:, :, col_idx]
    scale = dh ** -0.5

    def row_fn(r):
        rs = jnp.clip(r - kh // 2, 0, rows - kh)
        kr = lax.dynamic_slice_in_dim(kg, rs, kh, axis=1)
        vr = lax.dynamic_slice_in_dim(vg, rs, kh, axis=1)
        ks = jnp.stack([kr[:, :, s0:s0 + C_SLAB] for s0 in slab_start], axis=2)
        vs = jnp.stack([vr[:, :, s0:s0 + C_SLAB] for s0 in slab_start], axis=2)
        qr = lax.dynamic_index_in_dim(qg, r, axis=1, keepdims=False).reshape(bn, n_cb, C_COL_BLOCK, nh, dh)
        s = jnp.einsum('bnqhd,binchd->bhnqic', qr, ks, preferred_element_type=jnp.float32) * scale
        ridx = rs + jnp.arange(kh) - r + (C_WIN_ROWS - 1)
        bias = jnp.take(col_bias, ridx, axis=1).transpose(0, 2, 3, 1, 4)
        s = s + bias[None].astype(jnp.float32)
        s = jnp.where(col_valid[:, :, None, :], s, -jnp.inf)
        shp = s.shape
        p = jax.nn.softmax(s.reshape(*shp[:-2], kh * C_SLAB), axis=-1).reshape(shp).astype(vs.dtype)
        o = jnp.einsum('bhnqic,binchd->bnqhd', p, vs)
        return o.reshape(bn, GRID_W, nh, dh)

    out = lax.map(row_fn, jnp.arange(rows, dtype=jnp.int32))
    return out.transpose(1, 0, 2, 3, 4).reshape(bn, s_len, nh * dh)


def _mixer_ab(h, w_in, g_qn, g_kn, w_out):
    bn, s_len, _ = h.shape
    z = h @ w_in
    qa, ka, va, zb = jnp.split(z, [A_Q_W, A_Q_W + A_KV_W, A_Q_W + 2 * A_KV_W], axis=-1)
    qa = _rms_norm(qa.reshape(bn, s_len, A_Q_HEADS, HEAD_DIM), g_qn)
    ka = _rms_norm(ka.reshape(bn, s_len, A_KV_HEADS, HEAD_DIM), g_kn)
    cos, sin = _axial_rope_tables(s_len)
    qa = _apply_rope(qa, cos, sin)
    ka = _apply_rope(ka, cos, sin)
    va = va.reshape(bn, s_len, A_KV_HEADS, HEAD_DIM)
    oa = _gqa_block_attention(qa, ka, va)
    ob = _dilated_mixture(zb.reshape(bn, s_len, 3, B_BRANCHES, B_HEADS, HEAD_DIM))
    return jnp.concatenate([oa, ob], axis=-1) @ w_out


def _mixer_c(h, w_in, rpb, w_out):
    bn, s_len, _ = h.shape
    z = (h @ w_in).reshape(bn, s_len, 3, C_HEADS, HEAD_DIM)
    return _neighbourhood_attention(z[:, :, 0], z[:, :, 1], z[:, :, 2], rpb) @ w_out


def _memory_cross_attention(h, mem, g_mem, wq, wkv, wo):
    bn, s_len, _ = h.shape
    n_mem = mem.shape[1]
    m = _rms_norm(mem, g_mem)
    q = (h @ wq).reshape(bn, s_len, X_HEADS, X_HEAD_DIM)
    kv = (m @ wkv).reshape(bn, n_mem, 2, X_HEADS, X_HEAD_DIM)
    s = jnp.einsum('bshd,bmhd->bhsm', q, kv[:, :, 0], preferred_element_type=jnp.float32) * (X_HEAD_DIM ** -0.5)
    p = jax.nn.softmax(s, axis=-1).astype(kv.dtype)
    o = jnp.einsum('bhsm,bmhd->bshd', p, kv[:, :, 1]).reshape(bn, s_len, D_MODEL)
    return o @ wo


def _swiglu(h, w_gu, w_down):
    g, u = jnp.split(h @ w_gu, 2, axis=-1)
    return (jax.nn.silu(g) * u) @ w_down


def _encoder_trunk(x, mem, g_mix, w_in_ab, g_qn, g_kn, w_out_ab, w_in_c, rpb_c, w_out_c,
                   g_xattn, g_mem, wq_x, wkv_x, wo_x, g_ffn, w_gu, w_down, g_final):
    for l in range(DEPTH):
        h = _rms_norm(x, g_mix[l])
        if l % 2 == 0:
            e = l // 2
            x = x + _mixer_ab(h, w_in_ab[e], g_qn[e], g_kn[e], w_out_ab[e])
        else:
            o = l // 2
            x = x + _mixer_c(h, w_in_c[o], rpb_c[o], w_out_c[o])
        x = x + _memory_cross_attention(_rms_norm(x, g_xattn[l]), mem, g_mem[l], wq_x[l], wkv_x[l], wo_x[l])
        x = x + _swiglu(_rms_norm(x, g_ffn[l]), w_gu[l], w_down[l])
    return _rms_norm(x, g_final)


def setup_inputs(seed: int = 0) -> dict:
    key = jax.random.key(seed)
    ks = jax.random.split(key, 24)
    f32 = jnp.float32
    n_even = (DEPTH + 1) // 2
    n_odd = DEPTH // 2

    def nrm(k, shape, scale):
        return jax.random.normal(k, shape, f32) * scale

    def gain(k, shape):
        return 1.0 + 0.05 * jax.random.normal(k, shape, f32)

    return {
        "x_prompt": nrm(ks[0], (BATCH, SEQ, D_MODEL), 1.0),
        "x_sample": nrm(ks[1], (DEC_BATCH, DEC_SEQ, D_MODEL), 1.0),
        "mem_prompt": nrm(ks[2], (BATCH, MEM_TOKENS, D_MODEL), 1.0),
        "mem_sample": nrm(ks[3], (DEC_BATCH, MEM_TOKENS, D_MODEL), 1.0),
        "g_mix": gain(ks[4], (DEPTH, D_MODEL)),
        "w_in_ab": nrm(ks[5], (n_even, D_MODEL, IN_AB), D_MODEL ** -0.5),
        "g_qn": gain(ks[6], (n_even, HEAD_DIM)),
        "g_kn": gain(ks[7], (n_even, HEAD_DIM)),
        "w_out_ab": nrm(ks[8], (n_even, OUT_AB, D_MODEL), OUT_AB ** -0.5),
        "w_in_c": nrm(ks[9], (n_odd, D_MODEL, IN_C), D_MODEL ** -0.5),
        "rpb_c": nrm(ks[10], (n_odd, C_HEADS, 2 * C_WIN_ROWS - 1, 2 * C_WIN_COLS - 1), 0.1),
        "w_out_c": nrm(ks[11], (n_odd, OUT_C, D_MODEL), OUT_C ** -0.5),
        "g_xattn": gain(ks[12], (DEPTH, D_MODEL)),
        "g_mem": gain(ks[13], (DEPTH, D_MODEL)),
        "wq_x": nrm(ks[14], (DEPTH, D_MODEL, D_MODEL), D_MODEL ** -0.5),
        "wkv_x": nrm(ks[15], (DEPTH, D_MODEL, 2 * D_MODEL), D_MODEL ** -0.5),
        "wo_x": nrm(ks[16], (DEPTH, D_MODEL, D_MODEL), D_MODEL ** -0.5),
        "g_ffn": gain(ks[17], (DEPTH, D_MODEL)),
        "w_gu": nrm(ks[18], (DEPTH, D_MODEL, 2 * D_FF), D_MODEL ** -0.5),
        "w_down": nrm(ks[19], (DEPTH, D_FF, D_MODEL), D_FF ** -0.5),
        "g_final": gain(ks[20], (D_MODEL,)),
    }


def reference(x_prompt, x_sample, mem_prompt, mem_sample, g_mix, w_in_ab, g_qn, g_kn, w_out_ab,
              w_in_c, rpb_c, w_out_c, g_xattn, g_mem, wq_x, wkv_x, wo_x, g_ffn, w_gu, w_down, g_final):
    y_prompt = _encoder_trunk(x_prompt, mem_prompt, g_mix, w_in_ab, g_qn, g_kn, w_out_ab, w_in_c, rpb_c,
                              w_out_c, g_xattn, g_mem, wq_x, wkv_x, wo_x, g_ffn, w_gu, w_down, g_final)
    y_sample = _encoder_trunk(x_sample, mem_sample, g_mix, w_in_ab, g_qn, g_kn, w_out_ab, w_in_c, rpb_c,
                              w_out_c, g_xattn, g_mem, wq_x, wkv_x, wo_x, g_ffn, w_gu, w_down, g_final)
    return (y_prompt, y_sample)
```

```python
import functools
import math

import numpy as np
import jax
import jax.numpy as jnp
from jax import lax
from jax.experimental import pallas as pl
from jax.experimental.pallas import tpu as pltpu

F32 = jnp.float32
BF16 = jnp.bfloat16

D_MODEL = 1024
GRID_W = 64
HEAD_DIM = 64
EPS = 1e-6
NEG = -1e30
LOG2E = math.log2(math.e)

A_Q_HEADS = 8
A_KV_HEADS = 2
A_GROUP = A_Q_HEADS // A_KV_HEADS
ROPE_THETA = 10000.0
A_Q_W = A_Q_HEADS * HEAD_DIM
A_KV_W = A_KV_HEADS * HEAD_DIM
A_W = A_Q_W + 2 * A_KV_W
A_TK = 512

B_PATTERNS = ((128, 1), (512, 4), (2048, 16))
B_BRANCHES = len(B_PATTERNS)
B_HEADS = 4
B_HALF = 64
B_W = B_HEADS * HEAD_DIM
B_SUB = 2 * B_HALF
B_KEYS = 4 * B_HALF
LSE_REP = 2 * HEAD_DIM // B_HEADS

C_HEADS = 16
C_WIN_ROWS = 8
C_WIN_COLS = 16
C_ROWS_Q = 4
C_ROWS_K = 12
C_W = C_HEADS * HEAD_DIM

X_HEADS = 4
X_HEAD_DIM = D_MODEL // X_HEADS
D_FF = 2816

VMEM_LIMIT = 56 * 1024 * 1024


def _cparams(n_axes):
    return pltpu.CompilerParams(dimension_semantics=("arbitrary",) * n_axes,
                                vmem_limit_bytes=VMEM_LIMIT)


def _rms(x, g):
    ms = jnp.mean(x * x, axis=-1, keepdims=True)
    return x * lax.rsqrt(ms + EPS) * g


def _dot(a, b):
    return jnp.dot(a, b, preferred_element_type=F32)


def _dot_nt(a, b):
    return lax.dot_general(a, b, (((1,), (1,)), ((), ())), preferred_element_type=F32)


def _const_spec(shape):
    zeros = (0,) * len(shape)
    return pl.BlockSpec(shape, lambda *_: zeros)


def _inproj_ab_kernel(x_ref, g_ref, waT_ref, wb_ref, gq_ref, gk_ref, cos_ref, sin_ref,
                      qT_ref, k_ref, vT_ref, zb_ref):
    hb = _rms(x_ref[...], g_ref[...]).astype(BF16)
    zT = _dot_nt(waT_ref[...], hb)
    cos = cos_ref[...]
    sin = sin_ref[...]
    half = HEAD_DIM // 2

    def norm_rope(zh, gcol):
        ms = jnp.mean(zh * zh, axis=0, keepdims=True)
        y = zh * lax.rsqrt(ms + EPS) * gcol
        yr, yi = y[:half], y[half:]
        return jnp.concatenate([yr * cos - yi * sin, yr * sin + yi * cos], axis=0)

    for h in range(A_Q_HEADS):
        sl = slice(h * HEAD_DIM, (h + 1) * HEAD_DIM)
        qT_ref[sl, :] = norm_rope(zT[sl], gq_ref[...]).astype(BF16)
    kT = jnp.concatenate(
        [norm_rope(zT[A_Q_W + h * HEAD_DIM:A_Q_W + (h + 1) * HEAD_DIM], gk_ref[...])
         for h in range(A_KV_HEADS)], axis=0)
    k_nat = kT.T
    for h in range(A_KV_HEADS):
        k_ref[h] = k_nat[:, h * HEAD_DIM:(h + 1) * HEAD_DIM].astype(BF16)
    for h in range(A_KV_HEADS):
        for c in range(vT_ref.shape[1]):
            vT_ref[h, c] = zT[A_Q_W + A_KV_W + h * HEAD_DIM:A_Q_W + A_KV_W + (h + 1) * HEAD_DIM,
                              c * A_TK:(c + 1) * A_TK].astype(BF16)
    for g in range(B_BRANCHES):
        zb_ref[g] = _dot(hb, wb_ref[g]).astype(BF16)


def _inproj_ab(x, g_mix, waT, wb, gq, gk, cosT, sinT, seq, tm=512):
    t = x.shape[0]
    n_seq_tiles = seq // tm
    return pl.pallas_call(
        _inproj_ab_kernel,
        grid=(t // tm,),
        in_specs=[
            pl.BlockSpec((tm, D_MODEL), lambda i: (i, 0)),
            _const_spec((1, D_MODEL)),
            _const_spec((A_W, D_MODEL)),
            _const_spec((B_BRANCHES, D_MODEL, 3 * B_W)),
            _const_spec((HEAD_DIM, 1)),
            _const_spec((HEAD_DIM, 1)),
            pl.BlockSpec((HEAD_DIM // 2, tm), lambda i: (0, i % n_seq_tiles)),
            pl.BlockSpec((HEAD_DIM // 2, tm), lambda i: (0, i % n_seq_tiles)),
        ],
        out_specs=[
            pl.BlockSpec((A_Q_W, tm), lambda i: (0, i)),
            pl.BlockSpec((A_KV_HEADS, tm, HEAD_DIM), lambda i: (0, i, 0)),
            pl.BlockSpec((A_KV_HEADS, tm // A_TK, HEAD_DIM, A_TK), lambda i: (0, i, 0, 0)),
            pl.BlockSpec((B_BRANCHES, tm, 3 * B_W), lambda i: (0, i, 0)),
        ],
        out_shape=[
            jax.ShapeDtypeStruct((A_Q_W, t), BF16),
            jax.ShapeDtypeStruct((A_KV_HEADS, t, HEAD_DIM), BF16),
            jax.ShapeDtypeStruct((A_KV_HEADS, t // A_TK, HEAD_DIM, A_TK), BF16),
            jax.ShapeDtypeStruct((B_BRANCHES, t, 3 * B_W), BF16),
        ],
        compiler_params=_cparams(1),
        name="inproj_ab",
    )(x, g_mix, waT, wb, gq, gk, cosT, sinT)


def _gqa_kernel(qT_ref, k_ref, vT_ref, o_ref):
    tq = qT_ref.shape[1]
    tk = A_TK
    n_chunks = vT_ref.shape[0]
    qTs = [qT_ref[h * HEAD_DIM:(h + 1) * HEAD_DIM, :] for h in range(A_GROUP)]

    def body(j, carry):
        ms, ls, accs = carry
        off = pl.multiple_of(j * tk, tk)
        k = k_ref[pl.ds(off, tk), :]
        vT = vT_ref[j]
        new_ms, new_ls, new_accs = [], [], []
        for h in range(A_GROUP):
            s = _dot(k, qTs[h])
            m_new = jnp.maximum(ms[h], jnp.max(s, axis=0, keepdims=True))
            p = jnp.exp2(s - m_new)
            alpha = jnp.exp2(ms[h] - m_new)
            new_ls.append(alpha * ls[h] + jnp.sum(p, axis=0, keepdims=True))
            new_accs.append(alpha * accs[h] + _dot(vT, p.astype(BF16)))
            new_ms.append(m_new)
        return tuple(new_ms), tuple(new_ls), tuple(new_accs)

    init = (tuple(jnp.full((1, tq), NEG, F32) for _ in range(A_GROUP)),
            tuple(jnp.zeros((1, tq), F32) for _ in range(A_GROUP)),
            tuple(jnp.zeros((HEAD_DIM, tq), F32) for _ in range(A_GROUP)))
    _, ls, accs = lax.fori_loop(0, n_chunks, body, init)
    oT = jnp.concatenate([accs[h] / ls[h] for h in range(A_GROUP)], axis=0)
    o_ref[...] = oT.T.astype(BF16)


def _gqa_attention(qT, k, vT, batch, seq, tq=256):
    t = batch * seq
    nq = seq // tq
    k4 = k.reshape(A_KV_HEADS, batch, seq, HEAD_DIM)
    v5 = vT.reshape(A_KV_HEADS, batch, seq // A_TK, HEAD_DIM, A_TK)
    return pl.pallas_call(
        _gqa_kernel,
        grid=(batch, A_KV_HEADS, nq),
        in_specs=[
            pl.BlockSpec((A_GROUP * HEAD_DIM, tq), lambda b, g, i: (g, b * nq + i)),
            pl.BlockSpec((None, None, seq, HEAD_DIM), lambda b, g, i: (g, b, 0, 0)),
            pl.BlockSpec((None, None, seq // A_TK, HEAD_DIM, A_TK), lambda b, g, i: (g, b, 0, 0, 0)),
        ],
        out_specs=pl.BlockSpec((tq, A_GROUP * HEAD_DIM), lambda b, g, i: (b * nq + i, g)),
        out_shape=jax.ShapeDtypeStruct((t, A_Q_W), BF16),
        compiler_params=_cparams(3),
        name="gqa_attention",
    )(qT, k4, v5)


def _dilated_kernel(*refs, seq_len, cq, has_prev, is_last):
    n_in = 6 if has_prev else 4
    main_ref, prev_ref, next_ref, bias_ref = refs[:4]
    po_ref, plse_ref = refs[4:n_in] if has_prev else (None, None)
    o_ref = refs[n_in]
    lse_ref = None if is_last else refs[n_in + 1]
    kv_scr = refs[-1]

    n = pl.program_id(2)
    kv_scr[0:B_HALF, :] = prev_ref[:, B_W:]
    kv_scr[B_HALF:B_HALF + cq, :] = main_ref[:, B_W:]
    kv_scr[B_HALF + cq:, :] = next_ref[:, B_W:]

    pair_w = 2 * HEAD_DIM
    lane = lax.broadcasted_iota(jnp.int32, (1, pair_w), 1)
    lo = lane < HEAD_DIM
    key_col = lax.broadcasted_iota(jnp.int32, (1, B_KEYS), 1)

    for j in range(cq // B_SUB):
        key_pos = key_col + (n * cq + j * B_SUB - B_HALF)
        edge = jnp.where((key_pos >= 0) & (key_pos < seq_len), 0.0, NEG).astype(F32)
        rows = slice(j * B_SUB, (j + 1) * B_SUB)
        krows = slice(j * B_SUB, j * B_SUB + B_KEYS)
        lses = []
        for pr in range(B_HEADS // 2):
            cols = slice(pr * pair_w, (pr + 1) * pair_w)
            q2 = main_ref[rows, cols]
            k2 = kv_scr[krows, cols]
            v2 = kv_scr[krows, B_W + pr * pair_w:B_W + (pr + 1) * pair_w]
            o_pair, w_prev = [], []
            for hh in range(2):
                head = pr * 2 + hh
                qm = jnp.where(lo if hh == 0 else ~lo, q2, jnp.zeros_like(q2))
                s = _dot_nt(qm, k2) + bias_ref[head] + edge
                m = jnp.max(s, axis=-1, keepdims=True)
                p = jnp.exp(s - m)
                l = jnp.sum(p, axis=-1, keepdims=True)
                o = _dot(p.astype(BF16), v2) / l
                lse = m + jnp.log(l)
                if has_prev:
                    lse_p = plse_ref[rows, head * LSE_REP:head * LSE_REP + 1]
                    mx = jnp.maximum(lse_p, lse)
                    wp = jnp.exp(lse_p - mx)
                    wc = jnp.exp(lse - mx)
                    o = o * (wc / (wp + wc))
                    w_prev.append(wp / (wp + wc))
                    lse = mx + jnp.log(wp + wc)
                o_pair.append(o)
                lses.append(lse)
            o2 = jnp.where(lo, o_pair[0], o_pair[1])
            if has_prev:
                o2 = o2 + jnp.where(lo, w_prev[0], w_prev[1]) * po_ref[rows, cols]
            o_ref[rows, cols] = o2.astype(o_ref.dtype)
        if lse_ref is not None:
            tile = lses[B_HEADS - 1]
            for head in range(B_HEADS - 2, -1, -1):
                tile = jnp.where(lane < (head + 1) * LSE_REP, lses[head], tile)
            lse_ref[rows, :] = tile


def _dilated_bias(branch):
    _, dil = B_PATTERNS[branch]
    slopes = np.exp2(-8.0 * np.arange(1, B_BRANCHES * B_HEADS + 1, dtype=np.float64)
                     / (B_BRANCHES * B_HEADS)).reshape(B_BRANCHES, B_HEADS)[branch]
    a = np.arange(B_SUB)[:, None]
    c = np.arange(B_KEYS)[None, :]
    rel = (c - B_HALF) - a
    bias = -slopes[:, None, None] * (np.abs(rel) * dil).astype(np.float64)[None]
    bias = np.where((np.abs(rel) <= B_HALF)[None], bias, NEG)
    return jnp.asarray(bias, F32)


def _dilated_branch(zb, branch, batch, seq, prev):
    _, dil = B_PATTERNS[branch]
    t = batch * seq
    seq_len = seq // dil
    cq = min(512, seq_len)
    nchunk = seq_len // cq
    nb64 = seq_len // B_HALF
    per = cq // B_HALF
    is_last = branch == B_BRANCHES - 1
    has_prev = prev is not None
    zv = zb.reshape(B_BRANCHES, batch, seq_len, dil * 3 * B_W)
    bias = _dilated_bias(branch)

    def view(a, w):
        return a.reshape(batch, seq_len, dil * w)

    in_specs = [
        pl.BlockSpec((None, None, cq, 3 * B_W), lambda b, r, n: (branch, b, n, r)),
        pl.BlockSpec((None, None, B_HALF, 3 * B_W),
                     lambda b, r, n: (branch, b, jnp.maximum(n * per - 1, 0), r)),
        pl.BlockSpec((None, None, B_HALF, 3 * B_W),
                     lambda b, r, n: (branch, b, jnp.minimum((n + 1) * per, nb64 - 1), r)),
        _const_spec((B_HEADS, B_SUB, B_KEYS)),
    ]
    args = [zv, zv, zv, bias]
    if has_prev:
        in_specs += [pl.BlockSpec((None, cq, B_W), lambda b, r, n: (b, n, r)),
                     pl.BlockSpec((None, cq, 2 * HEAD_DIM), lambda b, r, n: (b, n, r))]
        args += [view(prev[0], B_W), view(prev[1], 2 * HEAD_DIM)]
    o_dtype = BF16 if is_last else F32
    out_specs = [pl.BlockSpec((None, cq, B_W), lambda b, r, n: (b, n, r))]
    out_shape = [jax.ShapeDtypeStruct((batch, seq_len, dil * B_W), o_dtype)]
    if not is_last:
        out_specs.append(pl.BlockSpec((None, cq, 2 * HEAD_DIM), lambda b, r, n: (b, n, r)))
        out_shape.append(jax.ShapeDtypeStruct((batch, seq_len, dil * 2 * HEAD_DIM), F32))
    res = pl.pallas_call(
        functools.partial(_dilated_kernel, seq_len=seq_len, cq=cq, has_prev=has_prev, is_last=is_last),
        grid=(batch, dil, nchunk),
        in_specs=in_specs,
        out_specs=out_specs,
        out_shape=out_shape,
        scratch_shapes=[pltpu.VMEM((cq + 2 * B_HALF, 2 * B_W), BF16)],
        compiler_params=_cparams(3),
        name=f"dilated_branch{branch}",
    )(*args)
    if is_last:
        return res[0].reshape(t, B_W)
    return res[0].reshape(t, B_W), res[1].reshape(t, 2 * HEAD_DIM)


def _dilated_mixture(zb, batch, seq):
    state = None
    for g in range(B_BRANCHES):
        state = _dilated_branch(zb, g, batch, seq, state)
    return state


def _inproj_kernel(x_ref, g_ref, w_ref, z_ref):
    hb = _rms(x_ref[...], g_ref[...]).astype(BF16)
    z_ref[...] = _dot(hb, w_ref[...]).astype(BF16)


def _inproj(x, g, w, tm=512):
    t = x.shape[0]
    n_out = w.shape[1]
    return pl.pallas_call(
        _inproj_kernel,
        grid=(t // tm,),
        in_specs=[pl.BlockSpec((tm, D_MODEL), lambda i: (i, 0)),
                  _const_spec((1, D_MODEL)),
                  _const_spec((D_MODEL, n_out))],
        out_specs=pl.BlockSpec((tm, n_out), lambda i: (i, 0)),
        out_shape=jax.ShapeDtypeStruct((t, n_out), BF16),
        compiler_params=_cparams(1),
        name="inproj_c",
    )(x, g, w)


def _natten_kernel(q_ref, k0_ref, k1_ref, k2_ref, v0_ref, v1_ref, v2_ref, bias_ref, o_ref):
    k_refs = (k0_ref, k1_ref, k2_ref)
    v_refs = (v0_ref, v1_ref, v2_ref)
    lane = lax.broadcasted_iota(jnp.int32, (1, 2 * HEAD_DIM), 1)
    lo = lane < HEAD_DIM
    kb = k0_ref.shape[0]
    for pr in range(C_HEADS // 2):
        cols = slice(pr * 2 * HEAD_DIM, (pr + 1) * 2 * HEAD_DIM)
        q2 = q_ref[:, cols]
        ks = [r[:, cols] for r in k_refs]
        vs = [r[:, cols] for r in v_refs]
        o_pair = []
        for hh in range(2):
            qm = jnp.where(lo if hh == 0 else ~lo, q2, jnp.zeros_like(q2))
            s = jnp.concatenate([_dot_nt(qm, kk) for kk in ks], axis=-1) + bias_ref[pr * 2 + hh]
            m = jnp.max(s, axis=-1, keepdims=True)
            p = jnp.exp(s - m)
            l = jnp.sum(p, axis=-1, keepdims=True)
            pb = p.astype(BF16)
            o = _dot(pb[:, :kb], vs[0])
            for i in range(1, len(vs)):
                o = o + _dot(pb[:, i * kb:(i + 1) * kb], vs[i])
            o_pair.append(o / l)
        o_ref[:, cols] = jnp.where(lo, o_pair[0], o_pair[1]).astype(BF16)


def _natten_bias(rpb):
    nq = C_ROWS_Q * GRID_W
    nk = C_ROWS_K * GRID_W
    dr = (np.arange(nq) // GRID_W)[:, None]
    c = (np.arange(nq) % GRID_W)[:, None]
    kri = (np.arange(nk) // GRID_W)[None, :]
    kc = (np.arange(nk) % GRID_W)[None, :]
    cstart = np.clip(c - C_WIN_COLS // 2, 0, GRID_W - C_WIN_COLS)
    col_valid = (kc >= cstart) & (kc < cstart + C_WIN_COLS)
    cidx = np.clip(kc - c + C_WIN_COLS - 1, 0, 2 * C_WIN_COLS - 2)
    tables = []
    for off, first in ((0, 0 * dr), (-4, dr), (-8, 0 * dr + 4)):
        row_valid = (kri >= first) & (kri < first + C_WIN_ROWS)
        ridx = np.clip(off + kri - dr + C_WIN_ROWS - 1, 0, 2 * C_WIN_ROWS - 2)
        ridx_b, cidx_b = np.broadcast_arrays(ridx, cidx)
        vals = rpb[:, ridx_b, cidx_b].astype(F32)
        tables.append(jnp.where((row_valid & col_valid)[None], vals, NEG))
    return jnp.stack(tables)


def _natten(z, bias, batch, seq):
    t = batch * seq
    qb = C_ROWS_Q * GRID_W
    nrb = seq // qb
    nkb = C_ROWS_K // C_ROWS_Q

    def kspec(i, col):
        return pl.BlockSpec(
            (qb, C_W), lambda rb, b: (b * nrb + jnp.clip(rb - 1, 0, nrb - nkb) + i, col))

    def variant(rb):
        return jnp.where(rb == 0, 0, jnp.where(rb == nrb - 1, 2, 1))

    return pl.pallas_call(
        _natten_kernel,
        grid=(nrb, batch),
        in_specs=[pl.BlockSpec((qb, C_W), lambda rb, b: (b * nrb + rb, 0))]
        + [kspec(i, 1) for i in range(nkb)]
        + [kspec(i, 2) for i in range(nkb)]
        + [pl.BlockSpec((None, C_HEADS, qb, C_ROWS_K * GRID_W), lambda rb, b: (variant(rb), 0, 0, 0))],
        out_specs=pl.BlockSpec((qb, C_W), lambda rb, b: (b * nrb + rb, 0)),
        out_shape=jax.ShapeDtypeStruct((t, C_W), BF16),
        compiler_params=_cparams(2),
        name="natten",
    )(z, z, z, z, z, z, z, bias)


def _mem_kv_kernel(mem_ref, g_ref, w_ref, kv_ref):
    mb = _rms(mem_ref[...], g_ref[...]).astype(BF16)
    kv_ref[...] = _dot(mb, w_ref[...]).astype(BF16)


def _mem_kv(mem, g, w):
    batch, n_mem, _ = mem.shape
    return pl.pallas_call(
        _mem_kv_kernel,
        grid=(batch,),
        in_specs=[pl.BlockSpec((None, n_mem, D_MODEL), lambda b: (b, 0, 0)),
                  _const_spec((1, D_MODEL)),
                  _const_spec((D_MODEL, 2 * D_MODEL))],
        out_specs=pl.BlockSpec((None, n_mem, 2 * D_MODEL), lambda b: (b, 0, 0)),
        out_shape=jax.ShapeDtypeStruct((batch, n_mem, 2 * D_MODEL), BF16),
        compiler_params=_cparams(1),
        name="mem_kv",
    )(mem, g, w)


def _mix_xattn_kernel(*refs, n_mix):
    x_ref = refs[0]
    mix_refs = refs[1:1 + 2 * n_mix]
    g_ref, wq_ref, kv_ref, wo_ref, y_ref = refs[1 + 2 * n_mix:]
    x = x_ref[...]
    for i in range(n_mix):
        x = x + _dot(mix_refs[2 * i][...], mix_refs[2 * i + 1][...])
    q = _dot(_rms(x, g_ref[...]).astype(BF16), wq_ref[...]).astype(BF16)
    outs = []
    for h in range(X_HEADS):
        cols = slice(h * X_HEAD_DIM, (h + 1) * X_HEAD_DIM)
        s = _dot_nt(q[:, cols], kv_ref[:, cols])
        m = jnp.max(s, axis=-1, keepdims=True)
        p = jnp.exp(s - m)
        l = jnp.sum(p, axis=-1, keepdims=True)
        o = _dot(p.astype(BF16), kv_ref[:, D_MODEL + h * X_HEAD_DIM:D_MODEL + (h + 1) * X_HEAD_DIM])
        outs.append((o / l).astype(BF16))
    y_ref[...] = x + _dot(jnp.concatenate(outs, axis=-1), wo_ref[...])


def _mix_xattn(x, mixes, g, wq, kv, wo, seq, tm=512):
    t = x.shape[0]
    n_seq_tiles = seq // tm
    n_mem = kv.shape[1]
    in_specs = [pl.BlockSpec((tm, D_MODEL), lambda i: (i, 0))]
    args = [x]
    for o, w in mixes:
        in_specs += [pl.BlockSpec((tm, o.shape[1]), lambda i: (i, 0)), _const_spec(w.shape)]
        args += [o, w]
    in_specs += [_const_spec((1, D_MODEL)),
                 _const_spec((D_MODEL, D_MODEL)),
                 pl.BlockSpec((None, n_mem, 2 * D_MODEL), lambda i: (i // n_seq_tiles, 0, 0)),
                 _const_spec((D_MODEL, D_MODEL))]
    args += [g, wq, kv, wo]
    return pl.pallas_call(
        functools.partial(_mix_xattn_kernel, n_mix=len(mixes)),
        grid=(t // tm,),
        in_specs=in_specs,
        out_specs=pl.BlockSpec((tm, D_MODEL), lambda i: (i, 0)),
        out_shape=jax.ShapeDtypeStruct((t, D_MODEL), F32),
        compiler_params=_cparams(1),
        name="mix_xattn",
    )(*args)


def _swiglu_kernel(x_ref, g_ref, wg_ref, wu_ref, wd_ref, gf_ref, y_ref, *, final_norm):
    x = x_ref[...]
    hb = _rms(x, g_ref[...]).astype(BF16)
    gate = _dot(hb, wg_ref[...])
    up = _dot(hb, wu_ref[...])
    act = (gate / (1.0 + jnp.exp(-gate)) * up).astype(BF16)
    y = x + _dot(act, wd_ref[...])
    if final_norm:
        y = _rms(y, gf_ref[...])
    y_ref[...] = y


def _swiglu(x, g, wg, wu, wd, g_final, final_norm, tm=256):
    t = x.shape[0]
    return pl.pallas_call(
        functools.partial(_swiglu_kernel, final_norm=final_norm),
        grid=(t // tm,),
        in_specs=[pl.BlockSpec((tm, D_MODEL), lambda i: (i, 0)),
                  _const_spec((1, D_MODEL)),
                  _const_spec((D_MODEL, D_FF)),
                  _const_spec((D_MODEL, D_FF)),
                  _const_spec((D_FF, D_MODEL)),
                  _const_spec((1, D_MODEL))],
        out_specs=pl.BlockSpec((tm, D_MODEL), lambda i: (i, 0)),
        out_shape=jax.ShapeDtypeStruct((t, D_MODEL), F32),
        compiler_params=_cparams(1),
        name="swiglu",
    )(x, g, wg, wu, wd, g_final)


def _rope_tables(seq):
    tok = jnp.arange(seq, dtype=jnp.int32)
    row = (tok // GRID_W).astype(F32)
    col = (tok % GRID_W).astype(F32)
    axis_dim = HEAD_DIM // 2
    inv_freq = ROPE_THETA ** (-jnp.arange(0, axis_dim, 2, dtype=F32) / axis_dim)
    ang = jnp.concatenate([row[:, None] * inv_freq, col[:, None] * inv_freq], axis=-1)
    return jnp.cos(ang).T, jnp.sin(ang).T


def _prepare_ab(w_in, g_qn, g_kn, w_out):
    perm = np.concatenate([np.arange(0, HEAD_DIM, 2), np.arange(1, HEAD_DIM, 2)])
    cols = np.concatenate(
        [h * HEAD_DIM + perm for h in range(A_Q_HEADS + A_KV_HEADS)]
        + [np.arange(A_Q_W + A_KV_W, A_W)])
    waT = w_in[:, cols].T.astype(BF16)
    zbw = w_in[:, A_W:].reshape(D_MODEL, 3, B_BRANCHES, B_W)
    qscale = HEAD_DIM ** -0.5
    wb = jnp.stack([jnp.concatenate([zbw[:, 0, g] * qscale, zbw[:, 1, g], zbw[:, 2, g]], axis=1)
                    for g in range(B_BRANCHES)]).astype(BF16)
    gq = (g_qn[perm] * (qscale * LOG2E)).reshape(HEAD_DIM, 1).astype(F32)
    gk = g_kn[perm].reshape(HEAD_DIM, 1).astype(F32)
    return waT, wb, gq, gk, w_out[:A_Q_W].astype(BF16), w_out[A_Q_W:].astype(BF16)


def _trunk(x, mem, p):
    batch, seq, _ = x.shape
    t = batch * seq
    x = x.reshape(t, D_MODEL)
    depth = p["g_mix"].shape[0]
    for l in range(depth):
        g_mix = p["g_mix"][l].reshape(1, D_MODEL)
        if l % 2 == 0:
            e = l // 2
            waT, wb, gq, gk, wo_a, wo_b = _prepare_ab(p["w_in_ab"][e], p["g_qn"][e], p["g_kn"][e],
                                                      p["w_out_ab"][e])
            cosT, sinT = _rope_tables(seq)
            qT, k, vT, zb = _inproj_ab(x, g_mix, waT, wb, gq, gk, cosT, sinT, seq)
            oa = _gqa_attention(qT, k, vT, batch, seq)
            ob = _dilated_mixture(zb, batch, seq)
            mixes = [(oa, wo_a), (ob, wo_b)]
        else:
            o = l // 2
            w_in = p["w_in_c"][o]
            w_in = jnp.concatenate([w_in[:, :C_W] * HEAD_DIM ** -0.5, w_in[:, C_W:]], axis=1).astype(BF16)
            z = _inproj(x, g_mix, w_in)
            oc = _natten(z, _natten_bias(p["rpb_c"][o]), batch, seq)
            mixes = [(oc, p["w_out_c"][o].astype(BF16))]
        kv = _mem_kv(mem, p["g_mem"][l].reshape(1, D_MODEL), p["wkv_x"][l].astype(BF16))
        wq = (p["wq_x"][l] * X_HEAD_DIM ** -0.5).astype(BF16)
        x = _mix_xattn(x, mixes, p["g_xattn"][l].reshape(1, D_MODEL), wq, kv,
                       p["wo_x"][l].astype(BF16), seq)
        w_gu = p["w_gu"][l]
        x = _swiglu(x, p["g_ffn"][l].reshape(1, D_MODEL), w_gu[:, :D_FF].astype(BF16),
                    w_gu[:, D_FF:].astype(BF16), p["w_down"][l].astype(BF16),
                    p["g_final"].reshape(1, D_MODEL), final_norm=(l == depth - 1))
    return x.reshape(batch, seq, D_MODEL)


def kernel(x_prompt, x_sample, mem_prompt, mem_sample, g_mix, w_in_ab, g_qn, g_kn, w_out_ab, w_in_c,
           rpb_c, w_out_c, g_xattn, g_mem, wq_x, wkv_x, wo_x, g_ffn, w_gu, w_down, g_final):
    p = dict(g_mix=g_mix, w_in_ab=w_in_ab, g_qn=g_qn, g_kn=g_kn, w_out_ab=w_out_ab, w_in_c=w_in_c,
             rpb_c=rpb_c, w_out_c=w_out_c, g_xattn=g_xattn, g_mem=g_mem, wq_x=wq_x, wkv_x=wkv_x,
             wo_x=wo_x, g_ffn=g_ffn, w_gu=w_gu, w_down=w_down, g_final=g_final)
    return (_trunk(x_prompt, mem_prompt, p), _trunk(x_sample, mem_sample, p))
```

```python
import functools
import math

import numpy as np
import jax
import jax.numpy as jnp
from jax import lax
from jax.experimental import pallas as pl
from jax.experimental.pallas import tpu as pltpu

F32 = jnp.float32
BF16 = jnp.bfloat16

D_MODEL = 1024
GRID_W = 64
HEAD_DIM = 64
EPS = 1e-6
NEG = -1e30
LOG2E = math.log2(math.e)

A_Q_HEADS = 8
A_KV_HEADS = 2
A_GROUP = A_Q_HEADS // A_KV_HEADS
ROPE_THETA = 10000.0
A_Q_W = A_Q_HEADS * HEAD_DIM
A_KV_W = A_KV_HEADS * HEAD_DIM
A_W = A_Q_W + 2 * A_KV_W
A_TK = 256
A_V_ROWS = HEAD_DIM + 16
A_UNROLL = 4

B_PATTERNS = ((128, 1), (512, 4), (2048, 16))
B_BRANCHES = len(B_PATTERNS)
B_HEADS = 4
B_HALF = 64
B_W = B_HEADS * HEAD_DIM
B_SUB = 2 * B_HALF
B_KEYS = 4 * B_HALF
LSE_REP = 2 * HEAD_DIM // B_HEADS

C_HEADS = 16
C_WIN_ROWS = 8
C_WIN_COLS = 16
C_ROWS_Q = 4
C_ROWS_K = 12
C_W = C_HEADS * HEAD_DIM

X_HEADS = 4
X_HEAD_DIM = D_MODEL // X_HEADS
D_FF = 2816

VMEM_LIMIT = 56 * 1024 * 1024


def _cparams(n_axes):
    return pltpu.CompilerParams(dimension_semantics=("arbitrary",) * n_axes,
                                vmem_limit_bytes=VMEM_LIMIT)


def _rms(x, g):
    ms = jnp.mean(x * x, axis=-1, keepdims=True)
    return x * lax.rsqrt(ms + EPS) * g


def _dot(a, b):
    return jnp.dot(a, b, preferred_element_type=F32)


def _dot_nt(a, b):
    return lax.dot_general(a, b, (((1,), (1,)), ((), ())), preferred_element_type=F32)


def _const_spec(shape):
    zeros = (0,) * len(shape)
    return pl.BlockSpec(shape, lambda *_: zeros)


def _inproj_ab_kernel(x_ref, g_ref, waT_ref, wb_ref, gq_ref, gk_ref, cos_ref, sin_ref,
                      qT_ref, k_ref, vT_ref, zb_ref):
    hb = _rms(x_ref[...], g_ref[...]).astype(BF16)
    zT = _dot_nt(waT_ref[...], hb)
    cos = cos_ref[...]
    sin = sin_ref[...]
    half = HEAD_DIM // 2

    def norm_rope(zh, gcol):
        ms = jnp.mean(zh * zh, axis=0, keepdims=True)
        y = zh * lax.rsqrt(ms + EPS) * gcol
        yr, yi = y[:half], y[half:]
        return jnp.concatenate([yr * cos - yi * sin, yr * sin + yi * cos], axis=0)

    for h in range(A_Q_HEADS):
        sl = slice(h * HEAD_DIM, (h + 1) * HEAD_DIM)
        qT_ref[sl, :] = norm_rope(zT[sl], gq_ref[...]).astype(BF16)
    kT = jnp.concatenate(
        [norm_rope(zT[A_Q_W + h * HEAD_DIM:A_Q_W + (h + 1) * HEAD_DIM], gk_ref[...])
         for h in range(A_KV_HEADS)], axis=0)
    k_nat = kT.T
    for h in range(A_KV_HEADS):
        k_ref[h] = k_nat[:, h * HEAD_DIM:(h + 1) * HEAD_DIM].astype(BF16)
    for h in range(A_KV_HEADS):
        for c in range(vT_ref.shape[1]):
            vT_ref[h, c] = zT[A_Q_W + A_KV_W + h * HEAD_DIM:A_Q_W + A_KV_W + (h + 1) * HEAD_DIM,
                              c * A_TK:(c + 1) * A_TK].astype(BF16)
    for g in range(B_BRANCHES):
        zb_ref[g] = _dot(hb, wb_ref[g]).astype(BF16)


def _inproj_ab(x, g_mix, waT, wb, gq, gk, cosT, sinT, seq, tm=512):
    t = x.shape[0]
    n_seq_tiles = seq // tm
    return pl.pallas_call(
        _inproj_ab_kernel,
        grid=(t // tm,),
        in_specs=[
            pl.BlockSpec((tm, D_MODEL), lambda i: (i, 0)),
            _const_spec((1, D_MODEL)),
            _const_spec((A_W, D_MODEL)),
            _const_spec((B_BRANCHES, D_MODEL, 3 * B_W)),
            _const_spec((HEAD_DIM, 1)),
            _const_spec((HEAD_DIM, 1)),
            pl.BlockSpec((HEAD_DIM // 2, tm), lambda i: (0, i % n_seq_tiles)),
            pl.BlockSpec((HEAD_DIM // 2, tm), lambda i: (0, i % n_seq_tiles)),
        ],
        out_specs=[
            pl.BlockSpec((A_Q_W, tm), lambda i: (0, i)),
            pl.BlockSpec((A_KV_HEADS, tm, HEAD_DIM), lambda i: (0, i, 0)),
            pl.BlockSpec((A_KV_HEADS, tm // A_TK, HEAD_DIM, A_TK), lambda i: (0, i, 0, 0)),
            pl.BlockSpec((B_BRANCHES, tm, 3 * B_W), lambda i: (0, i, 0)),
        ],
        out_shape=[
            jax.ShapeDtypeStruct((A_Q_W, t), BF16),
            jax.ShapeDtypeStruct((A_KV_HEADS, t, HEAD_DIM), BF16),
            jax.ShapeDtypeStruct((A_KV_HEADS, t // A_TK, HEAD_DIM, A_TK), BF16),
            jax.ShapeDtypeStruct((B_BRANCHES, t, 3 * B_W), BF16),
        ],
        compiler_params=_cparams(1),
        name="inproj_ab",
    )(x, g_mix, waT, wb, gq, gk, cosT, sinT)


def _gqa_kernel(qT_ref, k_ref, vT_ref, o_ref, s_scr, p_scr, cm_scr, alpha_scr, m_scr, acc_scr):
    tq = qT_ref.shape[1]
    tk = A_TK
    n_chunks = vT_ref.shape[0]
    ones = jnp.ones((A_V_ROWS - HEAD_DIM, tk), BF16)

    def scores(chunk, slot):
        k = k_ref[pl.ds(pl.multiple_of(chunk * tk, tk), tk), :]
        for h in range(A_GROUP):
            s = _dot(k, qT_ref[h * HEAD_DIM:(h + 1) * HEAD_DIM, :])
            s_scr[slot, h] = s
            cm_scr[slot, h] = jnp.max(s.reshape(tk // 8, 8, tq), axis=0)

    def probs(slot):
        for h in range(A_GROUP):
            m_old = m_scr[h]
            m_new = jnp.maximum(m_old, jnp.max(cm_scr[slot, h], axis=0, keepdims=True))
            alpha_scr[slot, h] = jnp.exp2(m_old - m_new)
            m_scr[h] = m_new
            p_scr[slot, h] = jnp.exp2(s_scr[slot, h] - m_new).astype(BF16)

    def accumulate(chunk, slot):
        vT_aug = jnp.concatenate([vT_ref[chunk], ones], axis=0)
        for h in range(A_GROUP):
            acc_scr[h] = alpha_scr[slot, h] * acc_scr[h] + _dot(vT_aug, p_scr[slot, h])

    def steady(c, count):
        for u in range(count):
            accumulate(c + u, u % 2)
            probs((u + 1) % 2)
            scores(c + u + 2, u % 2)

    m_scr[...] = jnp.full(m_scr.shape, NEG, F32)
    acc_scr[...] = jnp.zeros(acc_scr.shape, F32)
    scores(0, 0)
    scores(1, 1)
    probs(0)
    n_loop = (n_chunks - 2) // A_UNROLL

    def body(i, carry):
        steady(i * A_UNROLL, A_UNROLL)
        return carry

    lax.fori_loop(0, n_loop, body, 0)
    steady(n_loop * A_UNROLL, (n_chunks - 2) % A_UNROLL)
    accumulate(n_chunks - 2, 0)
    probs(1)
    accumulate(n_chunks - 1, 1)
    oT = jnp.concatenate(
        [acc_scr[h, :HEAD_DIM, :] / acc_scr[h, HEAD_DIM:HEAD_DIM + 1, :] for h in range(A_GROUP)], axis=0)
    o_ref[...] = oT.T.astype(BF16)


def _gqa_attention(qT, k, vT, batch, seq, tq=256):
    t = batch * seq
    nq = seq // tq
    k4 = k.reshape(A_KV_HEADS, batch, seq, HEAD_DIM)
    v5 = vT.reshape(A_KV_HEADS, batch, seq // A_TK, HEAD_DIM, A_TK)
    return pl.pallas_call(
        _gqa_kernel,
        grid=(batch, A_KV_HEADS, nq),
        in_specs=[
            pl.BlockSpec((A_GROUP * HEAD_DIM, tq), lambda b, g, i: (g, b * nq + i)),
            pl.BlockSpec((None, None, seq, HEAD_DIM), lambda b, g, i: (g, b, 0, 0)),
            pl.BlockSpec((None, None, seq // A_TK, HEAD_DIM, A_TK), lambda b, g, i: (g, b, 0, 0, 0)),
        ],
        out_specs=pl.BlockSpec((tq, A_GROUP * HEAD_DIM), lambda b, g, i: (b * nq + i, g)),
        out_shape=jax.ShapeDtypeStruct((t, A_Q_W), BF16),
        scratch_shapes=[pltpu.VMEM((2, A_GROUP, A_TK, tq), F32),
                        pltpu.VMEM((2, A_GROUP, A_TK, tq), BF16),
                        pltpu.VMEM((2, A_GROUP, 8, tq), F32),
                        pltpu.VMEM((2, A_GROUP, 1, tq), F32),
                        pltpu.VMEM((A_GROUP, 1, tq), F32),
                        pltpu.VMEM((A_GROUP, A_V_ROWS, tq), F32)],
        compiler_params=_cparams(3),
        name="gqa_attention",
    )(qT, k4, v5)


def _dilated_kernel(*refs, seq_len, cq, has_prev, is_last):
    n_in = 6 if has_prev else 4
    main_ref, prev_ref, next_ref, bias_ref = refs[:4]
    po_ref, plse_ref = refs[4:n_in] if has_prev else (None, None)
    o_ref = refs[n_in]
    lse_ref = None if is_last else refs[n_in + 1]
    kv_scr = refs[-1]

    n = pl.program_id(2)
    kv_scr[0:B_HALF, :] = prev_ref[:, B_W:]
    kv_scr[B_HALF:B_HALF + cq, :] = main_ref[:, B_W:]
    kv_scr[B_HALF + cq:, :] = next_ref[:, B_W:]

    pair_w = 2 * HEAD_DIM
    lane = lax.broadcasted_iota(jnp.int32, (1, pair_w), 1)
    lo = lane < HEAD_DIM
    key_col = lax.broadcasted_iota(jnp.int32, (1, B_KEYS), 1)

    for j in range(cq // B_SUB):
        key_pos = key_col + (n * cq + j * B_SUB - B_HALF)
        edge = jnp.where((key_pos >= 0) & (key_pos < seq_len), 0.0, NEG).astype(F32)
        rows = slice(j * B_SUB, (j + 1) * B_SUB)
        krows = slice(j * B_SUB, j * B_SUB + B_KEYS)
        lses = []
        for pr in range(B_HEADS // 2):
            cols = slice(pr * pair_w, (pr + 1) * pair_w)
            q2 = main_ref[rows, cols]
            k2 = kv_scr[krows, cols]
            v2 = kv_scr[krows, B_W + pr * pair_w:B_W + (pr + 1) * pair_w]
            o_pair, w_prev = [], []
            for hh in range(2):
                head = pr * 2 + hh
                qm = jnp.where(lo if hh == 0 else ~lo, q2, jnp.zeros_like(q2))
                s = _dot_nt(qm, k2) + bias_ref[head] + edge
                m = jnp.max(s, axis=-1, keepdims=True)
                p = jnp.exp(s - m)
                l = jnp.sum(p, axis=-1, keepdims=True)
                o = _dot(p.astype(BF16), v2) / l
                lse = m + jnp.log(l)
                if has_prev:
                    lse_p = plse_ref[rows, head * LSE_REP:head * LSE_REP + 1]
                    mx = jnp.maximum(lse_p, lse)
                    wp = jnp.exp(lse_p - mx)
                    wc = jnp.exp(lse - mx)
                    o = o * (wc / (wp + wc))
                    w_prev.append(wp / (wp + wc))
                    lse = mx + jnp.log(wp + wc)
                o_pair.append(o)
                lses.append(lse)
            o2 = jnp.where(lo, o_pair[0], o_pair[1])
            if has_prev:
                o2 = o2 + jnp.where(lo, w_prev[0], w_prev[1]) * po_ref[rows, cols]
            o_ref[rows, cols] = o2.astype(o_ref.dtype)
        if lse_ref is not None:
            tile = lses[B_HEADS - 1]
            for head in range(B_HEADS - 2, -1, -1):
                tile = jnp.where(lane < (head + 1) * LSE_REP, lses[head], tile)
            lse_ref[rows, :] = tile


def _dilated_bias(branch):
    _, dil = B_PATTERNS[branch]
    slopes = np.exp2(-8.0 * np.arange(1, B_BRANCHES * B_HEADS + 1, dtype=np.float64)
                     / (B_BRANCHES * B_HEADS)).reshape(B_BRANCHES, B_HEADS)[branch]
    a = np.arange(B_SUB)[:, None]
    c = np.arange(B_KEYS)[None, :]
    rel = (c - B_HALF) - a
    bias = -slopes[:, None, None] * (np.abs(rel) * dil).astype(np.float64)[None]
    bias = np.where((np.abs(rel) <= B_HALF)[None], bias, NEG)
    return jnp.asarray(bias, F32)


def _dilated_branch(zb, branch, batch, seq, prev):
    _, dil = B_PATTERNS[branch]
    t = batch * seq
    seq_len = seq // dil
    cq = min(512, seq_len)
    nchunk = seq_len // cq
    nb64 = seq_len // B_HALF
    per = cq // B_HALF
    is_last = branch == B_BRANCHES - 1
    has_prev = prev is not None
    zv = zb.reshape(B_BRANCHES, batch, seq_len, dil * 3 * B_W)
    bias = _dilated_bias(branch)

    def view(a, w):
        return a.reshape(batch, seq_len, dil * w)

    in_specs = [
        pl.BlockSpec((None, None, cq, 3 * B_W), lambda b, r, n: (branch, b, n, r)),
        pl.BlockSpec((None, None, B_HALF, 3 * B_W),
                     lambda b, r, n: (branch, b, jnp.maximum(n * per - 1, 0), r)),
        pl.BlockSpec((None, None, B_HALF, 3 * B_W),
                     lambda b, r, n: (branch, b, jnp.minimum((n + 1) * per, nb64 - 1), r)),
        _const_spec((B_HEADS, B_SUB, B_KEYS)),
    ]
    args = [zv, zv, zv, bias]
    if has_prev:
        in_specs += [pl.BlockSpec((None, cq, B_W), lambda b, r, n: (b, n, r)),
                     pl.BlockSpec((None, cq, 2 * HEAD_DIM), lambda b, r, n: (b, n, r))]
        args += [view(prev[0], B_W), view(prev[1], 2 * HEAD_DIM)]
    o_dtype = BF16 if is_last else F32
    out_specs = [pl.BlockSpec((None, cq, B_W), lambda b, r, n: (b, n, r))]
    out_shape = [jax.ShapeDtypeStruct((batch, seq_len, dil * B_W), o_dtype)]
    if not is_last:
        out_specs.append(pl.BlockSpec((None, cq, 2 * HEAD_DIM), lambda b, r, n: (b, n, r)))
        out_shape.append(jax.ShapeDtypeStruct((batch, seq_len, dil * 2 * HEAD_DIM), F32))
    res = pl.pallas_call(
        functools.partial(_dilated_kernel, seq_len=seq_len, cq=cq, has_prev=has_prev, is_last=is_last),
        grid=(batch, dil, nchunk),
        in_specs=in_specs,
        out_specs=out_specs,
        out_shape=out_shape,
        scratch_shapes=[pltpu.VMEM((cq + 2 * B_HALF, 2 * B_W), BF16)],
        compiler_params=_cparams(3),
        name=f"dilated_branch{branch}",
    )(*args)
    if is_last:
        return res[0].reshape(t, B_W)
    return res[0].reshape(t, B_W), res[1].reshape(t, 2 * HEAD_DIM)


def _dilated_mixture(zb, batch, seq):
    state = None
    for g in range(B_BRANCHES):
        state = _dilated_branch(zb, g, batch, seq, state)
    return state


def _inproj_kernel(x_ref, g_ref, w_ref, z_ref):
    hb = _rms(x_ref[...], g_ref[...]).astype(BF16)
    z_ref[...] = _dot(hb, w_ref[...]).astype(BF16)


def _inproj(x, g, w, tm=512):
    t = x.shape[0]
    n_out = w.shape[1]
    return pl.pallas_call(
        _inproj_kernel,
        grid=(t // tm,),
        in_specs=[pl.BlockSpec((tm, D_MODEL), lambda i: (i, 0)),
                  _const_spec((1, D_MODEL)),
                  _const_spec((D_MODEL, n_out))],
        out_specs=pl.BlockSpec((tm, n_out), lambda i: (i, 0)),
        out_shape=jax.ShapeDtypeStruct((t, n_out), BF16),
        compiler_params=_cparams(1),
        name="inproj_c",
    )(x, g, w)


def _natten_kernel(q_ref, k0_ref, k1_ref, k2_ref, v0_ref, v1_ref, v2_ref, bias_ref, o_ref):
    k_refs = (k0_ref, k1_ref, k2_ref)
    v_refs = (v0_ref, v1_ref, v2_ref)
    lane = lax.broadcasted_iota(jnp.int32, (1, 2 * HEAD_DIM), 1)
    lo = lane < HEAD_DIM
    kb = k0_ref.shape[0]
    for pr in range(C_HEADS // 2):
        cols = slice(pr * 2 * HEAD_DIM, (pr + 1) * 2 * HEAD_DIM)
        q2 = q_ref[:, cols]
        ks = [r[:, cols] for r in k_refs]
        vs = [r[:, cols] for r in v_refs]
        o_pair = []
        for hh in range(2):
            qm = jnp.where(lo if hh == 0 else ~lo, q2, jnp.zeros_like(q2))
            s = jnp.concatenate([_dot_nt(qm, kk) for kk in ks], axis=-1) + bias_ref[pr * 2 + hh]
            m = jnp.max(s, axis=-1, keepdims=True)
            p = jnp.exp(s - m)
            l = jnp.sum(p, axis=-1, keepdims=True)
            pb = p.astype(BF16)
            o = _dot(pb[:, :kb], vs[0])
            for i in range(1, len(vs)):
                o = o + _dot(pb[:, i * kb:(i + 1) * kb], vs[i])
            o_pair.append(o / l)
        o_ref[:, cols] = jnp.where(lo, o_pair[0], o_pair[1]).astype(BF16)


def _natten_bias(rpb):
    c = np.arange(GRID_W)[:, None]
    kc = np.arange(GRID_W)[None, :]
    cstart = np.clip(c - C_WIN_COLS // 2, 0, GRID_W - C_WIN_COLS)
    col_valid = (kc >= cstart) & (kc < cstart + C_WIN_COLS)
    padded = jnp.pad(rpb.astype(F32), ((0, 0), (0, 0), (GRID_W, GRID_W)))
    base = GRID_W + C_WIN_COLS - 1
    blocks = jnp.stack([padded[:, :, base - q:base - q + GRID_W] for q in range(GRID_W)], axis=2)
    blocks = jnp.where(col_valid[None, None], blocks, NEG)
    masked = jnp.full((rpb.shape[0], GRID_W, GRID_W), NEG, F32)
    tables = []
    for off, first in ((0, lambda dr: 0), (-4, lambda dr: dr), (-8, lambda dr: 4)):
        rows = []
        for dr in range(C_ROWS_Q):
            rows.append(jnp.concatenate(
                [blocks[:, off + kri - dr + C_WIN_ROWS - 1]
                 if first(dr) <= kri < first(dr) + C_WIN_ROWS else masked
                 for kri in range(C_ROWS_K)], axis=-1))
        tables.append(jnp.concatenate(rows, axis=1))
    return jnp.stack(tables)


def _natten(z, bias, batch, seq):
    t = batch * seq
    qb = C_ROWS_Q * GRID_W
    nrb = seq // qb
    nkb = C_ROWS_K // C_ROWS_Q

    def kspec(i, col):
        return pl.BlockSpec(
            (qb, C_W), lambda rb, b: (b * nrb + jnp.clip(rb - 1, 0, nrb - nkb) + i, col))

    def variant(rb):
        return jnp.where(rb == 0, 0, jnp.where(rb == nrb - 1, 2, 1))

    return pl.pallas_call(
        _natten_kernel,
        grid=(nrb, batch),
        in_specs=[pl.BlockSpec((qb, C_W), lambda rb, b: (b * nrb + rb, 0))]
        + [kspec(i, 1) for i in range(nkb)]
        + [kspec(i, 2) for i in range(nkb)]
        + [pl.BlockSpec((None, C_HEADS, qb, C_ROWS_K * GRID_W), lambda rb, b: (variant(rb), 0, 0, 0))],
        out_specs=pl.BlockSpec((qb, C_W), lambda rb, b: (b * nrb + rb, 0)),
        out_shape=jax.ShapeDtypeStruct((t, C_W), BF16),
        compiler_params=_cparams(2),
        name="natten",
    )(z, z, z, z, z, z, z, bias)


def _mem_kv_kernel(mem_ref, g_ref, w_ref, kv_ref):
    mb = _rms(mem_ref[...], g_ref[...]).astype(BF16)
    kv_ref[...] = _dot(mb, w_ref[...]).astype(BF16)


def _mem_kv(mem, g, w):
    batch, n_mem, _ = mem.shape
    return pl.pallas_call(
        _mem_kv_kernel,
        grid=(batch,),
        in_specs=[pl.BlockSpec((None, n_mem, D_MODEL), lambda b: (b, 0, 0)),
                  _const_spec((1, D_MODEL)),
                  _const_spec((D_MODEL, 2 * D_MODEL))],
        out_specs=pl.BlockSpec((None, n_mem, 2 * D_MODEL), lambda b: (b, 0, 0)),
        out_shape=jax.ShapeDtypeStruct((batch, n_mem, 2 * D_MODEL), BF16),
        compiler_params=_cparams(1),
        name="mem_kv",
    )(mem, g, w)


def _mix_xattn_kernel(*refs, n_mix):
    x_ref = refs[0]
    mix_refs = refs[1:1 + 2 * n_mix]
    g_ref, wq_ref, kv_ref, wo_ref, y_ref = refs[1 + 2 * n_mix:]
    x = x_ref[...]
    for i in range(n_mix):
        x = x + _dot(mix_refs[2 * i][...], mix_refs[2 * i + 1][...])
    q = _dot(_rms(x, g_ref[...]).astype(BF16), wq_ref[...]).astype(BF16)
    outs = []
    for h in range(X_HEADS):
        cols = slice(h * X_HEAD_DIM, (h + 1) * X_HEAD_DIM)
        s = _dot_nt(q[:, cols], kv_ref[:, cols])
        m = jnp.max(s, axis=-1, keepdims=True)
        p = jnp.exp(s - m)
        l = jnp.sum(p, axis=-1, keepdims=True)
        o = _dot(p.astype(BF16), kv_ref[:, D_MODEL + h * X_HEAD_DIM:D_MODEL + (h + 1) * X_HEAD_DIM])
        outs.append((o / l).astype(BF16))
    y_ref[...] = x + _dot(jnp.concatenate(outs, axis=-1), wo_ref[...])


def _mix_xattn(x, mixes, g, wq, kv, wo, seq, tm=512):
    t = x.shape[0]
    n_seq_tiles = seq // tm
    n_mem = kv.shape[1]
    in_specs = [pl.BlockSpec((tm, D_MODEL), lambda i: (i, 0))]
    args = [x]
    for o, w in mixes:
        in_specs += [pl.BlockSpec((tm, o.shape[1]), lambda i: (i, 0)), _const_spec(w.shape)]
        args += [o, w]
    in_specs += [_const_spec((1, D_MODEL)),
                 _const_spec((D_MODEL, D_MODEL)),
                 pl.BlockSpec((None, n_mem, 2 * D_MODEL), lambda i: (i // n_seq_tiles, 0, 0)),
                 _const_spec((D_MODEL, D_MODEL))]
    args += [g, wq, kv, wo]
    return pl.pallas_call(
        functools.partial(_mix_xattn_kernel, n_mix=len(mixes)),
        grid=(t // tm,),
        in_specs=in_specs,
        out_specs=pl.BlockSpec((tm, D_MODEL), lambda i: (i, 0)),
        out_shape=jax.ShapeDtypeStruct((t, D_MODEL), F32),
        compiler_params=_cparams(1),
        name="mix_xattn",
    )(*args)


def _swiglu_kernel(x_ref, g_ref, wg_ref, wu_ref, wd_ref, gf_ref, y_ref, *, final_norm):
    x = x_ref[...]
    hb = _rms(x, g_ref[...]).astype(BF16)
    gate = _dot(hb, wg_ref[...])
    up = _dot(hb, wu_ref[...])
    act = (gate / (1.0 + jnp.exp(-gate)) * up).astype(BF16)
    y = x + _dot(act, wd_ref[...])
    if final_norm:
        y = _rms(y, gf_ref[...])
    y_ref[...] = y


def _swiglu(x, g, wg, wu, wd, g_final, final_norm, tm=256):
    t = x.shape[0]
    return pl.pallas_call(
        functools.partial(_swiglu_kernel, final_norm=final_norm),
        grid=(t // tm,),
        in_specs=[pl.BlockSpec((tm, D_MODEL), lambda i: (i, 0)),
                  _const_spec((1, D_MODEL)),
                  _const_spec((D_MODEL, D_FF)),
                  _const_spec((D_MODEL, D_FF)),
                  _const_spec((D_FF, D_MODEL)),
                  _const_spec((1, D_MODEL))],
        out_specs=pl.BlockSpec((tm, D_MODEL), lambda i: (i, 0)),
        out_shape=jax.ShapeDtypeStruct((t, D_MODEL), F32),
        compiler_params=_cparams(1),
        name="swiglu",
    )(x, g, wg, wu, wd, g_final)


def _rope_tables(seq):
    tok = jnp.arange(seq, dtype=jnp.int32)
    row = (tok // GRID_W).astype(F32)
    col = (tok % GRID_W).astype(F32)
    axis_dim = HEAD_DIM // 2
    inv_freq = ROPE_THETA ** (-jnp.arange(0, axis_dim, 2, dtype=F32) / axis_dim)
    ang = jnp.concatenate([row[:, None] * inv_freq, col[:, None] * inv_freq], axis=-1)
    return jnp.cos(ang).T, jnp.sin(ang).T


def _prepare_ab(w_in, g_qn, g_kn, w_out):
    perm = np.concatenate([np.arange(0, HEAD_DIM, 2), np.arange(1, HEAD_DIM, 2)])
    cols = np.concatenate(
        [h * HEAD_DIM + perm for h in range(A_Q_HEADS + A_KV_HEADS)]
        + [np.arange(A_Q_W + A_KV_W, A_W)])
    waT = w_in[:, cols].T.astype(BF16)
    zbw = w_in[:, A_W:].reshape(D_MODEL, 3, B_BRANCHES, B_W)
    qscale = HEAD_DIM ** -0.5
    wb = jnp.stack([jnp.concatenate([zbw[:, 0, g] * qscale, zbw[:, 1, g], zbw[:, 2, g]], axis=1)
                    for g in range(B_BRANCHES)]).astype(BF16)
    gq = (g_qn[perm] * (qscale * LOG2E)).reshape(HEAD_DIM, 1).astype(F32)
    gk = g_kn[perm].reshape(HEAD_DIM, 1).astype(F32)
    return waT, wb, gq, gk, w_out[:A_Q_W].astype(BF16), w_out[A_Q_W:].astype(BF16)


def _prepare_layers(p):
    depth = p["g_mix"].shape[0]
    row = lambda g: g.reshape(1, D_MODEL)
    layers = []
    for l in range(depth):
        lay = dict(g_mix=row(p["g_mix"][l]), g_xattn=row(p["g_xattn"][l]), g_mem=row(p["g_mem"][l]),
                   g_ffn=row(p["g_ffn"][l]),
                   wq=(p["wq_x"][l] * X_HEAD_DIM ** -0.5).astype(BF16), wkv=p["wkv_x"][l].astype(BF16),
                   wo=p["wo_x"][l].astype(BF16), wg=p["w_gu"][l][:, :D_FF].astype(BF16),
                   wu=p["w_gu"][l][:, D_FF:].astype(BF16), wd=p["w_down"][l].astype(BF16))
        i = l // 2
        if l % 2 == 0:
            lay["ab"] = _prepare_ab(p["w_in_ab"][i], p["g_qn"][i], p["g_kn"][i], p["w_out_ab"][i])
        else:
            w_in = p["w_in_c"][i]
            lay["w_in_c"] = jnp.concatenate(
                [w_in[:, :C_W] * HEAD_DIM ** -0.5, w_in[:, C_W:]], axis=1).astype(BF16)
            lay["bias_c"] = _natten_bias(p["rpb_c"][i])
            lay["w_out_c"] = p["w_out_c"][i].astype(BF16)
        layers.append(lay)
    return layers


def _trunk(x, mem, layers, g_final):
    batch, seq, _ = x.shape
    x = x.reshape(batch * seq, D_MODEL)
    for l, lay in enumerate(layers):
        if l % 2 == 0:
            waT, wb, gq, gk, wo_a, wo_b = lay["ab"]
            cosT, sinT = _rope_tables(seq)
            qT, k, vT, zb = _inproj_ab(x, lay["g_mix"], waT, wb, gq, gk, cosT, sinT, seq)
            mixes = [(_gqa_attention(qT, k, vT, batch, seq), wo_a),
                     (_dilated_mixture(zb, batch, seq), wo_b)]
        else:
            z = _inproj(x, lay["g_mix"], lay["w_in_c"])
            mixes = [(_natten(z, lay["bias_c"], batch, seq), lay["w_out_c"])]
        kv = _mem_kv(mem, lay["g_mem"], lay["wkv"])
        x = _mix_xattn(x, mixes, lay["g_xattn"], lay["wq"], kv, lay["wo"], seq)
        x = _swiglu(x, lay["g_ffn"], lay["wg"], lay["wu"], lay["wd"], g_final,
                    final_norm=(l == len(layers) - 1))
    return x.reshape(batch, seq, D_MODEL)


def kernel(x_prompt, x_sample, mem_prompt, mem_sample, g_mix, w_in_ab, g_qn, g_kn, w_out_ab, w_in_c,
           rpb_c, w_out_c, g_xattn, g_mem, wq_x, wkv_x, wo_x, g_ffn, w_gu, w_down, g_final):
    layers = _prepare_layers(dict(
        g_mix=g_mix, w_in_ab=w_in_ab, g_qn=g_qn, g_kn=g_kn, w_out_ab=w_out_ab, w_in_c=w_in_c,
        rpb_c=rpb_c, w_out_c=w_out_c, g_xattn=g_xattn, g_mem=g_mem, wq_x=wq_x, wkv_x=wkv_x,
        wo_x=wo_x, g_ffn=g_ffn, w_gu=w_gu, w_down=w_down))
    g_final = g_final.reshape(1, D_MODEL)
    return (_trunk(x_prompt, mem_prompt, layers, g_final), _trunk(x_sample, mem_sample, layers, g_final))
```

```python
import functools
import math

import numpy as np
import jax
import jax.numpy as jnp
from jax import lax
from jax.experimental import pallas as pl
from jax.experimental.pallas import tpu as pltpu

F32 = jnp.float32
BF16 = jnp.bfloat16

D_MODEL = 1024
GRID_W = 64
HEAD_DIM = 64
EPS = 1e-6
NEG = -1e30
LOG2E = math.log2(math.e)
TOKEN_TILE = 512

A_Q_HEADS = 8
A_KV_HEADS = 2
A_GROUP = A_Q_HEADS // A_KV_HEADS
ROPE_THETA = 10000.0
A_Q_W = A_Q_HEADS * HEAD_DIM
A_KV_W = A_KV_HEADS * HEAD_DIM
A_W = A_Q_W + 2 * A_KV_W
A_TK = 256
A_V_ROWS = HEAD_DIM + 16
A_UNROLL = 6

B_PATTERNS = ((128, 1), (512, 4), (2048, 16))
B_BRANCHES = len(B_PATTERNS)
B_HEADS = 4
B_HALF = 64
B_W = B_HEADS * HEAD_DIM
B_SUB = 2 * B_HALF
B_KEYS = 4 * B_HALF
LSE_REP = 2 * HEAD_DIM // B_HEADS
B_R = B_W + 4 * HEAD_DIM

C_HEADS = 16
C_WIN_ROWS = 8
C_WIN_COLS = 16
C_ROWS_Q = 4
C_ROWS_K = 12
C_W = C_HEADS * HEAD_DIM

X_HEADS = 4
X_HEAD_DIM = D_MODEL // X_HEADS
D_FF = 2816

VMEM_LIMIT = 56 * 1024 * 1024


def _cparams(n_axes):
    return pltpu.CompilerParams(dimension_semantics=("arbitrary",) * n_axes,
                                vmem_limit_bytes=VMEM_LIMIT)


def _rms(x, g):
    ms = jnp.mean(x * x, axis=-1, keepdims=True)
    return x * lax.rsqrt(ms + EPS) * g


def _dot(a, b):
    return jnp.dot(a, b, preferred_element_type=F32)


def _dot_nt(a, b):
    return lax.dot_general(a, b, (((1,), (1,)), ((), ())), preferred_element_type=F32)


def _const_spec(shape):
    zeros = (0,) * len(shape)
    return pl.BlockSpec(shape, lambda *_: zeros)


def _inproj_ab_kernel(x_ref, g_ref, waT_ref, wb_ref, perm_ref, gq_ref, gk_ref, cos_ref, sin_ref,
                      qT_ref, k_ref, vT_ref, zb_ref):
    hb = _rms(x_ref[...], g_ref[...]).astype(BF16)
    zT = _dot_nt(waT_ref[...], hb)
    cos = cos_ref[...]
    sin = sin_ref[...]
    half = HEAD_DIM // 2

    def norm_rope(zh, gcol):
        ms = jnp.mean(zh * zh, axis=0, keepdims=True)
        y = zh * lax.rsqrt(ms + EPS) * gcol
        yr, yi = y[:half], y[half:]
        return jnp.concatenate([yr * cos - yi * sin, yr * sin + yi * cos], axis=0)

    for h in range(A_Q_HEADS):
        sl = slice(h * HEAD_DIM, (h + 1) * HEAD_DIM)
        qT_ref[sl, :] = norm_rope(zT[sl], gq_ref[...]).astype(BF16)
    kT = jnp.concatenate(
        [norm_rope(zT[A_Q_W + h * HEAD_DIM:A_Q_W + (h + 1) * HEAD_DIM], gk_ref[...])
         for h in range(A_KV_HEADS)], axis=0)
    k_nat = kT.T
    for h in range(A_KV_HEADS):
        k_ref[h] = k_nat[:, h * HEAD_DIM:(h + 1) * HEAD_DIM].astype(BF16)
    for h in range(A_KV_HEADS):
        for c in range(vT_ref.shape[1]):
            vT_ref[h, c] = zT[A_Q_W + A_KV_W + h * HEAD_DIM:A_Q_W + A_KV_W + (h + 1) * HEAD_DIM,
                              c * A_TK:(c + 1) * A_TK].astype(BF16)
    zb_ref[0] = _dot(hb, wb_ref[0]).astype(BF16)
    for g in range(1, B_BRANCHES):
        zb_ref[g] = _dot(perm_ref[g - 1], _dot(hb, wb_ref[g]).astype(BF16)).astype(BF16)


def _inproj_ab(x, g_mix, waT, wb, perm, gq, gk, cosT, sinT, seq):
    t = x.shape[0]
    tm = TOKEN_TILE
    n_seq_tiles = seq // tm
    return pl.pallas_call(
        _inproj_ab_kernel,
        grid=(t // tm,),
        in_specs=[
            pl.BlockSpec((tm, D_MODEL), lambda i: (i, 0)),
            _const_spec((1, D_MODEL)),
            _const_spec((A_W, D_MODEL)),
            _const_spec((B_BRANCHES, D_MODEL, 3 * B_W)),
            _const_spec((B_BRANCHES - 1, tm, tm)),
            _const_spec((HEAD_DIM, 1)),
            _const_spec((HEAD_DIM, 1)),
            pl.BlockSpec((HEAD_DIM // 2, tm), lambda i: (0, i % n_seq_tiles)),
            pl.BlockSpec((HEAD_DIM // 2, tm), lambda i: (0, i % n_seq_tiles)),
        ],
        out_specs=[
            pl.BlockSpec((A_Q_W, tm), lambda i: (0, i)),
            pl.BlockSpec((A_KV_HEADS, tm, HEAD_DIM), lambda i: (0, i, 0)),
            pl.BlockSpec((A_KV_HEADS, tm // A_TK, HEAD_DIM, A_TK), lambda i: (0, i, 0, 0)),
            pl.BlockSpec((B_BRANCHES, tm, 3 * B_W), lambda i: (0, i, 0)),
        ],
        out_shape=[
            jax.ShapeDtypeStruct((A_Q_W, t), BF16),
            jax.ShapeDtypeStruct((A_KV_HEADS, t, HEAD_DIM), BF16),
            jax.ShapeDtypeStruct((A_KV_HEADS, t // A_TK, HEAD_DIM, A_TK), BF16),
            jax.ShapeDtypeStruct((B_BRANCHES, t, 3 * B_W), BF16),
        ],
        compiler_params=_cparams(1),
        name="inproj_ab",
    )(x, g_mix, waT, wb, perm, gq, gk, cosT, sinT)


def _gqa_kernel(qT_ref, k_ref, vT_ref, o_ref, s_scr, p_scr, cm_scr, alpha_scr, m_scr, acc_scr):
    tq = qT_ref.shape[1]
    tk = A_TK
    n_chunks = vT_ref.shape[0]
    ones = jnp.ones((A_V_ROWS - HEAD_DIM, tk), BF16)

    def scores(chunk, slot):
        k = k_ref[pl.ds(pl.multiple_of(chunk * tk, tk), tk), :]
        for h in range(A_GROUP):
            s = _dot(k, qT_ref[h * HEAD_DIM:(h + 1) * HEAD_DIM, :])
            s_scr[slot, h] = s
            cm_scr[slot, h] = jnp.max(s.reshape(tk // 8, 8, tq), axis=0)

    def probs(slot):
        for h in range(A_GROUP):
            m_old = m_scr[h]
            m_new = jnp.maximum(m_old, jnp.max(cm_scr[slot, h], axis=0, keepdims=True))
            alpha_scr[slot, h] = jnp.exp2(m_old - m_new)
            m_scr[h] = m_new
            p_scr[slot, h] = jnp.exp2(s_scr[slot, h] - m_new).astype(BF16)

    def accumulate(chunk, slot):
        vT_aug = jnp.concatenate([vT_ref[chunk], ones], axis=0)
        for h in range(A_GROUP):
            acc_scr[h] = alpha_scr[slot, h] * acc_scr[h] + _dot(vT_aug, p_scr[slot, h])

    def steady(c, count):
        for u in range(count):
            accumulate(c + u, u % 2)
            probs((u + 1) % 2)
            scores(c + u + 2, u % 2)

    m_scr[...] = jnp.full(m_scr.shape, NEG, F32)
    acc_scr[...] = jnp.zeros(acc_scr.shape, F32)
    scores(0, 0)
    scores(1, 1)
    probs(0)
    n_loop = (n_chunks - 2) // A_UNROLL

    def body(i, carry):
        steady(i * A_UNROLL, A_UNROLL)
        return carry

    if n_loop > 1:
        lax.fori_loop(0, n_loop, body, 0)
    else:
        steady(0, n_loop * A_UNROLL)
    steady(n_loop * A_UNROLL, (n_chunks - 2) % A_UNROLL)
    accumulate(n_chunks - 2, 0)
    probs(1)
    accumulate(n_chunks - 1, 1)
    oT = jnp.concatenate(
        [acc_scr[h, :HEAD_DIM, :] / acc_scr[h, HEAD_DIM:HEAD_DIM + 1, :] for h in range(A_GROUP)], axis=0)
    o_ref[...] = oT.T.astype(BF16)


def _gqa_attention(qT, k, vT, batch, seq, tq=256):
    t = batch * seq
    nq = seq // tq
    k4 = k.reshape(A_KV_HEADS, batch, seq, HEAD_DIM)
    v5 = vT.reshape(A_KV_HEADS, batch, seq // A_TK, HEAD_DIM, A_TK)
    return pl.pallas_call(
        _gqa_kernel,
        grid=(batch, A_KV_HEADS, nq),
        in_specs=[
            pl.BlockSpec((A_GROUP * HEAD_DIM, tq), lambda b, g, i: (g, b * nq + i)),
            pl.BlockSpec((None, None, seq, HEAD_DIM), lambda b, g, i: (g, b, 0, 0)),
            pl.BlockSpec((None, None, seq // A_TK, HEAD_DIM, A_TK), lambda b, g, i: (g, b, 0, 0, 0)),
        ],
        out_specs=pl.BlockSpec((tq, A_GROUP * HEAD_DIM), lambda b, g, i: (b * nq + i, g)),
        out_shape=jax.ShapeDtypeStruct((t, A_Q_W), BF16),
        scratch_shapes=[pltpu.VMEM((2, A_GROUP, A_TK, tq), F32),
                        pltpu.VMEM((2, A_GROUP, A_TK, tq), BF16),
                        pltpu.VMEM((2, A_GROUP, 8, tq), F32),
                        pltpu.VMEM((2, A_GROUP, 1, tq), F32),
                        pltpu.VMEM((A_GROUP, 1, tq), F32),
                        pltpu.VMEM((A_GROUP, A_V_ROWS, tq), F32)],
        compiler_params=_cparams(3),
        name="gqa_attention",
    )(qT, k4, v5)


def _tile_perm(dil):
    c = TOKEN_TILE // dil
    dst = np.arange(TOKEN_TILE)
    src = (dst % c) * dil + dst // c
    p = np.zeros((TOKEN_TILE, TOKEN_TILE), np.float32)
    p[dst, src] = 1.0
    return p


def _seq_rows(ref, start, n, cols):
    c = ref.shape[1]
    if n <= c:
        return ref[start // c, start % c:start % c + n, cols]
    blocks = ref[start // c:(start + n) // c, :, cols]
    return blocks.reshape(n, blocks.shape[-1])


def _store_seq_rows(ref, start, cols, val):
    c = ref.shape[1]
    n = val.shape[0]
    if n <= c:
        ref[start // c, start % c:start % c + n, cols] = val
    else:
        ref[start // c:(start + n) // c, :, cols] = val.reshape(n // c, c, val.shape[-1])


def _dilated_kernel(main_ref, prev_ref, next_ref, bias_ref, r_ref, kv_scr, *, seq_len, cq):
    n = pl.program_id(2)
    kv_cols = slice(B_W, 3 * B_W)
    kv_scr[0:B_HALF, :] = prev_ref[..., kv_cols].reshape(B_HALF, 2 * B_W)
    for j in range(cq // B_SUB):
        kv_scr[B_HALF + j * B_SUB:B_HALF + (j + 1) * B_SUB, :] = _seq_rows(main_ref, j * B_SUB, B_SUB, kv_cols)
    kv_scr[B_HALF + cq:, :] = next_ref[..., kv_cols].reshape(B_HALF, 2 * B_W)

    pair_w = 2 * HEAD_DIM
    lane = lax.broadcasted_iota(jnp.int32, (1, pair_w), 1)
    lo = lane < HEAD_DIM
    key_col = lax.broadcasted_iota(jnp.int32, (1, B_KEYS), 1)

    for j in range(cq // B_SUB):
        key_pos = key_col + (n * cq + j * B_SUB - B_HALF)
        edge = jnp.where((key_pos >= 0) & (key_pos < seq_len), 0.0, NEG).astype(F32)
        krows = slice(j * B_SUB, j * B_SUB + B_KEYS)
        lses = []
        for pr in range(B_HEADS // 2):
            cols = slice(pr * pair_w, (pr + 1) * pair_w)
            q2 = _seq_rows(main_ref, j * B_SUB, B_SUB, cols)
            k2 = kv_scr[krows, cols]
            v2 = kv_scr[krows, B_W + pr * pair_w:B_W + (pr + 1) * pair_w]
            o_pair = []
            for hh in range(2):
                qm = jnp.where(lo if hh == 0 else ~lo, q2, jnp.zeros_like(q2))
                s = _dot_nt(qm, k2) + bias_ref[pr * 2 + hh] + edge
                m = jnp.max(s, axis=-1, keepdims=True)
                p = jnp.exp(s - m)
                l = jnp.sum(p, axis=-1, keepdims=True)
                o_pair.append(_dot(p.astype(BF16), v2) / l)
                lses.append(m + jnp.log(l))
            _store_seq_rows(r_ref, j * B_SUB, cols, jnp.where(lo, o_pair[0], o_pair[1]).astype(BF16))
        tile = lses[B_HEADS - 1]
        for head in range(B_HEADS - 2, -1, -1):
            tile = jnp.where(lane < (head + 1) * LSE_REP, lses[head], tile)
        hi = tile.astype(BF16)
        _store_seq_rows(r_ref, j * B_SUB, slice(B_W, B_W + pair_w), hi)
        _store_seq_rows(r_ref, j * B_SUB, slice(B_W + pair_w, B_R),
                        (tile - hi.astype(F32)).astype(BF16))


def _dilated_bias(branch):
    _, dil = B_PATTERNS[branch]
    slopes = np.exp2(-8.0 * np.arange(1, B_BRANCHES * B_HEADS + 1, dtype=np.float64)
                     / (B_BRANCHES * B_HEADS)).reshape(B_BRANCHES, B_HEADS)[branch]
    a = np.arange(B_SUB)[:, None]
    c = np.arange(B_KEYS)[None, :]
    rel = (c - B_HALF) - a
    bias = -slopes[:, None, None] * (np.abs(rel) * dil).astype(np.float64)[None]
    bias = np.where((np.abs(rel) <= B_HALF)[None], bias, NEG)
    return jnp.asarray(bias, F32)


def _dilated_branch(zb, branch, batch, seq):
    _, dil = B_PATTERNS[branch]
    t = batch * seq
    seq_len = seq // dil
    c = TOKEN_TILE // dil
    n_tiles = seq // TOKEN_TILE
    cq = min(512, seq_len)
    nchunk = seq_len // cq
    n_halo = seq_len // B_HALF
    per = cq // B_HALF
    zv = zb.reshape(B_BRANCHES, batch, n_tiles, dil, c, 3 * B_W)

    def halo_spec(pos):
        if c >= B_HALF:
            sub = c // B_HALF
            z6 = zb.reshape(B_BRANCHES, batch, n_tiles, dil, sub, B_HALF, 3 * B_W)
            return z6, pl.BlockSpec((None, None, None, None, None, B_HALF, 3 * B_W),
                                    lambda b, r, n: (branch, b, pos(n) // sub, r, pos(n) % sub, 0, 0))
        return zv, pl.BlockSpec((None, None, B_HALF // c, None, c, 3 * B_W),
                                lambda b, r, n: (branch, b, pos(n), r, 0, 0))

    prev_arr, prev_spec = halo_spec(lambda n: jnp.maximum(n * per - 1, 0))
    next_arr, next_spec = halo_spec(lambda n: jnp.minimum((n + 1) * per, n_halo - 1))
    res = pl.pallas_call(
        functools.partial(_dilated_kernel, seq_len=seq_len, cq=cq),
        grid=(batch, dil, nchunk),
        in_specs=[pl.BlockSpec((None, None, cq // c, None, c, 3 * B_W), lambda b, r, n: (branch, b, n, r, 0, 0)),
                  prev_spec, next_spec, _const_spec((B_HEADS, B_SUB, B_KEYS))],
        out_specs=pl.BlockSpec((None, cq // c, None, c, B_R), lambda b, r, n: (b, n, r, 0, 0)),
        out_shape=jax.ShapeDtypeStruct((batch, n_tiles, dil, c, B_R), BF16),
        scratch_shapes=[pltpu.VMEM((cq + 2 * B_HALF, 2 * B_W), BF16)],
        compiler_params=_cparams(3),
        name=f"dilated_branch{branch}",
    )(zv, prev_arr, next_arr, _dilated_bias(branch))
    return res.reshape(t, B_R)


def _merge_branches(rs, permT_ref):
    pair_w = 2 * HEAD_DIM
    lo = lax.broadcasted_iota(jnp.int32, (1, pair_w), 1) < HEAD_DIM
    tiles = [rs[0].astype(F32)] + [_dot(permT_ref[g - 1], rs[g]) for g in range(1, B_BRANCHES)]
    lses = [tl[:, B_W:B_W + pair_w] + tl[:, B_W + pair_w:] for tl in tiles]
    mx = functools.reduce(jnp.maximum, lses)
    es = [jnp.exp(l - mx) for l in lses]
    inv = 1.0 / functools.reduce(lambda u, v: u + v, es)
    pairs = []
    for pr in range(B_HEADS // 2):
        acc = None
        for tl, e in zip(tiles, es):
            w = e * inv
            w2 = jnp.where(lo, w[:, 2 * pr * LSE_REP:2 * pr * LSE_REP + 1],
                           w[:, (2 * pr + 1) * LSE_REP:(2 * pr + 1) * LSE_REP + 1])
            term = w2 * tl[:, pr * pair_w:(pr + 1) * pair_w]
            acc = term if acc is None else acc + term
        pairs.append(acc)
    return jnp.concatenate(pairs, axis=-1).astype(BF16)


def _inproj_kernel(x_ref, g_ref, w_ref, z_ref):
    hb = _rms(x_ref[...], g_ref[...]).astype(BF16)
    z_ref[...] = _dot(hb, w_ref[...]).astype(BF16)


def _inproj(x, g, w):
    t = x.shape[0]
    tm = TOKEN_TILE
    n_out = w.shape[1]
    return pl.pallas_call(
        _inproj_kernel,
        grid=(t // tm,),
        in_specs=[pl.BlockSpec((tm, D_MODEL), lambda i: (i, 0)),
                  _const_spec((1, D_MODEL)),
                  _const_spec((D_MODEL, n_out))],
        out_specs=pl.BlockSpec((tm, n_out), lambda i: (i, 0)),
        out_shape=jax.ShapeDtypeStruct((t, n_out), BF16),
        compiler_params=_cparams(1),
        name="inproj_c",
    )(x, g, w)


def _natten_kernel(q_ref, k0_ref, k1_ref, k2_ref, v0_ref, v1_ref, v2_ref, bias_ref, o_ref):
    k_refs = (k0_ref, k1_ref, k2_ref)
    v_refs = (v0_ref, v1_ref, v2_ref)
    lane = lax.broadcasted_iota(jnp.int32, (1, 2 * HEAD_DIM), 1)
    lo = lane < HEAD_DIM
    kb = k0_ref.shape[0]
    for pr in range(C_HEADS // 2):
        cols = slice(pr * 2 * HEAD_DIM, (pr + 1) * 2 * HEAD_DIM)
        q2 = q_ref[:, cols]
        ks = [r[:, cols] for r in k_refs]
        vs = [r[:, cols] for r in v_refs]
        o_pair = []
        for hh in range(2):
            qm = jnp.where(lo if hh == 0 else ~lo, q2, jnp.zeros_like(q2))
            s = jnp.concatenate([_dot_nt(qm, kk) for kk in ks], axis=-1) + bias_ref[pr * 2 + hh]
            m = jnp.max(s, axis=-1, keepdims=True)
            p = jnp.exp(s - m)
            l = jnp.sum(p, axis=-1, keepdims=True)
            pb = p.astype(BF16)
            o = _dot(pb[:, :kb], vs[0])
            for i in range(1, len(vs)):
                o = o + _dot(pb[:, i * kb:(i + 1) * kb], vs[i])
            o_pair.append(o / l)
        o_ref[:, cols] = jnp.where(lo, o_pair[0], o_pair[1]).astype(BF16)


def _natten_bias(rpb):
    c = np.arange(GRID_W)[:, None]
    kc = np.arange(GRID_W)[None, :]
    cstart = np.clip(c - C_WIN_COLS // 2, 0, GRID_W - C_WIN_COLS)
    col_valid = (kc >= cstart) & (kc < cstart + C_WIN_COLS)
    padded = jnp.pad(rpb.astype(F32), ((0, 0), (0, 0), (GRID_W, GRID_W)))
    base = GRID_W + C_WIN_COLS - 1
    blocks = jnp.stack([padded[:, :, base - q:base - q + GRID_W] for q in range(GRID_W)], axis=2)
    blocks = jnp.where(col_valid[None, None], blocks, NEG)
    masked = jnp.full((rpb.shape[0], GRID_W, GRID_W), NEG, F32)
    tables = []
    for off, first in ((0, lambda dr: 0), (-4, lambda dr: dr), (-8, lambda dr: 4)):
        rows = []
        for dr in range(C_ROWS_Q):
            rows.append(jnp.concatenate(
                [blocks[:, off + kri - dr + C_WIN_ROWS - 1]
                 if first(dr) <= kri < first(dr) + C_WIN_ROWS else masked
                 for kri in range(C_ROWS_K)], axis=-1))
        tables.append(jnp.concatenate(rows, axis=1))
    return jnp.stack(tables)


def _natten(z, bias, batch, seq):
    t = batch * seq
    qb = C_ROWS_Q * GRID_W
    nrb = seq // qb
    nkb = C_ROWS_K // C_ROWS_Q

    def kspec(i, col):
        return pl.BlockSpec(
            (qb, C_W), lambda rb, b: (b * nrb + jnp.clip(rb - 1, 0, nrb - nkb) + i, col))

    def variant(rb):
        return jnp.where(rb == 0, 0, jnp.where(rb == nrb - 1, 2, 1))

    return pl.pallas_call(
        _natten_kernel,
        grid=(nrb, batch),
        in_specs=[pl.BlockSpec((qb, C_W), lambda rb, b: (b * nrb + rb, 0))]
        + [kspec(i, 1) for i in range(nkb)]
        + [kspec(i, 2) for i in range(nkb)]
        + [pl.BlockSpec((None, C_HEADS, qb, C_ROWS_K * GRID_W), lambda rb, b: (variant(rb), 0, 0, 0))],
        out_specs=pl.BlockSpec((qb, C_W), lambda rb, b: (b * nrb + rb, 0)),
        out_shape=jax.ShapeDtypeStruct((t, C_W), BF16),
        compiler_params=_cparams(2),
        name="natten",
    )(z, z, z, z, z, z, z, bias)


def _mem_kv_kernel(mem_ref, g_ref, w_ref, kv_ref):
    mb = _rms(mem_ref[...], g_ref[...]).astype(BF16)
    kv_ref[...] = _dot(mb, w_ref[...]).astype(BF16)


def _mem_kv(mem, g, w):
    batch, n_mem, _ = mem.shape
    return pl.pallas_call(
        _mem_kv_kernel,
        grid=(batch,),
        in_specs=[pl.BlockSpec((None, n_mem, D_MODEL), lambda b: (b, 0, 0)),
                  _const_spec((1, D_MODEL)),
                  _const_spec((D_MODEL, 2 * D_MODEL))],
        out_specs=pl.BlockSpec((None, n_mem, 2 * D_MODEL), lambda b: (b, 0, 0)),
        out_shape=jax.ShapeDtypeStruct((batch, n_mem, 2 * D_MODEL), BF16),
        compiler_params=_cparams(1),
        name="mem_kv",
    )(mem, g, w)


def _xattn_tail(x, g_ref, wq_ref, kv_ref, wo_ref, y_ref):
    q = _dot(_rms(x, g_ref[...]).astype(BF16), wq_ref[...]).astype(BF16)
    outs = []
    for h in range(X_HEADS):
        cols = slice(h * X_HEAD_DIM, (h + 1) * X_HEAD_DIM)
        s = _dot_nt(q[:, cols], kv_ref[:, cols])
        m = jnp.max(s, axis=-1, keepdims=True)
        p = jnp.exp(s - m)
        l = jnp.sum(p, axis=-1, keepdims=True)
        o = _dot(p.astype(BF16), kv_ref[:, D_MODEL + h * X_HEAD_DIM:D_MODEL + (h + 1) * X_HEAD_DIM])
        outs.append((o / l).astype(BF16))
    y_ref[...] = x + _dot(jnp.concatenate(outs, axis=-1), wo_ref[...])


def _mix_ab_xattn_kernel(x_ref, oa_ref, r0_ref, r1_ref, r2_ref, permT_ref, woa_ref, wob_ref,
                         g_ref, wq_ref, kv_ref, wo_ref, y_ref):
    ob = _merge_branches([r0_ref[...], r1_ref[...], r2_ref[...]], permT_ref)
    x = x_ref[...] + _dot(oa_ref[...], woa_ref[...]) + _dot(ob, wob_ref[...])
    _xattn_tail(x, g_ref, wq_ref, kv_ref, wo_ref, y_ref)


def _mix_c_xattn_kernel(x_ref, oc_ref, woc_ref, g_ref, wq_ref, kv_ref, wo_ref, y_ref):
    x = x_ref[...] + _dot(oc_ref[...], woc_ref[...])
    _xattn_tail(x, g_ref, wq_ref, kv_ref, wo_ref, y_ref)


def _mix_xattn(body, x, token_args, const_args, g, wq, kv, wo, seq):
    t = x.shape[0]
    tm = TOKEN_TILE
    n_seq_tiles = seq // tm
    n_mem = kv.shape[1]
    token_spec = lambda a: pl.BlockSpec((tm, a.shape[1]), lambda i: (i, 0))
    in_specs = ([token_spec(x)] + [token_spec(a) for a in token_args]
                + [_const_spec(a.shape) for a in const_args]
                + [_const_spec((1, D_MODEL)),
                   _const_spec((D_MODEL, D_MODEL)),
                   pl.BlockSpec((None, n_mem, 2 * D_MODEL), lambda i: (i // n_seq_tiles, 0, 0)),
                   _const_spec((D_MODEL, D_MODEL))])
    return pl.pallas_call(
        body,
        grid=(t // tm,),
        in_specs=in_specs,
        out_specs=pl.BlockSpec((tm, D_MODEL), lambda i: (i, 0)),
        out_shape=jax.ShapeDtypeStruct((t, D_MODEL), F32),
        compiler_params=_cparams(1),
        name="mix_xattn",
    )(x, *token_args, *const_args, g, wq, kv, wo)


def _swiglu_kernel(x_ref, g_ref, wg_ref, wu_ref, wd_ref, gf_ref, y_ref, *, final_norm):
    x = x_ref[...]
    hb = _rms(x, g_ref[...]).astype(BF16)
    gate = _dot(hb, wg_ref[...])
    up = _dot(hb, wu_ref[...])
    act = (gate / (1.0 + jnp.exp(-gate)) * up).astype(BF16)
    y = x + _dot(act, wd_ref[...])
    if final_norm:
        y = _rms(y, gf_ref[...])
    y_ref[...] = y


def _swiglu(x, g, wg, wu, wd, g_final, final_norm, tm=256):
    t = x.shape[0]
    return pl.pallas_call(
        functools.partial(_swiglu_kernel, final_norm=final_norm),
        grid=(t // tm,),
        in_specs=[pl.BlockSpec((tm, D_MODEL), lambda i: (i, 0)),
                  _const_spec((1, D_MODEL)),
                  _const_spec((D_MODEL, D_FF)),
                  _const_spec((D_MODEL, D_FF)),
                  _const_spec((D_FF, D_MODEL)),
                  _const_spec((1, D_MODEL))],
        out_specs=pl.BlockSpec((tm, D_MODEL), lambda i: (i, 0)),
        out_shape=jax.ShapeDtypeStruct((t, D_MODEL), F32),
        compiler_params=_cparams(1),
        name="swiglu",
    )(x, g, wg, wu, wd, g_final)


def _rope_tables(seq):
    tok = jnp.arange(seq, dtype=jnp.int32)
    row = (tok // GRID_W).astype(F32)
    col = (tok % GRID_W).astype(F32)
    axis_dim = HEAD_DIM // 2
    inv_freq = ROPE_THETA ** (-jnp.arange(0, axis_dim, 2, dtype=F32) / axis_dim)
    ang = jnp.concatenate([row[:, None] * inv_freq, col[:, None] * inv_freq], axis=-1)
    return jnp.cos(ang).T, jnp.sin(ang).T


def _prepare_ab(w_in, g_qn, g_kn, w_out):
    perm = np.concatenate([np.arange(0, HEAD_DIM, 2), np.arange(1, HEAD_DIM, 2)])
    cols = np.concatenate(
        [h * HEAD_DIM + perm for h in range(A_Q_HEADS + A_KV_HEADS)]
        + [np.arange(A_Q_W + A_KV_W, A_W)])
    waT = w_in[:, cols].T.astype(BF16)
    zbw = w_in[:, A_W:].reshape(D_MODEL, 3, B_BRANCHES, B_W)
    qscale = HEAD_DIM ** -0.5
    wb = jnp.stack([jnp.concatenate([zbw[:, 0, g] * qscale, zbw[:, 1, g], zbw[:, 2, g]], axis=1)
                    for g in range(B_BRANCHES)]).astype(BF16)
    gq = (g_qn[perm] * (qscale * LOG2E)).reshape(HEAD_DIM, 1).astype(F32)
    gk = g_kn[perm].reshape(HEAD_DIM, 1).astype(F32)
    perms = np.stack([_tile_perm(dil) for _, dil in B_PATTERNS[1:]])
    tile_perm = jnp.asarray(perms, BF16)
    tile_perm_t = jnp.asarray(perms.transpose(0, 2, 1), BF16)
    return waT, wb, tile_perm, tile_perm_t, gq, gk, w_out[:A_Q_W].astype(BF16), w_out[A_Q_W:].astype(BF16)


def _prepare_layers(p):
    depth = p["g_mix"].shape[0]
    row = lambda g: g.reshape(1, D_MODEL)
    layers = []
    for l in range(depth):
        lay = dict(g_mix=row(p["g_mix"][l]), g_xattn=row(p["g_xattn"][l]), g_mem=row(p["g_mem"][l]),
                   g_ffn=row(p["g_ffn"][l]),
                   wq=(p["wq_x"][l] * X_HEAD_DIM ** -0.5).astype(BF16), wkv=p["wkv_x"][l].astype(BF16),
                   wo=p["wo_x"][l].astype(BF16), wg=p["w_gu"][l][:, :D_FF].astype(BF16),
                   wu=p["w_gu"][l][:, D_FF:].astype(BF16), wd=p["w_down"][l].astype(BF16))
        i = l // 2
        if l % 2 == 0:
            lay["ab"] = _prepare_ab(p["w_in_ab"][i], p["g_qn"][i], p["g_kn"][i], p["w_out_ab"][i])
        else:
            w_in = p["w_in_c"][i]
            lay["w_in_c"] = jnp.concatenate(
                [w_in[:, :C_W] * HEAD_DIM ** -0.5, w_in[:, C_W:]], axis=1).astype(BF16)
            lay["bias_c"] = _natten_bias(p["rpb_c"][i])
            lay["w_out_c"] = p["w_out_c"][i].astype(BF16)
        layers.append(lay)
    return layers


def _trunk(x, mem, layers, g_final):
    batch, seq, _ = x.shape
    x = x.reshape(batch * seq, D_MODEL)
    for l, lay in enumerate(layers):
        if l % 2 == 0:
            waT, wb, perm, permT, gq, gk, wo_a, wo_b = lay["ab"]
            cosT, sinT = _rope_tables(seq)
            qT, k, vT, zb = _inproj_ab(x, lay["g_mix"], waT, wb, perm, gq, gk, cosT, sinT, seq)
            oa = _gqa_attention(qT, k, vT, batch, seq)
            rs = [_dilated_branch(zb, g, batch, seq) for g in range(B_BRANCHES)]
            body, token_args, const_args = _mix_ab_xattn_kernel, [oa] + rs, [permT, wo_a, wo_b]
        else:
            z = _inproj(x, lay["g_mix"], lay["w_in_c"])
            oc = _natten(z, lay["bias_c"], batch, seq)
            body, token_args, const_args = _mix_c_xattn_kernel, [oc], [lay["w_out_c"]]
        kv = _mem_kv(mem, lay["g_mem"], lay["wkv"])
        x = _mix_xattn(body, x, token_args, const_args, lay["g_xattn"], lay["wq"], kv, lay["wo"], seq)
        x = _swiglu(x, lay["g_ffn"], lay["wg"], lay["wu"], lay["wd"], g_final,
                    final_norm=(l == len(layers) - 1))
    return x.reshape(batch, seq, D_MODEL)


def kernel(x_prompt, x_sample, mem_prompt, mem_sample, g_mix, w_in_ab, g_qn, g_kn, w_out_ab, w_in_c,
           rpb_c, w_out_c, g_xattn, g_mem, wq_x, wkv_x, wo_x, g_ffn, w_gu, w_down, g_final):
    layers = _prepare_layers(dict(
        g_mix=g_mix, w_in_ab=w_in_ab, g_qn=g_qn, g_kn=g_kn, w_out_ab=w_out_ab, w_in_c=w_in_c,
        rpb_c=rpb_c, w_out_c=w_out_c, g_xattn=g_xattn, g_mem=g_mem, wq_x=wq_x, wkv_x=wkv_x,
        wo_x=wo_x, g_ffn=g_ffn, w_gu=w_gu, w_down=w_down))
    g_final = g_final.reshape(1, D_MODEL)
    return (_trunk(x_prompt, mem_prompt, layers, g_final), _trunk(x_sample, mem_sample, layers, g_final))
```

```python
import functools
import math

import numpy as np
import jax
import jax.numpy as jnp
from jax import lax
from jax.experimental import pallas as pl
from jax.experimental.pallas import tpu as pltpu

F32 = jnp.float32
BF16 = jnp.bfloat16

D_MODEL = 1024
GRID_W = 64
HEAD_DIM = 64
EPS = 1e-6
NEG = -1e30
LOG2E = math.log2(math.e)
TOKEN_TILE = 512

A_Q_HEADS = 8
A_KV_HEADS = 2
A_GROUP = A_Q_HEADS // A_KV_HEADS
ROPE_THETA = 10000.0
A_Q_W = A_Q_HEADS * HEAD_DIM
A_KV_W = A_KV_HEADS * HEAD_DIM
A_W = A_Q_W + 2 * A_KV_W
A_TK = 256
A_V_ROWS = HEAD_DIM + 16
A_UNROLL = 6
A_UNROLL_BOUNDED = 4
A_SHIFT_MARGIN = 60.0
A_BOUND_LIMIT = 90.0
A_BOUND_SLACK = 1.0 + 2.0 ** -6

B_PATTERNS = ((128, 1), (512, 4), (2048, 16))
B_BRANCHES = len(B_PATTERNS)
B_HEADS = 4
B_HALF = 64
B_W = B_HEADS * HEAD_DIM
B_SUB = 2 * B_HALF
B_KEYS = 4 * B_HALF
LSE_REP = 2 * HEAD_DIM // B_HEADS
B_R = B_W + 4 * HEAD_DIM

C_HEADS = 16
C_WIN_ROWS = 8
C_WIN_COLS = 16
C_ROWS_Q = 4
C_ROWS_K = 12
C_W = C_HEADS * HEAD_DIM

X_HEADS = 4
X_HEAD_DIM = D_MODEL // X_HEADS
D_FF = 2816

VMEM_LIMIT = 56 * 1024 * 1024


def _cparams(n_axes):
    return pltpu.CompilerParams(dimension_semantics=("arbitrary",) * n_axes,
                                vmem_limit_bytes=VMEM_LIMIT)


def _rms(x, g):
    ms = jnp.mean(x * x, axis=-1, keepdims=True)
    return x * lax.rsqrt(ms + EPS) * g


def _dot(a, b):
    return jnp.dot(a, b, preferred_element_type=F32)


def _dot_nt(a, b):
    return lax.dot_general(a, b, (((1,), (1,)), ((), ())), preferred_element_type=F32)


def _const_spec(shape):
    zeros = (0,) * len(shape)
    return pl.BlockSpec(shape, lambda *_: zeros)


def _inproj_ab_kernel(x_ref, g_ref, waT_ref, wb_ref, perm_ref, gq_ref, gk_ref, cos_ref, sin_ref,
                      qT_ref, k_ref, vT_ref, stats_ref, zb_ref):
    hb = _rms(x_ref[...], g_ref[...]).astype(BF16)
    zT = _dot_nt(waT_ref[...], hb)
    cos = cos_ref[...]
    sin = sin_ref[...]
    half = HEAD_DIM // 2

    def norm_rope(zh, gcol):
        ms = jnp.mean(zh * zh, axis=0, keepdims=True)
        y = zh * lax.rsqrt(ms + EPS) * gcol
        yr, yi = y[:half], y[half:]
        return jnp.concatenate([yr * cos - yi * sin, yr * sin + yi * cos], axis=0)

    def l2(v):
        return jnp.sqrt(jnp.sum(v * v, axis=0, keepdims=True))

    stats_ref[...] = jnp.zeros(stats_ref.shape, F32)
    for h in range(A_Q_HEADS):
        sl = slice(h * HEAD_DIM, (h + 1) * HEAD_DIM)
        q = norm_rope(zT[sl], gq_ref[...])
        qT_ref[sl, :] = q.astype(BF16)
        stats_ref[h // A_GROUP, h % A_GROUP:h % A_GROUP + 1, :] = l2(q)
    ks = [norm_rope(zT[A_Q_W + h * HEAD_DIM:A_Q_W + (h + 1) * HEAD_DIM], gk_ref[...])
          for h in range(A_KV_HEADS)]
    for h in range(A_KV_HEADS):
        stats_ref[h, A_GROUP:A_GROUP + 1, :] = l2(ks[h])
    kT = jnp.concatenate(ks, axis=0)
    k_nat = kT.T
    for h in range(A_KV_HEADS):
        k_ref[h] = k_nat[:, h * HEAD_DIM:(h + 1) * HEAD_DIM].astype(BF16)
    for h in range(A_KV_HEADS):
        for c in range(vT_ref.shape[1]):
            vT_ref[h, c] = zT[A_Q_W + A_KV_W + h * HEAD_DIM:A_Q_W + A_KV_W + (h + 1) * HEAD_DIM,
                              c * A_TK:(c + 1) * A_TK].astype(BF16)
    zb_ref[0] = _dot(hb, wb_ref[0]).astype(BF16)
    for g in range(1, B_BRANCHES):
        zb_ref[g] = _dot(perm_ref[g - 1], _dot(hb, wb_ref[g]).astype(BF16)).astype(BF16)


def _inproj_ab(x, g_mix, waT, wb, perm, gq, gk, cosT, sinT, seq):
    t = x.shape[0]
    tm = TOKEN_TILE
    n_seq_tiles = seq // tm
    return pl.pallas_call(
        _inproj_ab_kernel,
        grid=(t // tm,),
        in_specs=[
            pl.BlockSpec((tm, D_MODEL), lambda i: (i, 0)),
            _const_spec((1, D_MODEL)),
            _const_spec((A_W, D_MODEL)),
            _const_spec((B_BRANCHES, D_MODEL, 3 * B_W)),
            _const_spec((B_BRANCHES - 1, tm, tm)),
            _const_spec((HEAD_DIM, 1)),
            _const_spec((HEAD_DIM, 1)),
            pl.BlockSpec((HEAD_DIM // 2, tm), lambda i: (0, i % n_seq_tiles)),
            pl.BlockSpec((HEAD_DIM // 2, tm), lambda i: (0, i % n_seq_tiles)),
        ],
        out_specs=[
            pl.BlockSpec((A_Q_W, tm), lambda i: (0, i)),
            pl.BlockSpec((A_KV_HEADS, tm, HEAD_DIM), lambda i: (0, i, 0)),
            pl.BlockSpec((A_KV_HEADS, tm // A_TK, HEAD_DIM, A_TK), lambda i: (0, i, 0, 0)),
            pl.BlockSpec((A_KV_HEADS, 8, tm), lambda i: (0, 0, i)),
            pl.BlockSpec((B_BRANCHES, tm, 3 * B_W), lambda i: (0, i, 0)),
        ],
        out_shape=[
            jax.ShapeDtypeStruct((A_Q_W, t), BF16),
            jax.ShapeDtypeStruct((A_KV_HEADS, t, HEAD_DIM), BF16),
            jax.ShapeDtypeStruct((A_KV_HEADS, t // A_TK, HEAD_DIM, A_TK), BF16),
            jax.ShapeDtypeStruct((A_KV_HEADS, 8, t), F32),
            jax.ShapeDtypeStruct((B_BRANCHES, t, 3 * B_W), BF16),
        ],
        compiler_params=_cparams(1),
        name="inproj_ab",
    )(x, g_mix, waT, wb, perm, gq, gk, cosT, sinT)


def _gqa_kernel(qT_ref, k_ref, vT_ref, o_ref, s_scr, p_scr, cm_scr, alpha_scr, m_scr, acc_scr):
    tq = qT_ref.shape[1]
    tk = A_TK
    n_chunks = vT_ref.shape[0]
    ones = jnp.ones((A_V_ROWS - HEAD_DIM, tk), BF16)

    def scores(chunk, slot):
        k = k_ref[pl.ds(pl.multiple_of(chunk * tk, tk), tk), :]
        for h in range(A_GROUP):
            s = _dot(k, qT_ref[h * HEAD_DIM:(h + 1) * HEAD_DIM, :])
            s_scr[slot, h] = s
            cm_scr[slot, h] = jnp.max(s.reshape(tk // 8, 8, tq), axis=0)

    def probs(slot):
        for h in range(A_GROUP):
            m_old = m_scr[h]
            m_new = jnp.maximum(m_old, jnp.max(cm_scr[slot, h], axis=0, keepdims=True))
            alpha_scr[slot, h] = jnp.exp2(m_old - m_new)
            m_scr[h] = m_new
            p_scr[slot, h] = jnp.exp2(s_scr[slot, h] - m_new).astype(BF16)

    def accumulate(chunk, slot):
        vT_aug = jnp.concatenate([vT_ref[chunk], ones], axis=0)
        for h in range(A_GROUP):
            acc_scr[h] = alpha_scr[slot, h] * acc_scr[h] + _dot(vT_aug, p_scr[slot, h])

    def steady(c, count):
        for u in range(count):
            accumulate(c + u, u % 2)
            probs((u + 1) % 2)
            scores(c + u + 2, u % 2)

    m_scr[...] = jnp.full(m_scr.shape, NEG, F32)
    acc_scr[...] = jnp.zeros(acc_scr.shape, F32)
    scores(0, 0)
    scores(1, 1)
    probs(0)
    n_loop = (n_chunks - 2) // A_UNROLL

    def body(i, carry):
        steady(i * A_UNROLL, A_UNROLL)
        return carry

    if n_loop > 1:
        lax.fori_loop(0, n_loop, body, 0)
    else:
        steady(0, n_loop * A_UNROLL)
    steady(n_loop * A_UNROLL, (n_chunks - 2) % A_UNROLL)
    accumulate(n_chunks - 2, 0)
    probs(1)
    accumulate(n_chunks - 1, 1)
    oT = jnp.concatenate(
        [acc_scr[h, :HEAD_DIM, :] / acc_scr[h, HEAD_DIM:HEAD_DIM + 1, :] for h in range(A_GROUP)], axis=0)
    o_ref[...] = oT.T.astype(BF16)


def _gqa_bounded_kernel(qT_ref, k_ref, vT_ref, shift_ref, o_ref, p_scr, acc_scr):
    tq = qT_ref.shape[1]
    tk = A_TK
    n_chunks = vT_ref.shape[0]
    ones = jnp.ones((A_V_ROWS - HEAD_DIM, tk), BF16)

    def probs(chunk, slot):
        k = k_ref[pl.ds(pl.multiple_of(chunk * tk, tk), tk), :]
        for h in range(A_GROUP):
            s = _dot(k, qT_ref[h * HEAD_DIM:(h + 1) * HEAD_DIM, :])
            p_scr[slot, h] = jnp.exp2(s - shift_ref[h:h + 1, :]).astype(BF16)

    def accumulate(chunk, slot):
        vT_aug = jnp.concatenate([vT_ref[chunk], ones], axis=0)
        for h in range(A_GROUP):
            acc_scr[h] += _dot(vT_aug, p_scr[slot, h])

    def steady(c, count):
        for u in range(count):
            accumulate(c + u, u % 2)
            probs(c + u + 2, u % 2)

    acc_scr[...] = jnp.zeros(acc_scr.shape, F32)
    probs(0, 0)
    probs(1, 1)
    n_loop = (n_chunks - 2) // A_UNROLL_BOUNDED

    def body(i, carry):
        steady(i * A_UNROLL_BOUNDED, A_UNROLL_BOUNDED)
        return carry

    if n_loop > 1:
        lax.fori_loop(0, n_loop, body, 0)
    else:
        steady(0, n_loop * A_UNROLL_BOUNDED)
    steady(n_loop * A_UNROLL_BOUNDED, (n_chunks - 2) % A_UNROLL_BOUNDED)
    accumulate(n_chunks - 2, 0)
    accumulate(n_chunks - 1, 1)
    oT = jnp.concatenate(
        [acc_scr[h, :HEAD_DIM, :] / acc_scr[h, HEAD_DIM:HEAD_DIM + 1, :] for h in range(A_GROUP)], axis=0)
    o_ref[...] = oT.T.astype(BF16)


def _gqa_attention(qT, k, vT, stats, batch, seq, tq=256):
    t = batch * seq
    nq = seq // tq
    k4 = k.reshape(A_KV_HEADS, batch, seq, HEAD_DIM)
    v5 = vT.reshape(A_KV_HEADS, batch, seq // A_TK, HEAD_DIM, A_TK)
    kmax = jnp.max(stats[:, A_GROUP].reshape(A_KV_HEADS, batch, seq), axis=-1)
    bound = (stats.reshape(A_KV_HEADS, 8, batch, seq) * kmax[:, None, :, None]
             * A_BOUND_SLACK).reshape(A_KV_HEADS, 8, t)
    in_specs = [
        pl.BlockSpec((A_GROUP * HEAD_DIM, tq), lambda b, g, i: (g, b * nq + i)),
        pl.BlockSpec((None, None, seq, HEAD_DIM), lambda b, g, i: (g, b, 0, 0)),
        pl.BlockSpec((None, None, seq // A_TK, HEAD_DIM, A_TK), lambda b, g, i: (g, b, 0, 0, 0)),
    ]
    common = dict(
        grid=(batch, A_KV_HEADS, nq),
        out_specs=pl.BlockSpec((tq, A_GROUP * HEAD_DIM), lambda b, g, i: (b * nq + i, g)),
        out_shape=jax.ShapeDtypeStruct((t, A_Q_W), BF16),
        compiler_params=_cparams(3),
    )

    def bounded():
        return pl.pallas_call(
            _gqa_bounded_kernel,
            in_specs=in_specs + [pl.BlockSpec((None, 8, tq), lambda b, g, i: (g, 0, b * nq + i))],
            scratch_shapes=[pltpu.VMEM((2, A_GROUP, A_TK, tq), BF16),
                            pltpu.VMEM((A_GROUP, A_V_ROWS, tq), F32)],
            name="gqa_bounded", **common,
        )(qT, k4, v5, bound - A_SHIFT_MARGIN)

    def online():
        return pl.pallas_call(
            _gqa_kernel,
            in_specs=in_specs,
            scratch_shapes=[pltpu.VMEM((2, A_GROUP, A_TK, tq), F32),
                            pltpu.VMEM((2, A_GROUP, A_TK, tq), BF16),
                            pltpu.VMEM((2, A_GROUP, 8, tq), F32),
                            pltpu.VMEM((2, A_GROUP, 1, tq), F32),
                            pltpu.VMEM((A_GROUP, 1, tq), F32),
                            pltpu.VMEM((A_GROUP, A_V_ROWS, tq), F32)],
            name="gqa_attention", **common,
        )(qT, k4, v5)

    safe = jnp.max(bound[:, :A_GROUP]) <= A_BOUND_LIMIT
    return lax.cond(safe, bounded, online)


def _tile_perm(dil):
    c = TOKEN_TILE // dil
    dst = np.arange(TOKEN_TILE)
    src = (dst % c) * dil + dst // c
    p = np.zeros((TOKEN_TILE, TOKEN_TILE), np.float32)
    p[dst, src] = 1.0
    return p


def _seq_rows(ref, start, n, cols):
    c = ref.shape[1]
    if n <= c:
        return ref[start // c, start % c:start % c + n, cols]
    blocks = ref[start // c:(start + n) // c, :, cols]
    return blocks.reshape(n, blocks.shape[-1])


def _store_seq_rows(ref, start, cols, val):
    c = ref.shape[1]
    n = val.shape[0]
    if n <= c:
        ref[start // c, start % c:start % c + n, cols] = val
    else:
        ref[start // c:(start + n) // c, :, cols] = val.reshape(n // c, c, val.shape[-1])


def _dilated_kernel(main_ref, prev_ref, next_ref, bias_ref, r_ref, kv_scr, *, seq_len, cq):
    n = pl.program_id(2)
    kv_cols = slice(B_W, 3 * B_W)
    pair_w = 2 * HEAD_DIM
    lane = lax.broadcasted_iota(jnp.int32, (1, pair_w), 1)
    lo = lane < HEAD_DIM
    key_col = lax.broadcasted_iota(jnp.int32, (1, B_KEYS), 1)

    for rr in range(main_ref.shape[1]):
        main, res = main_ref.at[:, rr], r_ref.at[:, rr]
        kv_scr[0:B_HALF, :] = prev_ref[..., rr, :, kv_cols].reshape(B_HALF, 2 * B_W)
        for j in range(cq // B_SUB):
            kv_scr[B_HALF + j * B_SUB:B_HALF + (j + 1) * B_SUB, :] = _seq_rows(main, j * B_SUB, B_SUB, kv_cols)
        kv_scr[B_HALF + cq:, :] = next_ref[..., rr, :, kv_cols].reshape(B_HALF, 2 * B_W)

        for j in range(cq // B_SUB):
            key_pos = key_col + (n * cq + j * B_SUB - B_HALF)
            edge = jnp.where((key_pos >= 0) & (key_pos < seq_len), 0.0, NEG).astype(F32)
            krows = slice(j * B_SUB, j * B_SUB + B_KEYS)
            lses = []
            for pr in range(B_HEADS // 2):
                cols = slice(pr * pair_w, (pr + 1) * pair_w)
                q2 = _seq_rows(main, j * B_SUB, B_SUB, cols)
                k2 = kv_scr[krows, cols]
                v2 = kv_scr[krows, B_W + pr * pair_w:B_W + (pr + 1) * pair_w]
                o_pair = []
                for hh in range(2):
                    qm = jnp.where(lo if hh == 0 else ~lo, q2, jnp.zeros_like(q2))
                    s = _dot_nt(qm, k2) + bias_ref[pr * 2 + hh] + edge
                    m = jnp.max(s, axis=-1, keepdims=True)
                    p = jnp.exp(s - m)
                    l = jnp.sum(p, axis=-1, keepdims=True)
                    o_pair.append(_dot(p.astype(BF16), v2) / l)
                    lses.append(m + jnp.log(l))
                _store_seq_rows(res, j * B_SUB, cols, jnp.where(lo, o_pair[0], o_pair[1]).astype(BF16))
            tile = lses[B_HEADS - 1]
            for head in range(B_HEADS - 2, -1, -1):
                tile = jnp.where(lane < (head + 1) * LSE_REP, lses[head], tile)
            hi = tile.astype(BF16)
            _store_seq_rows(res, j * B_SUB, slice(B_W, B_W + pair_w), hi)
            _store_seq_rows(res, j * B_SUB, slice(B_W + pair_w, B_R),
                            (tile - hi.astype(F32)).astype(BF16))


def _dilated_bias(branch):
    _, dil = B_PATTERNS[branch]
    slopes = np.exp2(-8.0 * np.arange(1, B_BRANCHES * B_HEADS + 1, dtype=np.float64)
                     / (B_BRANCHES * B_HEADS)).reshape(B_BRANCHES, B_HEADS)[branch]
    a = np.arange(B_SUB)[:, None]
    c = np.arange(B_KEYS)[None, :]
    rel = (c - B_HALF) - a
    bias = -slopes[:, None, None] * (np.abs(rel) * dil).astype(np.float64)[None]
    bias = np.where((np.abs(rel) <= B_HALF)[None], bias, NEG)
    return jnp.asarray(bias, F32)


def _dilated_branch(zb, branch, batch, seq):
    _, dil = B_PATTERNS[branch]
    t = batch * seq
    seq_len = seq // dil
    c = TOKEN_TILE // dil
    n_tiles = seq // TOKEN_TILE
    cq = min(512, seq_len)
    nchunk = seq_len // cq
    n_halo = seq_len // B_HALF
    per = cq // B_HALF
    rb = min(dil, max(1, 512 // cq))
    zv = zb.reshape(B_BRANCHES, batch, n_tiles, dil, c, 3 * B_W)

    def halo_spec(pos):
        if c >= B_HALF:
            sub = c // B_HALF
            z6 = zb.reshape(B_BRANCHES, batch, n_tiles, dil, sub, B_HALF, 3 * B_W)
            return z6, pl.BlockSpec((None, None, None, rb, None, B_HALF, 3 * B_W),
                                    lambda b, r, n: (branch, b, pos(n) // sub, r, pos(n) % sub, 0, 0))
        return zv, pl.BlockSpec((None, None, B_HALF // c, rb, c, 3 * B_W),
                                lambda b, r, n: (branch, b, pos(n), r, 0, 0))

    prev_arr, prev_spec = halo_spec(lambda n: jnp.maximum(n * per - 1, 0))
    next_arr, next_spec = halo_spec(lambda n: jnp.minimum((n + 1) * per, n_halo - 1))
    res = pl.pallas_call(
        functools.partial(_dilated_kernel, seq_len=seq_len, cq=cq),
        grid=(batch, dil // rb, nchunk),
        in_specs=[pl.BlockSpec((None, None, cq // c, rb, c, 3 * B_W), lambda b, r, n: (branch, b, n, r, 0, 0)),
                  prev_spec, next_spec, _const_spec((B_HEADS, B_SUB, B_KEYS))],
        out_specs=pl.BlockSpec((None, cq // c, rb, c, B_R), lambda b, r, n: (b, n, r, 0, 0)),
        out_shape=jax.ShapeDtypeStruct((batch, n_tiles, dil, c, B_R), BF16),
        scratch_shapes=[pltpu.VMEM((cq + 2 * B_HALF, 2 * B_W), BF16)],
        compiler_params=_cparams(3),
        name=f"dilated_branch{branch}",
    )(zv, prev_arr, next_arr, _dilated_bias(branch))
    return res.reshape(t, B_R)


def _merge_branches(rs, permT_ref):
    pair_w = 2 * HEAD_DIM
    lo = lax.broadcasted_iota(jnp.int32, (1, pair_w), 1) < HEAD_DIM
    tiles = [rs[0].astype(F32)] + [_dot(permT_ref[g - 1], rs[g]) for g in range(1, B_BRANCHES)]
    lses = [tl[:, B_W:B_W + pair_w] + tl[:, B_W + pair_w:] for tl in tiles]
    mx = functools.reduce(jnp.maximum, lses)
    es = [jnp.exp(l - mx) for l in lses]
    inv = 1.0 / functools.reduce(lambda u, v: u + v, es)
    pairs = []
    for pr in range(B_HEADS // 2):
        acc = None
        for tl, e in zip(tiles, es):
            w = e * inv
            w2 = jnp.where(lo, w[:, 2 * pr * LSE_REP:2 * pr * LSE_REP + 1],
                           w[:, (2 * pr + 1) * LSE_REP:(2 * pr + 1) * LSE_REP + 1])
            term = w2 * tl[:, pr * pair_w:(pr + 1) * pair_w]
            acc = term if acc is None else acc + term
        pairs.append(acc)
    return jnp.concatenate(pairs, axis=-1).astype(BF16)


def _inproj_kernel(x_ref, g_ref, w_ref, z_ref):
    hb = _rms(x_ref[...], g_ref[...]).astype(BF16)
    z_ref[...] = _dot(hb, w_ref[...]).astype(BF16)


def _inproj(x, g, w):
    t = x.shape[0]
    tm = TOKEN_TILE
    n_out = w.shape[1]
    return pl.pallas_call(
        _inproj_kernel,
        grid=(t // tm,),
        in_specs=[pl.BlockSpec((tm, D_MODEL), lambda i: (i, 0)),
                  _const_spec((1, D_MODEL)),
                  _const_spec((D_MODEL, n_out))],
        out_specs=pl.BlockSpec((tm, n_out), lambda i: (i, 0)),
        out_shape=jax.ShapeDtypeStruct((t, n_out), BF16),
        compiler_params=_cparams(1),
        name="inproj_c",
    )(x, g, w)


def _natten_kernel(q_ref, k0_ref, k1_ref, k2_ref, v0_ref, v1_ref, v2_ref, bias_ref, o_ref):
    k_refs = (k0_ref, k1_ref, k2_ref)
    v_refs = (v0_ref, v1_ref, v2_ref)
    lane = lax.broadcasted_iota(jnp.int32, (1, 2 * HEAD_DIM), 1)
    lo = lane < HEAD_DIM
    kb = k0_ref.shape[0]
    for pr in range(C_HEADS // 2):
        cols = slice(pr * 2 * HEAD_DIM, (pr + 1) * 2 * HEAD_DIM)
        q2 = q_ref[:, cols]
        ks = [r[:, cols] for r in k_refs]
        vs = [r[:, cols] for r in v_refs]
        o_pair = []
        for hh in range(2):
            qm = jnp.where(lo if hh == 0 else ~lo, q2, jnp.zeros_like(q2))
            s = jnp.concatenate([_dot_nt(qm, kk) for kk in ks], axis=-1) + bias_ref[pr * 2 + hh]
            m = jnp.max(s, axis=-1, keepdims=True)
            p = jnp.exp(s - m)
            l = jnp.sum(p, axis=-1, keepdims=True)
            pb = p.astype(BF16)
            o = _dot(pb[:, :kb], vs[0])
            for i in range(1, len(vs)):
                o = o + _dot(pb[:, i * kb:(i + 1) * kb], vs[i])
            o_pair.append(o / l)
        o_ref[:, cols] = jnp.where(lo, o_pair[0], o_pair[1]).astype(BF16)


def _natten_bias(rpb):
    c = np.arange(GRID_W)[:, None]
    kc = np.arange(GRID_W)[None, :]
    cstart = np.clip(c - C_WIN_COLS // 2, 0, GRID_W - C_WIN_COLS)
    col_valid = (kc >= cstart) & (kc < cstart + C_WIN_COLS)
    padded = jnp.pad(rpb.astype(F32), ((0, 0), (0, 0), (GRID_W, GRID_W)))
    base = GRID_W + C_WIN_COLS - 1
    blocks = jnp.stack([padded[:, :, base - q:base - q + GRID_W] for q in range(GRID_W)], axis=2)
    blocks = jnp.where(col_valid[None, None], blocks, NEG)
    masked = jnp.full((rpb.shape[0], GRID_W, GRID_W), NEG, F32)
    tables = []
    for off, first in ((0, lambda dr: 0), (-4, lambda dr: dr), (-8, lambda dr: 4)):
        rows = []
        for dr in range(C_ROWS_Q):
            rows.append(jnp.concatenate(
                [blocks[:, off + kri - dr + C_WIN_ROWS - 1]
                 if first(dr) <= kri < first(dr) + C_WIN_ROWS else masked
                 for kri in range(C_ROWS_K)], axis=-1))
        tables.append(jnp.concatenate(rows, axis=1))
    return jnp.stack(tables)


def _natten(z, bias, batch, seq):
    t = batch * seq
    qb = C_ROWS_Q * GRID_W
    nrb = seq // qb
    nkb = C_ROWS_K // C_ROWS_Q

    def kspec(i, col):
        return pl.BlockSpec(
            (qb, C_W), lambda rb, b: (b * nrb + jnp.clip(rb - 1, 0, nrb - nkb) + i, col))

    def variant(rb):
        return jnp.where(rb == 0, 0, jnp.where(rb == nrb - 1, 2, 1))

    return pl.pallas_call(
        _natten_kernel,
        grid=(nrb, batch),
        in_specs=[pl.BlockSpec((qb, C_W), lambda rb, b: (b * nrb + rb, 0))]
        + [kspec(i, 1) for i in range(nkb)]
        + [kspec(i, 2) for i in range(nkb)]
        + [pl.BlockSpec((None, C_HEADS, qb, C_ROWS_K * GRID_W), lambda rb, b: (variant(rb), 0, 0, 0))],
        out_specs=pl.BlockSpec((qb, C_W), lambda rb, b: (b * nrb + rb, 0)),
        out_shape=jax.ShapeDtypeStruct((t, C_W), BF16),
        compiler_params=_cparams(2),
        name="natten",
    )(z, z, z, z, z, z, z, bias)


def _mem_kv_kernel(mem_ref, g_ref, w_ref, kv_ref):
    mb = _rms(mem_ref[...], g_ref[...]).astype(BF16)
    kv_ref[...] = _dot(mb, w_ref[...]).astype(BF16)


def _mem_kv(mem, g, w):
    batch, n_mem, _ = mem.shape
    return pl.pallas_call(
        _mem_kv_kernel,
        grid=(batch,),
        in_specs=[pl.BlockSpec((None, n_mem, D_MODEL), lambda b: (b, 0, 0)),
                  _const_spec((1, D_MODEL)),
                  _const_spec((D_MODEL, 2 * D_MODEL))],
        out_specs=pl.BlockSpec((None, n_mem, 2 * D_MODEL), lambda b: (b, 0, 0)),
        out_shape=jax.ShapeDtypeStruct((batch, n_mem, 2 * D_MODEL), BF16),
        compiler_params=_cparams(1),
        name="mem_kv",
    )(mem, g, w)


def _xattn_tail(x, g_ref, wq_ref, kv_ref, wo_ref, y_ref):
    q = _dot(_rms(x, g_ref[...]).astype(BF16), wq_ref[...]).astype(BF16)
    outs = []
    for h in range(X_HEADS):
        cols = slice(h * X_HEAD_DIM, (h + 1) * X_HEAD_DIM)
        s = _dot_nt(q[:, cols], kv_ref[:, cols])
        m = jnp.max(s, axis=-1, keepdims=True)
        p = jnp.exp(s - m)
        l = jnp.sum(p, axis=-1, keepdims=True)
        o = _dot(p.astype(BF16), kv_ref[:, D_MODEL + h * X_HEAD_DIM:D_MODEL + (h + 1) * X_HEAD_DIM])
        outs.append((o / l).astype(BF16))
    y_ref[...] = x + _dot(jnp.concatenate(outs, axis=-1), wo_ref[...])


def _mix_ab_xattn_kernel(x_ref, oa_ref, r0_ref, r1_ref, r2_ref, permT_ref, woa_ref, wob_ref,
                         g_ref, wq_ref, kv_ref, wo_ref, y_ref):
    ob = _merge_branches([r0_ref[...], r1_ref[...], r2_ref[...]], permT_ref)
    x = x_ref[...] + _dot(oa_ref[...], woa_ref[...]) + _dot(ob, wob_ref[...])
    _xattn_tail(x, g_ref, wq_ref, kv_ref, wo_ref, y_ref)


def _mix_c_xattn_kernel(x_ref, oc_ref, woc_ref, g_ref, wq_ref, kv_ref, wo_ref, y_ref):
    x = x_ref[...] + _dot(oc_ref[...], woc_ref[...])
    _xattn_tail(x, g_ref, wq_ref, kv_ref, wo_ref, y_ref)


def _mix_xattn(body, x, token_args, const_args, g, wq, kv, wo, seq):
    t = x.shape[0]
    tm = TOKEN_TILE
    n_seq_tiles = seq // tm
    n_mem = kv.shape[1]
    token_spec = lambda a: pl.BlockSpec((tm, a.shape[1]), lambda i: (i, 0))
    in_specs = ([token_spec(x)] + [token_spec(a) for a in token_args]
                + [_const_spec(a.shape) for a in const_args]
                + [_const_spec((1, D_MODEL)),
                   _const_spec((D_MODEL, D_MODEL)),
                   pl.BlockSpec((None, n_mem, 2 * D_MODEL), lambda i: (i // n_seq_tiles, 0, 0)),
                   _const_spec((D_MODEL, D_MODEL))])
    return pl.pallas_call(
        body,
        grid=(t // tm,),
        in_specs=in_specs,
        out_specs=pl.BlockSpec((tm, D_MODEL), lambda i: (i, 0)),
        out_shape=jax.ShapeDtypeStruct((t, D_MODEL), F32),
        compiler_params=_cparams(1),
        name="mix_xattn",
    )(x, *token_args, *const_args, g, wq, kv, wo)


def _swiglu_kernel(x_ref, g_ref, wg_ref, wu_ref, wd_ref, gf_ref, y_ref, *, final_norm):
    x = x_ref[...]
    hb = _rms(x, g_ref[...]).astype(BF16)
    gate = _dot(hb, wg_ref[...])
    up = _dot(hb, wu_ref[...])
    act = (gate / (1.0 + jnp.exp(-gate)) * up).astype(BF16)
    y = x + _dot(act, wd_ref[...])
    if final_norm:
        y = _rms(y, gf_ref[...])
    y_ref[...] = y


def _swiglu(x, g, wg, wu, wd, g_final, final_norm, tm=256):
    t = x.shape[0]
    return pl.pallas_call(
        functools.partial(_swiglu_kernel, final_norm=final_norm),
        grid=(t // tm,),
        in_specs=[pl.BlockSpec((tm, D_MODEL), lambda i: (i, 0)),
                  _const_spec((1, D_MODEL)),
                  _const_spec((D_MODEL, D_FF)),
                  _const_spec((D_MODEL, D_FF)),
                  _const_spec((D_FF, D_MODEL)),
                  _const_spec((1, D_MODEL))],
        out_specs=pl.BlockSpec((tm, D_MODEL), lambda i: (i, 0)),
        out_shape=jax.ShapeDtypeStruct((t, D_MODEL), F32),
        compiler_params=_cparams(1),
        name="swiglu",
    )(x, g, wg, wu, wd, g_final)


def _rope_tables(seq):
    tok = jnp.arange(seq, dtype=jnp.int32)
    row = (tok // GRID_W).astype(F32)
    col = (tok % GRID_W).astype(F32)
    axis_dim = HEAD_DIM // 2
    inv_freq = ROPE_THETA ** (-jnp.arange(0, axis_dim, 2, dtype=F32) / axis_dim)
    ang = jnp.concatenate([row[:, None] * inv_freq, col[:, None] * inv_freq], axis=-1)
    return jnp.cos(ang).T, jnp.sin(ang).T


def _prepare_ab(w_in, g_qn, g_kn, w_out):
    perm = np.concatenate([np.arange(0, HEAD_DIM, 2), np.arange(1, HEAD_DIM, 2)])
    cols = np.concatenate(
        [h * HEAD_DIM + perm for h in range(A_Q_HEADS + A_KV_HEADS)]
        + [np.arange(A_Q_W + A_KV_W, A_W)])
    waT = w_in[:, cols].T.astype(BF16)
    zbw = w_in[:, A_W:].reshape(D_MODEL, 3, B_BRANCHES, B_W)
    qscale = HEAD_DIM ** -0.5
    wb = jnp.stack([jnp.concatenate([zbw[:, 0, g] * qscale, zbw[:, 1, g], zbw[:, 2, g]], axis=1)
                    for g in range(B_BRANCHES)]).astype(BF16)
    gq = (g_qn[perm] * (qscale * LOG2E)).reshape(HEAD_DIM, 1).astype(F32)
    gk = g_kn[perm].reshape(HEAD_DIM, 1).astype(F32)
    perms = np.stack([_tile_perm(dil) for _, dil in B_PATTERNS[1:]])
    tile_perm = jnp.asarray(perms, BF16)
    tile_perm_t = jnp.asarray(perms.transpose(0, 2, 1), BF16)
    return waT, wb, tile_perm, tile_perm_t, gq, gk, w_out[:A_Q_W].astype(BF16), w_out[A_Q_W:].astype(BF16)


def _prepare_layers(p):
    depth = p["g_mix"].shape[0]
    row = lambda g: g.reshape(1, D_MODEL)
    layers = []
    for l in range(depth):
        lay = dict(g_mix=row(p["g_mix"][l]), g_xattn=row(p["g_xattn"][l]), g_mem=row(p["g_mem"][l]),
                   g_ffn=row(p["g_ffn"][l]),
                   wq=(p["wq_x"][l] * X_HEAD_DIM ** -0.5).astype(BF16), wkv=p["wkv_x"][l].astype(BF16),
                   wo=p["wo_x"][l].astype(BF16), wg=p["w_gu"][l][:, :D_FF].astype(BF16),
                   wu=p["w_gu"][l][:, D_FF:].astype(BF16), wd=p["w_down"][l].astype(BF16))
        i = l // 2
        if l % 2 == 0:
            lay["ab"] = _prepare_ab(p["w_in_ab"][i], p["g_qn"][i], p["g_kn"][i], p["w_out_ab"][i])
        else:
            w_in = p["w_in_c"][i]
            lay["w_in_c"] = jnp.concatenate(
                [w_in[:, :C_W] * HEAD_DIM ** -0.5, w_in[:, C_W:]], axis=1).astype(BF16)
            lay["bias_c"] = _natten_bias(p["rpb_c"][i])
            lay["w_out_c"] = p["w_out_c"][i].astype(BF16)
        layers.append(lay)
    return layers


def _trunk(x, mem, layers, g_final):
    batch, seq, _ = x.shape
    x = x.reshape(batch * seq, D_MODEL)
    for l, lay in enumerate(layers):
        if l % 2 == 0:
            waT, wb, perm, permT, gq, gk, wo_a, wo_b = lay["ab"]
            cosT, sinT = _rope_tables(seq)
            qT, k, vT, stats, zb = _inproj_ab(x, lay["g_mix"], waT, wb, perm, gq, gk, cosT, sinT, seq)
            oa = _gqa_attention(qT, k, vT, stats, batch, seq)
            rs = [_dilated_branch(zb, g, batch, seq) for g in range(B_BRANCHES)]
            body, token_args, const_args = _mix_ab_xattn_kernel, [oa] + rs, [permT, wo_a, wo_b]
        else:
            z = _inproj(x, lay["g_mix"], lay["w_in_c"])
            oc = _natten(z, lay["bias_c"], batch, seq)
            body, token_args, const_args = _mix_c_xattn_kernel, [oc], [lay["w_out_c"]]
        kv = _mem_kv(mem, lay["g_mem"], lay["wkv"])
        x = _mix_xattn(body, x, token_args, const_args, lay["g_xattn"], lay["wq"], kv, lay["wo"], seq)
        x = _swiglu(x, lay["g_ffn"], lay["wg"], lay["wu"], lay["wd"], g_final,
                    final_norm=(l == len(layers) - 1))
    return x.reshape(batch, seq, D_MODEL)


def kernel(x_prompt, x_sample, mem_prompt, mem_sample, g_mix, w_in_ab, g_qn, g_kn, w_out_ab, w_in_c,
           rpb_c, w_out_c, g_xattn, g_mem, wq_x, wkv_x, wo_x, g_ffn, w_gu, w_down, g_final):
    layers = _prepare_layers(dict(
        g_mix=g_mix, w_in_ab=w_in_ab, g_qn=g_qn, g_kn=g_kn, w_out_ab=w_out_ab, w_in_c=w_in_c,
        rpb_c=rpb_c, w_out_c=w_out_c, g_xattn=g_xattn, g_mem=g_mem, wq_x=wq_x, wkv_x=wkv_x,
        wo_x=wo_x, g_ffn=g_ffn, w_gu=w_gu, w_down=w_down))
    g_final = g_final.reshape(1, D_MODEL)
    return (_trunk(x_prompt, mem_prompt, layers, g_final), _trunk(x_sample, mem_sample, layers, g_final))
```

```python
import functools
import math

import numpy as np
import jax
import jax.numpy as jnp
from jax import lax
from jax.experimental import pallas as pl
from jax.experimental.pallas import tpu as pltpu

F32 = jnp.float32
BF16 = jnp.bfloat16

D_MODEL = 1024
GRID_W = 64
HEAD_DIM = 64
EPS = 1e-6
NEG = -1e30
LOG2E = math.log2(math.e)
TOKEN_TILE = 512

A_Q_HEADS = 8
A_KV_HEADS = 2
A_GROUP = A_Q_HEADS // A_KV_HEADS
ROPE_THETA = 10000.0
A_Q_W = A_Q_HEADS * HEAD_DIM
A_KV_W = A_KV_HEADS * HEAD_DIM
A_W = A_Q_W + 2 * A_KV_W
A_TK = 256
A_V_ROWS = HEAD_DIM + 16
A_UNROLL = 6
A_UNROLL_BOUNDED = 16
A_SHIFT_MARGIN = 60.0
A_BOUND_LIMIT = 90.0
A_BOUND_SLACK = 1.0 + 2.0 ** -6

B_PATTERNS = ((128, 1), (512, 4), (2048, 16))
B_BRANCHES = len(B_PATTERNS)
B_HEADS = 4
B_HALF = 64
B_W = B_HEADS * HEAD_DIM
B_SUB = 2 * B_HALF
B_KEYS = 4 * B_HALF
LSE_REP = 2 * HEAD_DIM // B_HEADS
B_R = B_W + 4 * HEAD_DIM

C_HEADS = 16
C_WIN_ROWS = 8
C_WIN_COLS = 16
C_ROWS_Q = 4
C_ROWS_K = 12
C_W = C_HEADS * HEAD_DIM

X_HEADS = 4
X_HEAD_DIM = D_MODEL // X_HEADS
D_FF = 2816

VMEM_LIMIT = 56 * 1024 * 1024


def _cparams(n_axes):
    return pltpu.CompilerParams(dimension_semantics=("arbitrary",) * n_axes,
                                vmem_limit_bytes=VMEM_LIMIT)


def _rms(x, g):
    ms = jnp.mean(x * x, axis=-1, keepdims=True)
    return x * lax.rsqrt(ms + EPS) * g


def _dot(a, b):
    return jnp.dot(a, b, preferred_element_type=F32)


def _dot_nt(a, b):
    return lax.dot_general(a, b, (((1,), (1,)), ((), ())), preferred_element_type=F32)


def _const_spec(shape):
    zeros = (0,) * len(shape)
    return pl.BlockSpec(shape, lambda *_: zeros)


def _inproj_ab_kernel(x_ref, g_ref, waT_ref, wb_ref, perm_ref, gq_ref, gk_ref, cos_ref, sin_ref,
                      qT_ref, k_ref, vT_ref, stats_ref, zb_ref):
    hb = _rms(x_ref[...], g_ref[...]).astype(BF16)
    zT = _dot_nt(waT_ref[...], hb)
    cos = cos_ref[...]
    sin = sin_ref[...]
    half = HEAD_DIM // 2

    def norm_rope(zh, gcol):
        ms = jnp.mean(zh * zh, axis=0, keepdims=True)
        y = zh * lax.rsqrt(ms + EPS) * gcol
        yr, yi = y[:half], y[half:]
        return jnp.concatenate([yr * cos - yi * sin, yr * sin + yi * cos], axis=0)

    def l2(v):
        return jnp.sqrt(jnp.sum(v * v, axis=0, keepdims=True))

    stats_ref[...] = jnp.zeros(stats_ref.shape, F32)
    for h in range(A_Q_HEADS):
        sl = slice(h * HEAD_DIM, (h + 1) * HEAD_DIM)
        q = norm_rope(zT[sl], gq_ref[...])
        qT_ref[sl, :] = q.astype(BF16)
        stats_ref[h // A_GROUP, h % A_GROUP:h % A_GROUP + 1, :] = l2(q)
    ks = [norm_rope(zT[A_Q_W + h * HEAD_DIM:A_Q_W + (h + 1) * HEAD_DIM], gk_ref[...])
          for h in range(A_KV_HEADS)]
    for h in range(A_KV_HEADS):
        stats_ref[h, A_GROUP:A_GROUP + 1, :] = l2(ks[h])
    kT = jnp.concatenate(ks, axis=0)
    k_nat = kT.T
    for h in range(A_KV_HEADS):
        k_ref[h] = k_nat[:, h * HEAD_DIM:(h + 1) * HEAD_DIM].astype(BF16)
    for h in range(A_KV_HEADS):
        for c in range(vT_ref.shape[1]):
            vT_ref[h, c] = zT[A_Q_W + A_KV_W + h * HEAD_DIM:A_Q_W + A_KV_W + (h + 1) * HEAD_DIM,
                              c * A_TK:(c + 1) * A_TK].astype(BF16)
    zb_ref[0] = _dot(hb, wb_ref[0]).astype(BF16)
    for g in range(1, B_BRANCHES):
        zb_ref[g] = _dot(perm_ref[g - 1], _dot(hb, wb_ref[g]).astype(BF16)).astype(BF16)


def _inproj_ab(x, g_mix, waT, wb, perm, gq, gk, cosT, sinT, seq):
    t = x.shape[0]
    tm = TOKEN_TILE
    n_seq_tiles = seq // tm
    return pl.pallas_call(
        _inproj_ab_kernel,
        grid=(t // tm,),
        in_specs=[
            pl.BlockSpec((tm, D_MODEL), lambda i: (i, 0)),
            _const_spec((1, D_MODEL)),
            _const_spec((A_W, D_MODEL)),
            _const_spec((B_BRANCHES, D_MODEL, 3 * B_W)),
            _const_spec((B_BRANCHES - 1, tm, tm)),
            _const_spec((HEAD_DIM, 1)),
            _const_spec((HEAD_DIM, 1)),
            pl.BlockSpec((HEAD_DIM // 2, tm), lambda i: (0, i % n_seq_tiles)),
            pl.BlockSpec((HEAD_DIM // 2, tm), lambda i: (0, i % n_seq_tiles)),
        ],
        out_specs=[
            pl.BlockSpec((A_Q_W, tm), lambda i: (0, i)),
            pl.BlockSpec((A_KV_HEADS, tm, HEAD_DIM), lambda i: (0, i, 0)),
            pl.BlockSpec((A_KV_HEADS, tm // A_TK, HEAD_DIM, A_TK), lambda i: (0, i, 0, 0)),
            pl.BlockSpec((A_KV_HEADS, 8, tm), lambda i: (0, 0, i)),
            pl.BlockSpec((B_BRANCHES, tm, 3 * B_W), lambda i: (0, i, 0)),
        ],
        out_shape=[
            jax.ShapeDtypeStruct((A_Q_W, t), BF16),
            jax.ShapeDtypeStruct((A_KV_HEADS, t, HEAD_DIM), BF16),
            jax.ShapeDtypeStruct((A_KV_HEADS, t // A_TK, HEAD_DIM, A_TK), BF16),
            jax.ShapeDtypeStruct((A_KV_HEADS, 8, t), F32),
            jax.ShapeDtypeStruct((B_BRANCHES, t, 3 * B_W), BF16),
        ],
        compiler_params=_cparams(1),
        name="inproj_ab",
    )(x, g_mix, waT, wb, perm, gq, gk, cosT, sinT)


def _gqa_kernel(qT_ref, k_ref, vT_ref, o_ref, s_scr, p_scr, cm_scr, alpha_scr, m_scr, acc_scr):
    tq = qT_ref.shape[1]
    tk = A_TK
    n_chunks = vT_ref.shape[0]
    ones = jnp.ones((A_V_ROWS - HEAD_DIM, tk), BF16)

    def scores(chunk, slot):
        k = k_ref[pl.ds(pl.multiple_of(chunk * tk, tk), tk), :]
        for h in range(A_GROUP):
            s = _dot(k, qT_ref[h * HEAD_DIM:(h + 1) * HEAD_DIM, :])
            s_scr[slot, h] = s
            cm_scr[slot, h] = jnp.max(s.reshape(tk // 8, 8, tq), axis=0)

    def probs(slot):
        for h in range(A_GROUP):
            m_old = m_scr[h]
            m_new = jnp.maximum(m_old, jnp.max(cm_scr[slot, h], axis=0, keepdims=True))
            alpha_scr[slot, h] = jnp.exp2(m_old - m_new)
            m_scr[h] = m_new
            p_scr[slot, h] = jnp.exp2(s_scr[slot, h] - m_new).astype(BF16)

    def accumulate(chunk, slot):
        vT_aug = jnp.concatenate([vT_ref[chunk], ones], axis=0)
        for h in range(A_GROUP):
            acc_scr[h] = alpha_scr[slot, h] * acc_scr[h] + _dot(vT_aug, p_scr[slot, h])

    def steady(c, count):
        for u in range(count):
            accumulate(c + u, u % 2)
            probs((u + 1) % 2)
            scores(c + u + 2, u % 2)

    m_scr[...] = jnp.full(m_scr.shape, NEG, F32)
    acc_scr[...] = jnp.zeros(acc_scr.shape, F32)
    scores(0, 0)
    scores(1, 1)
    probs(0)
    n_loop = (n_chunks - 2) // A_UNROLL

    def body(i, carry):
        steady(i * A_UNROLL, A_UNROLL)
        return carry

    if n_loop > 1:
        lax.fori_loop(0, n_loop, body, 0)
    else:
        steady(0, n_loop * A_UNROLL)
    steady(n_loop * A_UNROLL, (n_chunks - 2) % A_UNROLL)
    accumulate(n_chunks - 2, 0)
    probs(1)
    accumulate(n_chunks - 1, 1)
    oT = jnp.concatenate(
        [acc_scr[h, :HEAD_DIM, :] / acc_scr[h, HEAD_DIM:HEAD_DIM + 1, :] for h in range(A_GROUP)], axis=0)
    o_ref[...] = oT.T.astype(BF16)


def _gqa_bounded_kernel(qT_ref, k_ref, vT_ref, shift_ref, o_ref, p_scr, acc_scr):
    tq = qT_ref.shape[1]
    tk = A_TK
    n_chunks = vT_ref.shape[0]
    ones = jnp.ones((A_V_ROWS - HEAD_DIM, tk), BF16)

    def probs(chunk, slot):
        k = k_ref[pl.ds(pl.multiple_of(chunk * tk, tk), tk), :]
        for h in range(A_GROUP):
            s = _dot(k, qT_ref[h * HEAD_DIM:(h + 1) * HEAD_DIM, :])
            p_scr[slot, h] = jnp.exp2(s - shift_ref[h:h + 1, :]).astype(BF16)

    def accumulate(chunk, slot):
        vT_aug = jnp.concatenate([vT_ref[chunk], ones], axis=0)
        for h in range(A_GROUP):
            acc_scr[h] += _dot(vT_aug, p_scr[slot, h])

    def steady(c, count):
        for u in range(count):
            accumulate(c + u, u % 2)
            probs(c + u + 2, u % 2)

    acc_scr[...] = jnp.zeros(acc_scr.shape, F32)
    probs(0, 0)
    probs(1, 1)
    n_loop = (n_chunks - 2) // A_UNROLL_BOUNDED

    def body(i, carry):
        steady(i * A_UNROLL_BOUNDED, A_UNROLL_BOUNDED)
        return carry

    if n_loop > 1:
        lax.fori_loop(0, n_loop, body, 0)
    else:
        steady(0, n_loop * A_UNROLL_BOUNDED)
    steady(n_loop * A_UNROLL_BOUNDED, (n_chunks - 2) % A_UNROLL_BOUNDED)
    accumulate(n_chunks - 2, 0)
    accumulate(n_chunks - 1, 1)
    oT = jnp.concatenate(
        [acc_scr[h, :HEAD_DIM, :] / acc_scr[h, HEAD_DIM:HEAD_DIM + 1, :] for h in range(A_GROUP)], axis=0)
    o_ref[...] = oT.T.astype(BF16)


def _gqa_attention(qT, k, vT, stats, batch, seq, tq=256):
    t = batch * seq
    nq = seq // tq
    k4 = k.reshape(A_KV_HEADS, batch, seq, HEAD_DIM)
    v5 = vT.reshape(A_KV_HEADS, batch, seq // A_TK, HEAD_DIM, A_TK)
    kmax = jnp.max(stats[:, A_GROUP].reshape(A_KV_HEADS, batch, seq), axis=-1)
    bound = (stats.reshape(A_KV_HEADS, 8, batch, seq) * kmax[:, None, :, None]
             * A_BOUND_SLACK).reshape(A_KV_HEADS, 8, t)
    in_specs = [
        pl.BlockSpec((A_GROUP * HEAD_DIM, tq), lambda b, g, i: (g, b * nq + i)),
        pl.BlockSpec((None, None, seq, HEAD_DIM), lambda b, g, i: (g, b, 0, 0)),
        pl.BlockSpec((None, None, seq // A_TK, HEAD_DIM, A_TK), lambda b, g, i: (g, b, 0, 0, 0)),
    ]
    common = dict(
        grid=(batch, A_KV_HEADS, nq),
        out_specs=pl.BlockSpec((tq, A_GROUP * HEAD_DIM), lambda b, g, i: (b * nq + i, g)),
        out_shape=jax.ShapeDtypeStruct((t, A_Q_W), BF16),
        compiler_params=_cparams(3),
    )

    def bounded():
        return pl.pallas_call(
            _gqa_bounded_kernel,
            in_specs=in_specs + [pl.BlockSpec((None, 8, tq), lambda b, g, i: (g, 0, b * nq + i))],
            scratch_shapes=[pltpu.VMEM((2, A_GROUP, A_TK, tq), BF16),
                            pltpu.VMEM((A_GROUP, A_V_ROWS, tq), F32)],
            name="gqa_bounded", **common,
        )(qT, k4, v5, bound - A_SHIFT_MARGIN)

    def online():
        return pl.pallas_call(
            _gqa_kernel,
            in_specs=in_specs,
            scratch_shapes=[pltpu.VMEM((2, A_GROUP, A_TK, tq), F32),
                            pltpu.VMEM((2, A_GROUP, A_TK, tq), BF16),
                            pltpu.VMEM((2, A_GROUP, 8, tq), F32),
                            pltpu.VMEM((2, A_GROUP, 1, tq), F32),
                            pltpu.VMEM((A_GROUP, 1, tq), F32),
                            pltpu.VMEM((A_GROUP, A_V_ROWS, tq), F32)],
            name="gqa_attention", **common,
        )(qT, k4, v5)

    safe = jnp.max(bound[:, :A_GROUP]) <= A_BOUND_LIMIT
    return lax.cond(safe, bounded, online)


def _tile_perm(dil):
    c = TOKEN_TILE // dil
    dst = np.arange(TOKEN_TILE)
    src = (dst % c) * dil + dst // c
    p = np.zeros((TOKEN_TILE, TOKEN_TILE), np.float32)
    p[dst, src] = 1.0
    return p


def _seq_rows(ref, start, n, cols):
    c = ref.shape[1]
    if n <= c:
        return ref[start // c, start % c:start % c + n, cols]
    blocks = ref[start // c:(start + n) // c, :, cols]
    return blocks.reshape(n, blocks.shape[-1])


def _store_seq_rows(ref, start, cols, val):
    c = ref.shape[1]
    n = val.shape[0]
    if n <= c:
        ref[start // c, start % c:start % c + n, cols] = val
    else:
        ref[start // c:(start + n) // c, :, cols] = val.reshape(n // c, c, val.shape[-1])


def _dilated_kernel(main_ref, prev_ref, next_ref, bias_ref, r_ref, kv_scr, *, seq_len, cq):
    n = pl.program_id(2)
    kv_cols = slice(B_W, 3 * B_W)
    pair_w = 2 * HEAD_DIM
    lane = lax.broadcasted_iota(jnp.int32, (1, pair_w), 1)
    lo = lane < HEAD_DIM
    key_col = lax.broadcasted_iota(jnp.int32, (1, B_KEYS), 1)

    for rr in range(main_ref.shape[1]):
        main, res = main_ref.at[:, rr], r_ref.at[:, rr]
        kv_scr[0:B_HALF, :] = prev_ref[..., rr, :, kv_cols].reshape(B_HALF, 2 * B_W)
        for j in range(cq // B_SUB):
            kv_scr[B_HALF + j * B_SUB:B_HALF + (j + 1) * B_SUB, :] = _seq_rows(main, j * B_SUB, B_SUB, kv_cols)
        kv_scr[B_HALF + cq:, :] = next_ref[..., rr, :, kv_cols].reshape(B_HALF, 2 * B_W)

        for j in range(cq // B_SUB):
            key_pos = key_col + (n * cq + j * B_SUB - B_HALF)
            edge = jnp.where((key_pos >= 0) & (key_pos < seq_len), 0.0, NEG).astype(F32)
            krows = slice(j * B_SUB, j * B_SUB + B_KEYS)
            lses = []
            for pr in range(B_HEADS // 2):
                cols = slice(pr * pair_w, (pr + 1) * pair_w)
                q2 = _seq_rows(main, j * B_SUB, B_SUB, cols)
                k2 = kv_scr[krows, cols]
                v2 = kv_scr[krows, B_W + pr * pair_w:B_W + (pr + 1) * pair_w]
                o_pair = []
                for hh in range(2):
                    qm = jnp.where(lo if hh == 0 else ~lo, q2, jnp.zeros_like(q2))
                    s = _dot_nt(qm, k2) + bias_ref[pr * 2 + hh] + edge
                    m = jnp.max(s, axis=-1, keepdims=True)
                    p = jnp.exp(s - m)
                    l = jnp.sum(p, axis=-1, keepdims=True)
                    o_pair.append(_dot(p.astype(BF16), v2) / l)
                    lses.append(m + jnp.log(l))
                _store_seq_rows(res, j * B_SUB, cols, jnp.where(lo, o_pair[0], o_pair[1]).astype(BF16))
            tile = lses[B_HEADS - 1]
            for head in range(B_HEADS - 2, -1, -1):
                tile = jnp.where(lane < (head + 1) * LSE_REP, lses[head], tile)
            hi = tile.astype(BF16)
            _store_seq_rows(res, j * B_SUB, slice(B_W, B_W + pair_w), hi)
            _store_seq_rows(res, j * B_SUB, slice(B_W + pair_w, B_R),
                            (tile - hi.astype(F32)).astype(BF16))


def _dilated_bias(branch):
    _, dil = B_PATTERNS[branch]
    slopes = np.exp2(-8.0 * np.arange(1, B_BRANCHES * B_HEADS + 1, dtype=np.float64)
                     / (B_BRANCHES * B_HEADS)).reshape(B_BRANCHES, B_HEADS)[branch]
    a = np.arange(B_SUB)[:, None]
    c = np.arange(B_KEYS)[None, :]
    rel = (c - B_HALF) - a
    bias = -slopes[:, None, None] * (np.abs(rel) * dil).astype(np.float64)[None]
    bias = np.where((np.abs(rel) <= B_HALF)[None], bias, NEG)
    return jnp.asarray(bias, F32)


def _dilated_branch(zb, branch, batch, seq):
    _, dil = B_PATTERNS[branch]
    t = batch * seq
    seq_len = seq // dil
    c = TOKEN_TILE // dil
    n_tiles = seq // TOKEN_TILE
    cq = min(512, seq_len)
    nchunk = seq_len // cq
    n_halo = seq_len // B_HALF
    per = cq // B_HALF
    rb = min(dil, max(1, 512 // cq))
    zv = zb.reshape(B_BRANCHES, batch, n_tiles, dil, c, 3 * B_W)

    def halo_spec(pos):
        if c >= B_HALF:
            sub = c // B_HALF
            z6 = zb.reshape(B_BRANCHES, batch, n_tiles, dil, sub, B_HALF, 3 * B_W)
            return z6, pl.BlockSpec((None, None, None, rb, None, B_HALF, 3 * B_W),
                                    lambda b, r, n: (branch, b, pos(n) // sub, r, pos(n) % sub, 0, 0))
        return zv, pl.BlockSpec((None, None, B_HALF // c, rb, c, 3 * B_W),
                                lambda b, r, n: (branch, b, pos(n), r, 0, 0))

    prev_arr, prev_spec = halo_spec(lambda n: jnp.maximum(n * per - 1, 0))
    next_arr, next_spec = halo_spec(lambda n: jnp.minimum((n + 1) * per, n_halo - 1))
    res = pl.pallas_call(
        functools.partial(_dilated_kernel, seq_len=seq_len, cq=cq),
        grid=(batch, dil // rb, nchunk),
        in_specs=[pl.BlockSpec((None, None, cq // c, rb, c, 3 * B_W), lambda b, r, n: (branch, b, n, r, 0, 0)),
                  prev_spec, next_spec, _const_spec((B_HEADS, B_SUB, B_KEYS))],
        out_specs=pl.BlockSpec((None, cq // c, rb, c, B_R), lambda b, r, n: (b, n, r, 0, 0)),
        out_shape=jax.ShapeDtypeStruct((batch, n_tiles, dil, c, B_R), BF16),
        scratch_shapes=[pltpu.VMEM((cq + 2 * B_HALF, 2 * B_W), BF16)],
        compiler_params=_cparams(3),
        name=f"dilated_branch{branch}",
    )(zv, prev_arr, next_arr, _dilated_bias(branch))
    return res.reshape(t, B_R)


def _merge_branches(rs, permT_ref):
    pair_w = 2 * HEAD_DIM
    lo = lax.broadcasted_iota(jnp.int32, (1, pair_w), 1) < HEAD_DIM
    tiles = [rs[0].astype(F32)] + [_dot(permT_ref[g - 1], rs[g]) for g in range(1, B_BRANCHES)]
    lses = [tl[:, B_W:B_W + pair_w] + tl[:, B_W + pair_w:] for tl in tiles]
    mx = functools.reduce(jnp.maximum, lses)
    es = [jnp.exp(l - mx) for l in lses]
    inv = 1.0 / functools.reduce(lambda u, v: u + v, es)
    pairs = []
    for pr in range(B_HEADS // 2):
        acc = None
        for tl, e in zip(tiles, es):
            w = e * inv
            w2 = jnp.where(lo, w[:, 2 * pr * LSE_REP:2 * pr * LSE_REP + 1],
                           w[:, (2 * pr + 1) * LSE_REP:(2 * pr + 1) * LSE_REP + 1])
            term = w2 * tl[:, pr * pair_w:(pr + 1) * pair_w]
            acc = term if acc is None else acc + term
        pairs.append(acc)
    return jnp.concatenate(pairs, axis=-1).astype(BF16)


def _inproj_kernel(x_ref, g_ref, w_ref, z_ref):
    hb = _rms(x_ref[...], g_ref[...]).astype(BF16)
    z_ref[...] = _dot(hb, w_ref[...]).astype(BF16)


def _inproj(x, g, w):
    t = x.shape[0]
    tm = TOKEN_TILE
    n_out = w.shape[1]
    return pl.pallas_call(
        _inproj_kernel,
        grid=(t // tm,),
        in_specs=[pl.BlockSpec((tm, D_MODEL), lambda i: (i, 0)),
                  _const_spec((1, D_MODEL)),
                  _const_spec((D_MODEL, n_out))],
        out_specs=pl.BlockSpec((tm, n_out), lambda i: (i, 0)),
        out_shape=jax.ShapeDtypeStruct((t, n_out), BF16),
        compiler_params=_cparams(1),
        name="inproj_c",
    )(x, g, w)


def _natten_kernel(q_ref, k0_ref, k1_ref, k2_ref, v0_ref, v1_ref, v2_ref, bias_ref, o_ref):
    k_refs = (k0_ref, k1_ref, k2_ref)
    v_refs = (v0_ref, v1_ref, v2_ref)
    lane = lax.broadcasted_iota(jnp.int32, (1, 2 * HEAD_DIM), 1)
    lo = lane < HEAD_DIM
    kb = k0_ref.shape[0]
    for pr in range(C_HEADS // 2):
        cols = slice(pr * 2 * HEAD_DIM, (pr + 1) * 2 * HEAD_DIM)
        q2 = q_ref[:, cols]
        ks = [r[:, cols] for r in k_refs]
        vs = [r[:, cols] for r in v_refs]
        o_pair = []
        for hh in range(2):
            qm = jnp.where(lo if hh == 0 else ~lo, q2, jnp.zeros_like(q2))
            s = jnp.concatenate([_dot_nt(qm, kk) for kk in ks], axis=-1) + bias_ref[pr * 2 + hh]
            m = jnp.max(s, axis=-1, keepdims=True)
            p = jnp.exp(s - m)
            l = jnp.sum(p, axis=-1, keepdims=True)
            pb = p.astype(BF16)
            o = _dot(pb[:, :kb], vs[0])
            for i in range(1, len(vs)):
                o = o + _dot(pb[:, i * kb:(i + 1) * kb], vs[i])
            o_pair.append(o / l)
        o_ref[:, cols] = jnp.where(lo, o_pair[0], o_pair[1]).astype(BF16)


def _natten_bias(rpb):
    c = np.arange(GRID_W)[:, None]
    kc = np.arange(GRID_W)[None, :]
    cstart = np.clip(c - C_WIN_COLS // 2, 0, GRID_W - C_WIN_COLS)
    col_valid = (kc >= cstart) & (kc < cstart + C_WIN_COLS)
    padded = jnp.pad(rpb.astype(F32), ((0, 0), (0, 0), (GRID_W, GRID_W)))
    base = GRID_W + C_WIN_COLS - 1
    blocks = jnp.stack([padded[:, :, base - q:base - q + GRID_W] for q in range(GRID_W)], axis=2)
    blocks = jnp.where(col_valid[None, None], blocks, NEG)
    masked = jnp.full((rpb.shape[0], GRID_W, GRID_W), NEG, F32)
    tables = []
    for off, first in ((0, lambda dr: 0), (-4, lambda dr: dr), (-8, lambda dr: 4)):
        rows = []
        for dr in range(C_ROWS_Q):
            rows.append(jnp.concatenate(
                [blocks[:, off + kri - dr + C_WIN_ROWS - 1]
                 if first(dr) <= kri < first(dr) + C_WIN_ROWS else masked
                 for kri in range(C_ROWS_K)], axis=-1))
        tables.append(jnp.concatenate(rows, axis=1))
    return jnp.stack(tables)


def _natten(z, bias, batch, seq):
    t = batch * seq
    qb = C_ROWS_Q * GRID_W
    nrb = seq // qb
    nkb = C_ROWS_K // C_ROWS_Q

    def kspec(i, col):
        return pl.BlockSpec(
            (qb, C_W), lambda rb, b: (b * nrb + jnp.clip(rb - 1, 0, nrb - nkb) + i, col))

    def variant(rb):
        return jnp.where(rb == 0, 0, jnp.where(rb == nrb - 1, 2, 1))

    return pl.pallas_call(
        _natten_kernel,
        grid=(nrb, batch),
        in_specs=[pl.BlockSpec((qb, C_W), lambda rb, b: (b * nrb + rb, 0))]
        + [kspec(i, 1) for i in range(nkb)]
        + [kspec(i, 2) for i in range(nkb)]
        + [pl.BlockSpec((None, C_HEADS, qb, C_ROWS_K * GRID_W), lambda rb, b: (variant(rb), 0, 0, 0))],
        out_specs=pl.BlockSpec((qb, C_W), lambda rb, b: (b * nrb + rb, 0)),
        out_shape=jax.ShapeDtypeStruct((t, C_W), BF16),
        compiler_params=_cparams(2),
        name="natten",
    )(z, z, z, z, z, z, z, bias)


def _mem_kv_kernel(mem_ref, g_ref, w_ref, kv_ref):
    mb = _rms(mem_ref[...], g_ref[...]).astype(BF16)
    kv_ref[...] = _dot(mb, w_ref[...]).astype(BF16)


def _mem_kv(mem, g, w):
    batch, n_mem, _ = mem.shape
    return pl.pallas_call(
        _mem_kv_kernel,
        grid=(batch,),
        in_specs=[pl.BlockSpec((None, n_mem, D_MODEL), lambda b: (b, 0, 0)),
                  _const_spec((1, D_MODEL)),
                  _const_spec((D_MODEL, 2 * D_MODEL))],
        out_specs=pl.BlockSpec((None, n_mem, 2 * D_MODEL), lambda b: (b, 0, 0)),
        out_shape=jax.ShapeDtypeStruct((batch, n_mem, 2 * D_MODEL), BF16),
        compiler_params=_cparams(1),
        name="mem_kv",
    )(mem, g, w)


def _xattn_tail(x, g_ref, wq_ref, kv_ref, wo_ref, y_ref):
    q = _dot(_rms(x, g_ref[...]).astype(BF16), wq_ref[...]).astype(BF16)
    outs = []
    for h in range(X_HEADS):
        cols = slice(h * X_HEAD_DIM, (h + 1) * X_HEAD_DIM)
        s = _dot_nt(q[:, cols], kv_ref[:, cols])
        m = jnp.max(s, axis=-1, keepdims=True)
        p = jnp.exp(s - m)
        l = jnp.sum(p, axis=-1, keepdims=True)
        o = _dot(p.astype(BF16), kv_ref[:, D_MODEL + h * X_HEAD_DIM:D_MODEL + (h + 1) * X_HEAD_DIM])
        outs.append((o / l).astype(BF16))
    y_ref[...] = x + _dot(jnp.concatenate(outs, axis=-1), wo_ref[...])


def _mix_ab_xattn_kernel(x_ref, oa_ref, r0_ref, r1_ref, r2_ref, permT_ref, woa_ref, wob_ref,
                         g_ref, wq_ref, kv_ref, wo_ref, y_ref):
    ob = _merge_branches([r0_ref[...], r1_ref[...], r2_ref[...]], permT_ref)
    x = x_ref[...] + _dot(oa_ref[...], woa_ref[...]) + _dot(ob, wob_ref[...])
    _xattn_tail(x, g_ref, wq_ref, kv_ref, wo_ref, y_ref)


def _mix_c_xattn_kernel(x_ref, oc_ref, woc_ref, g_ref, wq_ref, kv_ref, wo_ref, y_ref):
    x = x_ref[...] + _dot(oc_ref[...], woc_ref[...])
    _xattn_tail(x, g_ref, wq_ref, kv_ref, wo_ref, y_ref)


def _mix_xattn(body, x, token_args, const_args, g, wq, kv, wo, seq):
    t = x.shape[0]
    tm = TOKEN_TILE
    n_seq_tiles = seq // tm
    n_mem = kv.shape[1]
    token_spec = lambda a: pl.BlockSpec((tm, a.shape[1]), lambda i: (i, 0))
    in_specs = ([token_spec(x)] + [token_spec(a) for a in token_args]
                + [_const_spec(a.shape) for a in const_args]
                + [_const_spec((1, D_MODEL)),
                   _const_spec((D_MODEL, D_MODEL)),
                   pl.BlockSpec((None, n_mem, 2 * D_MODEL), lambda i: (i // n_seq_tiles, 0, 0)),
                   _const_spec((D_MODEL, D_MODEL))])
    return pl.pallas_call(
        body,
        grid=(t // tm,),
        in_specs=in_specs,
        out_specs=pl.BlockSpec((tm, D_MODEL), lambda i: (i, 0)),
        out_shape=jax.ShapeDtypeStruct((t, D_MODEL), F32),
        compiler_params=_cparams(1),
        name="mix_xattn",
    )(x, *token_args, *const_args, g, wq, kv, wo)


def _swiglu_kernel(x_ref, g_ref, wg_ref, wu_ref, wd_ref, gf_ref, y_ref, *, final_norm):
    x = x_ref[...]
    hb = _rms(x, g_ref[...]).astype(BF16)
    gate = _dot(hb, wg_ref[...])
    up = _dot(hb, wu_ref[...])
    act = (gate / (1.0 + jnp.exp(-gate)) * up).astype(BF16)
    y = x + _dot(act, wd_ref[...])
    if final_norm:
        y = _rms(y, gf_ref[...])
    y_ref[...] = y


def _swiglu(x, g, wg, wu, wd, g_final, final_norm, tm=256):
    t = x.shape[0]
    return pl.pallas_call(
        functools.partial(_swiglu_kernel, final_norm=final_norm),
        grid=(t // tm,),
        in_specs=[pl.BlockSpec((tm, D_MODEL), lambda i: (i, 0)),
                  _const_spec((1, D_MODEL)),
                  _const_spec((D_MODEL, D_FF)),
                  _const_spec((D_MODEL, D_FF)),
                  _const_spec((D_FF, D_MODEL)),
                  _const_spec((1, D_MODEL))],
        out_specs=pl.BlockSpec((tm, D_MODEL), lambda i: (i, 0)),
        out_shape=jax.ShapeDtypeStruct((t, D_MODEL), F32),
        compiler_params=_cparams(1),
        name="swiglu",
    )(x, g, wg, wu, wd, g_final)


def _rope_tables(seq):
    tok = jnp.arange(seq, dtype=jnp.int32)
    row = (tok // GRID_W).astype(F32)
    col = (tok % GRID_W).astype(F32)
    axis_dim = HEAD_DIM // 2
    inv_freq = ROPE_THETA ** (-jnp.arange(0, axis_dim, 2, dtype=F32) / axis_dim)
    ang = jnp.concatenate([row[:, None] * inv_freq, col[:, None] * inv_freq], axis=-1)
    return jnp.cos(ang).T, jnp.sin(ang).T


def _prepare_ab(w_in, g_qn, g_kn, w_out):
    perm = np.concatenate([np.arange(0, HEAD_DIM, 2), np.arange(1, HEAD_DIM, 2)])
    cols = np.concatenate(
        [h * HEAD_DIM + perm for h in range(A_Q_HEADS + A_KV_HEADS)]
        + [np.arange(A_Q_W + A_KV_W, A_W)])
    waT = w_in[:, cols].T.astype(BF16)
    zbw = w_in[:, A_W:].reshape(D_MODEL, 3, B_BRANCHES, B_W)
    qscale = HEAD_DIM ** -0.5
    wb = jnp.stack([jnp.concatenate([zbw[:, 0, g] * qscale, zbw[:, 1, g], zbw[:, 2, g]], axis=1)
                    for g in range(B_BRANCHES)]).astype(BF16)
    gq = (g_qn[perm] * (qscale * LOG2E)).reshape(HEAD_DIM, 1).astype(F32)
    gk = g_kn[perm].reshape(HEAD_DIM, 1).astype(F32)
    perms = np.stack([_tile_perm(dil) for _, dil in B_PATTERNS[1:]])
    tile_perm = jnp.asarray(perms, BF16)
    tile_perm_t = jnp.asarray(perms.transpose(0, 2, 1), BF16)
    return waT, wb, tile_perm, tile_perm_t, gq, gk, w_out[:A_Q_W].astype(BF16), w_out[A_Q_W:].astype(BF16)


def _prepare_layers(p):
    depth = p["g_mix"].shape[0]
    row = lambda g: g.reshape(1, D_MODEL)
    layers = []
    for l in range(depth):
        lay = dict(g_mix=row(p["g_mix"][l]), g_xattn=row(p["g_xattn"][l]), g_mem=row(p["g_mem"][l]),
                   g_ffn=row(p["g_ffn"][l]),
                   wq=(p["wq_x"][l] * X_HEAD_DIM ** -0.5).astype(BF16), wkv=p["wkv_x"][l].astype(BF16),
                   wo=p["wo_x"][l].astype(BF16), wg=p["w_gu"][l][:, :D_FF].astype(BF16),
                   wu=p["w_gu"][l][:, D_FF:].astype(BF16), wd=p["w_down"][l].astype(BF16))
        i = l // 2
        if l % 2 == 0:
            lay["ab"] = _prepare_ab(p["w_in_ab"][i], p["g_qn"][i], p["g_kn"][i], p["w_out_ab"][i])
        else:
            w_in = p["w_in_c"][i]
            lay["w_in_c"] = jnp.concatenate(
                [w_in[:, :C_W] * HEAD_DIM ** -0.5, w_in[:, C_W:]], axis=1).astype(BF16)
            lay["bias_c"] = _natten_bias(p["rpb_c"][i])
            lay["w_out_c"] = p["w_out_c"][i].astype(BF16)
        layers.append(lay)
    return layers


def _trunk(x, mem, layers, g_final):
    batch, seq, _ = x.shape
    x = x.reshape(batch * seq, D_MODEL)
    for l, lay in enumerate(layers):
        if l % 2 == 0:
            waT, wb, perm, permT, gq, gk, wo_a, wo_b = lay["ab"]
            cosT, sinT = _rope_tables(seq)
            qT, k, vT, stats, zb = _inproj_ab(x, lay["g_mix"], waT, wb, perm, gq, gk, cosT, sinT, seq)
            oa = _gqa_attention(qT, k, vT, stats, batch, seq)
            rs = [_dilated_branch(zb, g, batch, seq) for g in range(B_BRANCHES)]
            body, token_args, const_args = _mix_ab_xattn_kernel, [oa] + rs, [permT, wo_a, wo_b]
        else:
            z = _inproj(x, lay["g_mix"], lay["w_in_c"])
            oc = _natten(z, lay["bias_c"], batch, seq)
            body, token_args, const_args = _mix_c_xattn_kernel, [oc], [lay["w_out_c"]]
        kv = _mem_kv(mem, lay["g_mem"], lay["wkv"])
        x = _mix_xattn(body, x, token_args, const_args, lay["g_xattn"], lay["wq"], kv, lay["wo"], seq)
        x = _swiglu(x, lay["g_ffn"], lay["wg"], lay["wu"], lay["wd"], g_final,
                    final_norm=(l == len(layers) - 1))
    return x.reshape(batch, seq, D_MODEL)


def kernel(x_prompt, x_sample, mem_prompt, mem_sample, g_mix, w_in_ab, g_qn, g_kn, w_out_ab, w_in_c,
           rpb_c, w_out_c, g_xattn, g_mem, wq_x, wkv_x, wo_x, g_ffn, w_gu, w_down, g_final):
    layers = _prepare_layers(dict(
        g_mix=g_mix, w_in_ab=w_in_ab, g_qn=g_qn, g_kn=g_kn, w_out_ab=w_out_ab, w_in_c=w_in_c,
        rpb_c=rpb_c, w_out_c=w_out_c, g_xattn=g_xattn, g_mem=g_mem, wq_x=wq_x, wkv_x=wkv_x,
        wo_x=wo_x, g_ffn=g_ffn, w_gu=w_gu, w_down=w_down))
    g_final = g_final.reshape(1, D_MODEL)
    return (_trunk(x_prompt, mem_prompt, layers, g_final), _trunk(x_sample, mem_sample, layers, g_final))
```

```python
import functools
import math

import numpy as np
import jax
import jax.numpy as jnp
from jax import lax
from jax.experimental import pallas as pl
from jax.experimental.pallas import tpu as pltpu

F32 = jnp.float32
BF16 = jnp.bfloat16

D_MODEL = 1024
GRID_W = 64
HEAD_DIM = 64
EPS = 1e-6
NEG = -1e30
LOG2E = math.log2(math.e)
TOKEN_TILE = 512

A_Q_HEADS = 8
A_KV_HEADS = 2
A_GROUP = A_Q_HEADS // A_KV_HEADS
ROPE_THETA = 10000.0
A_Q_W = A_Q_HEADS * HEAD_DIM
A_KV_W = A_KV_HEADS * HEAD_DIM
A_W = A_Q_W + 2 * A_KV_W
A_TK = 256
A_V_ROWS = HEAD_DIM + 16
A_UNROLL = 6
A_UNROLL_BOUNDED = 16
A_SHIFT_MARGIN = 60.0
A_BOUND_LIMIT = 90.0
A_BOUND_SLACK = 1.0 + 2.0 ** -6

B_PATTERNS = ((128, 1), (512, 4), (2048, 16))
B_BRANCHES = len(B_PATTERNS)
B_HEADS = 4
B_HALF = 64
B_W = B_HEADS * HEAD_DIM
B_SUB = 2 * B_HALF
B_KEYS = 4 * B_HALF
LSE_REP = 2 * HEAD_DIM // B_HEADS
B_R = B_W + 4 * HEAD_DIM

C_HEADS = 16
C_WIN_ROWS = 8
C_WIN_COLS = 16
C_ROWS_Q = 4
C_ROWS_K = 12
C_W = C_HEADS * HEAD_DIM
C_SHIFT_MARGIN = 40.0
C_BOUND_LIMIT = 120.0
C_BOUND_SLACK = 1.0 + 2.0 ** -6

X_HEADS = 4
X_HEAD_DIM = D_MODEL // X_HEADS
D_FF = 2816

VMEM_LIMIT = 56 * 1024 * 1024


def _cparams(n_axes):
    return pltpu.CompilerParams(dimension_semantics=("arbitrary",) * n_axes,
                                vmem_limit_bytes=VMEM_LIMIT)


def _rms(x, g):
    ms = jnp.mean(x * x, axis=-1, keepdims=True)
    return x * lax.rsqrt(ms + EPS) * g


def _dot(a, b):
    return jnp.dot(a, b, preferred_element_type=F32)


def _dot_nt(a, b):
    return lax.dot_general(a, b, (((1,), (1,)), ((), ())), preferred_element_type=F32)


def _const_spec(shape):
    zeros = (0,) * len(shape)
    return pl.BlockSpec(shape, lambda *_: zeros)


def _inproj_ab_kernel(x_ref, g_ref, waT_ref, wb_ref, perm_ref, gq_ref, gk_ref, cos_ref, sin_ref,
                      qT_ref, k_ref, vT_ref, stats_ref, zb_ref):
    hb = _rms(x_ref[...], g_ref[...]).astype(BF16)
    zT = _dot_nt(waT_ref[...], hb)
    cos = cos_ref[...]
    sin = sin_ref[...]
    half = HEAD_DIM // 2

    def norm_rope(zh, gcol):
        ms = jnp.mean(zh * zh, axis=0, keepdims=True)
        y = zh * lax.rsqrt(ms + EPS) * gcol
        yr, yi = y[:half], y[half:]
        return jnp.concatenate([yr * cos - yi * sin, yr * sin + yi * cos], axis=0)

    def l2(v):
        return jnp.sqrt(jnp.sum(v * v, axis=0, keepdims=True))

    stats_ref[...] = jnp.zeros(stats_ref.shape, F32)
    for h in range(A_Q_HEADS):
        sl = slice(h * HEAD_DIM, (h + 1) * HEAD_DIM)
        q = norm_rope(zT[sl], gq_ref[...])
        qT_ref[sl, :] = q.astype(BF16)
        stats_ref[h // A_GROUP, h % A_GROUP:h % A_GROUP + 1, :] = l2(q)
    ks = [norm_rope(zT[A_Q_W + h * HEAD_DIM:A_Q_W + (h + 1) * HEAD_DIM], gk_ref[...])
          for h in range(A_KV_HEADS)]
    for h in range(A_KV_HEADS):
        stats_ref[h, A_GROUP:A_GROUP + 1, :] = l2(ks[h])
    kT = jnp.concatenate(ks, axis=0)
    k_nat = kT.T
    for h in range(A_KV_HEADS):
        k_ref[h] = k_nat[:, h * HEAD_DIM:(h + 1) * HEAD_DIM].astype(BF16)
    for h in range(A_KV_HEADS):
        for c in range(vT_ref.shape[1]):
            vT_ref[h, c] = zT[A_Q_W + A_KV_W + h * HEAD_DIM:A_Q_W + A_KV_W + (h + 1) * HEAD_DIM,
                              c * A_TK:(c + 1) * A_TK].astype(BF16)
    zb_ref[0] = _dot(hb, wb_ref[0]).astype(BF16)
    for g in range(1, B_BRANCHES):
        zb_ref[g] = _dot(perm_ref[g - 1], _dot(hb, wb_ref[g]).astype(BF16)).astype(BF16)


def _inproj_ab(x, g_mix, waT, wb, perm, gq, gk, cosT, sinT, seq):
    t = x.shape[0]
    tm = TOKEN_TILE
    n_seq_tiles = seq // tm
    return pl.pallas_call(
        _inproj_ab_kernel,
        grid=(t // tm,),
        in_specs=[
            pl.BlockSpec((tm, D_MODEL), lambda i: (i, 0)),
            _const_spec((1, D_MODEL)),
            _const_spec((A_W, D_MODEL)),
            _const_spec((B_BRANCHES, D_MODEL, 3 * B_W)),
            _const_spec((B_BRANCHES - 1, tm, tm)),
            _const_spec((HEAD_DIM, 1)),
            _const_spec((HEAD_DIM, 1)),
            pl.BlockSpec((HEAD_DIM // 2, tm), lambda i: (0, i % n_seq_tiles)),
            pl.BlockSpec((HEAD_DIM // 2, tm), lambda i: (0, i % n_seq_tiles)),
        ],
        out_specs=[
            pl.BlockSpec((A_Q_W, tm), lambda i: (0, i)),
            pl.BlockSpec((A_KV_HEADS, tm, HEAD_DIM), lambda i: (0, i, 0)),
            pl.BlockSpec((A_KV_HEADS, tm // A_TK, HEAD_DIM, A_TK), lambda i: (0, i, 0, 0)),
            pl.BlockSpec((A_KV_HEADS, 8, tm), lambda i: (0, 0, i)),
            pl.BlockSpec((B_BRANCHES, tm, 3 * B_W), lambda i: (0, i, 0)),
        ],
        out_shape=[
            jax.ShapeDtypeStruct((A_Q_W, t), BF16),
            jax.ShapeDtypeStruct((A_KV_HEADS, t, HEAD_DIM), BF16),
            jax.ShapeDtypeStruct((A_KV_HEADS, t // A_TK, HEAD_DIM, A_TK), BF16),
            jax.ShapeDtypeStruct((A_KV_HEADS, 8, t), F32),
            jax.ShapeDtypeStruct((B_BRANCHES, t, 3 * B_W), BF16),
        ],
        compiler_params=_cparams(1),
        name="inproj_ab",
    )(x, g_mix, waT, wb, perm, gq, gk, cosT, sinT)


def _gqa_kernel(qT_ref, k_ref, vT_ref, o_ref, s_scr, p_scr, cm_scr, alpha_scr, m_scr, acc_scr):
    tq = qT_ref.shape[1]
    tk = A_TK
    n_chunks = vT_ref.shape[0]
    ones = jnp.ones((A_V_ROWS - HEAD_DIM, tk), BF16)

    def scores(chunk, slot):
        k = k_ref[pl.ds(pl.multiple_of(chunk * tk, tk), tk), :]
        for h in range(A_GROUP):
            s = _dot(k, qT_ref[h * HEAD_DIM:(h + 1) * HEAD_DIM, :])
            s_scr[slot, h] = s
            cm_scr[slot, h] = jnp.max(s.reshape(tk // 8, 8, tq), axis=0)

    def probs(slot):
        for h in range(A_GROUP):
            m_old = m_scr[h]
            m_new = jnp.maximum(m_old, jnp.max(cm_scr[slot, h], axis=0, keepdims=True))
            alpha_scr[slot, h] = jnp.exp2(m_old - m_new)
            m_scr[h] = m_new
            p_scr[slot, h] = jnp.exp2(s_scr[slot, h] - m_new).astype(BF16)

    def accumulate(chunk, slot):
        vT_aug = jnp.concatenate([vT_ref[chunk], ones], axis=0)
        for h in range(A_GROUP):
            acc_scr[h] = alpha_scr[slot, h] * acc_scr[h] + _dot(vT_aug, p_scr[slot, h])

    def steady(c, count):
        for u in range(count):
            accumulate(c + u, u % 2)
            probs((u + 1) % 2)
            scores(c + u + 2, u % 2)

    m_scr[...] = jnp.full(m_scr.shape, NEG, F32)
    acc_scr[...] = jnp.zeros(acc_scr.shape, F32)
    scores(0, 0)
    scores(1, 1)
    probs(0)
    n_loop = (n_chunks - 2) // A_UNROLL

    def body(i, carry):
        steady(i * A_UNROLL, A_UNROLL)
        return carry

    if n_loop > 1:
        lax.fori_loop(0, n_loop, body, 0)
    else:
        steady(0, n_loop * A_UNROLL)
    steady(n_loop * A_UNROLL, (n_chunks - 2) % A_UNROLL)
    accumulate(n_chunks - 2, 0)
    probs(1)
    accumulate(n_chunks - 1, 1)
    oT = jnp.concatenate(
        [acc_scr[h, :HEAD_DIM, :] / acc_scr[h, HEAD_DIM:HEAD_DIM + 1, :] for h in range(A_GROUP)], axis=0)
    o_ref[...] = oT.T.astype(BF16)


def _gqa_bounded_kernel(qT_ref, k_ref, vT_ref, shift_ref, o_ref, p_scr, acc_scr):
    tq = qT_ref.shape[1]
    tk = A_TK
    n_chunks = vT_ref.shape[0]
    ones = jnp.ones((A_V_ROWS - HEAD_DIM, tk), BF16)

    def probs(chunk, slot):
        k = k_ref[pl.ds(pl.multiple_of(chunk * tk, tk), tk), :]
        for h in range(A_GROUP):
            s = _dot(k, qT_ref[h * HEAD_DIM:(h + 1) * HEAD_DIM, :])
            p_scr[slot, h] = jnp.exp2(s - shift_ref[h:h + 1, :]).astype(BF16)

    def accumulate(chunk, slot):
        vT_aug = jnp.concatenate([vT_ref[chunk], ones], axis=0)
        for h in range(A_GROUP):
            acc_scr[h] += _dot(vT_aug, p_scr[slot, h])

    def steady(c, count):
        for u in range(count):
            accumulate(c + u, u % 2)
            probs(c + u + 2, u % 2)

    acc_scr[...] = jnp.zeros(acc_scr.shape, F32)
    probs(0, 0)
    probs(1, 1)
    n_loop = (n_chunks - 2) // A_UNROLL_BOUNDED

    def body(i, carry):
        steady(i * A_UNROLL_BOUNDED, A_UNROLL_BOUNDED)
        return carry

    if n_loop > 1:
        lax.fori_loop(0, n_loop, body, 0)
    else:
        steady(0, n_loop * A_UNROLL_BOUNDED)
    steady(n_loop * A_UNROLL_BOUNDED, (n_chunks - 2) % A_UNROLL_BOUNDED)
    accumulate(n_chunks - 2, 0)
    accumulate(n_chunks - 1, 1)
    oT = jnp.concatenate(
        [acc_scr[h, :HEAD_DIM, :] / acc_scr[h, HEAD_DIM:HEAD_DIM + 1, :] for h in range(A_GROUP)], axis=0)
    o_ref[...] = oT.T.astype(BF16)


def _gqa_attention(qT, k, vT, stats, batch, seq, tq=256):
    t = batch * seq
    nq = seq // tq
    k4 = k.reshape(A_KV_HEADS, batch, seq, HEAD_DIM)
    v5 = vT.reshape(A_KV_HEADS, batch, seq // A_TK, HEAD_DIM, A_TK)
    kmax = jnp.max(stats[:, A_GROUP].reshape(A_KV_HEADS, batch, seq), axis=-1)
    bound = (stats.reshape(A_KV_HEADS, 8, batch, seq) * kmax[:, None, :, None]
             * A_BOUND_SLACK).reshape(A_KV_HEADS, 8, t)
    in_specs = [
        pl.BlockSpec((A_GROUP * HEAD_DIM, tq), lambda b, g, i: (g, b * nq + i)),
        pl.BlockSpec((None, None, seq, HEAD_DIM), lambda b, g, i: (g, b, 0, 0)),
        pl.BlockSpec((None, None, seq // A_TK, HEAD_DIM, A_TK), lambda b, g, i: (g, b, 0, 0, 0)),
    ]
    common = dict(
        grid=(batch, A_KV_HEADS, nq),
        out_specs=pl.BlockSpec((tq, A_GROUP * HEAD_DIM), lambda b, g, i: (b * nq + i, g)),
        out_shape=jax.ShapeDtypeStruct((t, A_Q_W), BF16),
        compiler_params=_cparams(3),
    )

    def bounded():
        return pl.pallas_call(
            _gqa_bounded_kernel,
            in_specs=in_specs + [pl.BlockSpec((None, 8, tq), lambda b, g, i: (g, 0, b * nq + i))],
            scratch_shapes=[pltpu.VMEM((2, A_GROUP, A_TK, tq), BF16),
                            pltpu.VMEM((A_GROUP, A_V_ROWS, tq), F32)],
            name="gqa_bounded", **common,
        )(qT, k4, v5, bound - A_SHIFT_MARGIN)

    def online():
        return pl.pallas_call(
            _gqa_kernel,
            in_specs=in_specs,
            scratch_shapes=[pltpu.VMEM((2, A_GROUP, A_TK, tq), F32),
                            pltpu.VMEM((2, A_GROUP, A_TK, tq), BF16),
                            pltpu.VMEM((2, A_GROUP, 8, tq), F32),
                            pltpu.VMEM((2, A_GROUP, 1, tq), F32),
                            pltpu.VMEM((A_GROUP, 1, tq), F32),
                            pltpu.VMEM((A_GROUP, A_V_ROWS, tq), F32)],
            name="gqa_attention", **common,
        )(qT, k4, v5)

    safe = jnp.max(bound[:, :A_GROUP]) <= A_BOUND_LIMIT
    return lax.cond(safe, bounded, online)


def _tile_perm(dil):
    c = TOKEN_TILE // dil
    dst = np.arange(TOKEN_TILE)
    src = (dst % c) * dil + dst // c
    p = np.zeros((TOKEN_TILE, TOKEN_TILE), np.float32)
    p[dst, src] = 1.0
    return p


def _seq_rows(ref, start, n, cols):
    c = ref.shape[1]
    if n <= c:
        return ref[start // c, start % c:start % c + n, cols]
    blocks = ref[start // c:(start + n) // c, :, cols]
    return blocks.reshape(n, blocks.shape[-1])


def _store_seq_rows(ref, start, cols, val):
    c = ref.shape[1]
    n = val.shape[0]
    if n <= c:
        ref[start // c, start % c:start % c + n, cols] = val
    else:
        ref[start // c:(start + n) // c, :, cols] = val.reshape(n // c, c, val.shape[-1])


def _dilated_kernel(main_ref, prev_ref, next_ref, bias_ref, r_ref, kv_scr, *, seq_len, cq):
    n = pl.program_id(2)
    kv_cols = slice(B_W, 3 * B_W)
    pair_w = 2 * HEAD_DIM
    lane = lax.broadcasted_iota(jnp.int32, (1, pair_w), 1)
    lo = lane < HEAD_DIM
    key_col = lax.broadcasted_iota(jnp.int32, (1, B_KEYS), 1)

    for rr in range(main_ref.shape[1]):
        main, res = main_ref.at[:, rr], r_ref.at[:, rr]
        kv_scr[0:B_HALF, :] = prev_ref[..., rr, :, kv_cols].reshape(B_HALF, 2 * B_W)
        for j in range(cq // B_SUB):
            kv_scr[B_HALF + j * B_SUB:B_HALF + (j + 1) * B_SUB, :] = _seq_rows(main, j * B_SUB, B_SUB, kv_cols)
        kv_scr[B_HALF + cq:, :] = next_ref[..., rr, :, kv_cols].reshape(B_HALF, 2 * B_W)

        for j in range(cq // B_SUB):
            key_pos = key_col + (n * cq + j * B_SUB - B_HALF)
            edge = jnp.where((key_pos >= 0) & (key_pos < seq_len), 0.0, NEG).astype(F32)
            krows = slice(j * B_SUB, j * B_SUB + B_KEYS)
            lses = []
            for pr in range(B_HEADS // 2):
                cols = slice(pr * pair_w, (pr + 1) * pair_w)
                q2 = _seq_rows(main, j * B_SUB, B_SUB, cols)
                k2 = kv_scr[krows, cols]
                v2 = kv_scr[krows, B_W + pr * pair_w:B_W + (pr + 1) * pair_w]
                o_pair = []
                for hh in range(2):
                    qm = jnp.where(lo if hh == 0 else ~lo, q2, jnp.zeros_like(q2))
                    s = _dot_nt(qm, k2) + bias_ref[pr * 2 + hh] + edge
                    m = jnp.max(s, axis=-1, keepdims=True)
                    p = jnp.exp(s - m)
                    l = jnp.sum(p, axis=-1, keepdims=True)
                    o_pair.append(_dot(p.astype(BF16), v2) / l)
                    lses.append(m + jnp.log(l))
                _store_seq_rows(res, j * B_SUB, cols, jnp.where(lo, o_pair[0], o_pair[1]).astype(BF16))
            tile = lses[B_HEADS - 1]
            for head in range(B_HEADS - 2, -1, -1):
                tile = jnp.where(lane < (head + 1) * LSE_REP, lses[head], tile)
            hi = tile.astype(BF16)
            _store_seq_rows(res, j * B_SUB, slice(B_W, B_W + pair_w), hi)
            _store_seq_rows(res, j * B_SUB, slice(B_W + pair_w, B_R),
                            (tile - hi.astype(F32)).astype(BF16))


def _dilated_bias(branch):
    _, dil = B_PATTERNS[branch]
    slopes = np.exp2(-8.0 * np.arange(1, B_BRANCHES * B_HEADS + 1, dtype=np.float64)
                     / (B_BRANCHES * B_HEADS)).reshape(B_BRANCHES, B_HEADS)[branch]
    a = np.arange(B_SUB)[:, None]
    c = np.arange(B_KEYS)[None, :]
    rel = (c - B_HALF) - a
    bias = -slopes[:, None, None] * (np.abs(rel) * dil).astype(np.float64)[None]
    bias = np.where((np.abs(rel) <= B_HALF)[None], bias, NEG)
    return jnp.asarray(bias, F32)


def _dilated_branch(zb, branch, batch, seq):
    _, dil = B_PATTERNS[branch]
    t = batch * seq
    seq_len = seq // dil
    c = TOKEN_TILE // dil
    n_tiles = seq // TOKEN_TILE
    cq = min(512, seq_len)
    nchunk = seq_len // cq
    n_halo = seq_len // B_HALF
    per = cq // B_HALF
    rb = min(dil, max(1, 512 // cq))
    zv = zb.reshape(B_BRANCHES, batch, n_tiles, dil, c, 3 * B_W)

    def halo_spec(pos):
        if c >= B_HALF:
            sub = c // B_HALF
            z6 = zb.reshape(B_BRANCHES, batch, n_tiles, dil, sub, B_HALF, 3 * B_W)
            return z6, pl.BlockSpec((None, None, None, rb, None, B_HALF, 3 * B_W),
                                    lambda b, r, n: (branch, b, pos(n) // sub, r, pos(n) % sub, 0, 0))
        return zv, pl.BlockSpec((None, None, B_HALF // c, rb, c, 3 * B_W),
                                lambda b, r, n: (branch, b, pos(n), r, 0, 0))

    prev_arr, prev_spec = halo_spec(lambda n: jnp.maximum(n * per - 1, 0))
    next_arr, next_spec = halo_spec(lambda n: jnp.minimum((n + 1) * per, n_halo - 1))
    res = pl.pallas_call(
        functools.partial(_dilated_kernel, seq_len=seq_len, cq=cq),
        grid=(batch, dil // rb, nchunk),
        in_specs=[pl.BlockSpec((None, None, cq // c, rb, c, 3 * B_W), lambda b, r, n: (branch, b, n, r, 0, 0)),
                  prev_spec, next_spec, _const_spec((B_HEADS, B_SUB, B_KEYS))],
        out_specs=pl.BlockSpec((None, cq // c, rb, c, B_R), lambda b, r, n: (b, n, r, 0, 0)),
        out_shape=jax.ShapeDtypeStruct((batch, n_tiles, dil, c, B_R), BF16),
        scratch_shapes=[pltpu.VMEM((cq + 2 * B_HALF, 2 * B_W), BF16)],
        compiler_params=_cparams(3),
        name=f"dilated_branch{branch}",
    )(zv, prev_arr, next_arr, _dilated_bias(branch))
    return res.reshape(t, B_R)


def _merge_branches(rs, permT_ref):
    pair_w = 2 * HEAD_DIM
    lo = lax.broadcasted_iota(jnp.int32, (1, pair_w), 1) < HEAD_DIM
    tiles = [rs[0].astype(F32)] + [_dot(permT_ref[g - 1], rs[g]) for g in range(1, B_BRANCHES)]
    lses = [tl[:, B_W:B_W + pair_w] + tl[:, B_W + pair_w:] for tl in tiles]
    mx = functools.reduce(jnp.maximum, lses)
    es = [jnp.exp(l - mx) for l in lses]
    inv = 1.0 / functools.reduce(lambda u, v: u + v, es)
    pairs = []
    for pr in range(B_HEADS // 2):
        acc = None
        for tl, e in zip(tiles, es):
            w = e * inv
            w2 = jnp.where(lo, w[:, 2 * pr * LSE_REP:2 * pr * LSE_REP + 1],
                           w[:, (2 * pr + 1) * LSE_REP:(2 * pr + 1) * LSE_REP + 1])
            term = w2 * tl[:, pr * pair_w:(pr + 1) * pair_w]
            acc = term if acc is None else acc + term
        pairs.append(acc)
    return jnp.concatenate(pairs, axis=-1).astype(BF16)


def _inproj_kernel(x_ref, g_ref, w_ref, z_ref, stats_ref):
    hb = _rms(x_ref[...], g_ref[...]).astype(BF16)
    z = _dot(hb, w_ref[...])
    z_ref[...] = z.astype(BF16)
    pair_w = 2 * HEAD_DIM
    lane = lax.broadcasted_iota(jnp.int32, (1, pair_w), 1)
    lo = lane < HEAD_DIM
    out = jnp.zeros((1, pair_w), F32)
    for b in range(2 * C_W // pair_w):
        sq = z[:, b * pair_w:(b + 1) * pair_w]
        sq = sq * sq
        tot = jnp.sum(sq, axis=-1, keepdims=True)
        first = jnp.sum(jnp.where(lo, sq, 0.0), axis=-1, keepdims=True)
        m_first = jnp.max(first, axis=0, keepdims=True)
        m_second = jnp.max(tot - first, axis=0, keepdims=True)
        out = jnp.where(lane == 2 * b, m_first, jnp.where(lane == 2 * b + 1, m_second, out))
    stats_ref[...] = jnp.broadcast_to(out, stats_ref.shape)


def _inproj(x, g, w):
    t = x.shape[0]
    tm = TOKEN_TILE
    n_out = w.shape[1]
    return pl.pallas_call(
        _inproj_kernel,
        grid=(t // tm,),
        in_specs=[pl.BlockSpec((tm, D_MODEL), lambda i: (i, 0)),
                  _const_spec((1, D_MODEL)),
                  _const_spec((D_MODEL, n_out))],
        out_specs=[pl.BlockSpec((tm, n_out), lambda i: (i, 0)),
                   pl.BlockSpec((None, 8, 2 * HEAD_DIM), lambda i: (i, 0, 0))],
        out_shape=[jax.ShapeDtypeStruct((t, n_out), BF16),
                   jax.ShapeDtypeStruct((t // tm, 8, 2 * HEAD_DIM), F32)],
        compiler_params=_cparams(1),
        name="inproj_c",
    )(x, g, w)


def _natten_kernel(q_ref, k0_ref, k1_ref, k2_ref, v0_ref, v1_ref, v2_ref, bias_ref, o_ref):
    k_refs = (k0_ref, k1_ref, k2_ref)
    v_refs = (v0_ref, v1_ref, v2_ref)
    lane = lax.broadcasted_iota(jnp.int32, (1, 2 * HEAD_DIM), 1)
    lo = lane < HEAD_DIM
    kb = k0_ref.shape[0]
    for pr in range(C_HEADS // 2):
        cols = slice(pr * 2 * HEAD_DIM, (pr + 1) * 2 * HEAD_DIM)
        q2 = q_ref[:, cols]
        ks = [r[:, cols] for r in k_refs]
        vs = [r[:, cols] for r in v_refs]
        o_pair = []
        for hh in range(2):
            qm = jnp.where(lo if hh == 0 else ~lo, q2, jnp.zeros_like(q2))
            s = jnp.concatenate([_dot_nt(qm, kk) for kk in ks], axis=-1) + bias_ref[pr * 2 + hh]
            m = jnp.max(s, axis=-1, keepdims=True)
            p = jnp.exp(s - m)
            l = jnp.sum(p, axis=-1, keepdims=True)
            pb = p.astype(BF16)
            o = _dot(pb[:, :kb], vs[0])
            for i in range(1, len(vs)):
                o = o + _dot(pb[:, i * kb:(i + 1) * kb], vs[i])
            o_pair.append(o / l)
        o_ref[:, cols] = jnp.where(lo, o_pair[0], o_pair[1]).astype(BF16)


def _natten_bounded_kernel(q_ref, k0_ref, k1_ref, k2_ref, v0_ref, v1_ref, v2_ref, bias_ref, o_ref):
    k_refs = (k0_ref, k1_ref, k2_ref)
    v_refs = (v0_ref, v1_ref, v2_ref)
    pair_w = 2 * HEAD_DIM
    lo = lax.broadcasted_iota(jnp.int32, (1, pair_w), 1) < HEAD_DIM
    kb = k0_ref.shape[0]
    one = jnp.ones((), BF16)
    for pr in range(C_HEADS // 2):
        cols = slice(pr * pair_w, (pr + 1) * pair_w)
        q2 = q_ref[:, cols]
        ks = [r[:, cols] for r in k_refs]
        vs = [r[:, cols] for r in v_refs]
        o_pair = []
        for hh in range(2):
            own = lo if hh == 0 else ~lo
            qm = jnp.where(own, q2, jnp.zeros_like(q2))
            p = jnp.exp(jnp.concatenate([_dot_nt(qm, kk) for kk in ks], axis=-1)
                        + bias_ref[pr * 2 + hh]).astype(BF16)
            r = None
            for i in range(len(vs)):
                t = _dot(p[:, i * kb:(i + 1) * kb], jnp.where(own, vs[i], one))
                r = t if r is None else r + t
            l = r[:, HEAD_DIM:HEAD_DIM + 1] if hh == 0 else r[:, 0:1]
            o_pair.append(r / l)
        o_ref[:, cols] = jnp.where(lo, o_pair[0], o_pair[1]).astype(BF16)


def _natten_bias(rpb, shift):
    c = np.arange(GRID_W)[:, None]
    kc = np.arange(GRID_W)[None, :]
    cstart = np.clip(c - C_WIN_COLS // 2, 0, GRID_W - C_WIN_COLS)
    col_valid = (kc >= cstart) & (kc < cstart + C_WIN_COLS)
    padded = jnp.pad(rpb.astype(F32) - shift[:, None, None], ((0, 0), (0, 0), (GRID_W, GRID_W)))
    base = GRID_W + C_WIN_COLS - 1
    blocks = jnp.stack([padded[:, :, base - q:base - q + GRID_W] for q in range(GRID_W)], axis=2)
    blocks = jnp.where(col_valid[None, None], blocks, NEG)
    masked = jnp.full((rpb.shape[0], GRID_W, GRID_W), NEG, F32)
    tables = []
    for off, first in ((0, lambda dr: 0), (-4, lambda dr: dr), (-8, lambda dr: 4)):
        rows = []
        for dr in range(C_ROWS_Q):
            rows.append(jnp.concatenate(
                [blocks[:, off + kri - dr + C_WIN_ROWS - 1]
                 if first(dr) <= kri < first(dr) + C_WIN_ROWS else masked
                 for kri in range(C_ROWS_K)], axis=-1))
        tables.append(jnp.concatenate(rows, axis=1))
    return jnp.stack(tables)


def _natten(z, stats, rpb, batch, seq):
    t = batch * seq
    qb = C_ROWS_Q * GRID_W
    nrb = seq // qb
    nkb = C_ROWS_K // C_ROWS_Q

    def kspec(i, col):
        return pl.BlockSpec(
            (qb, C_W), lambda rb, b: (b * nrb + jnp.clip(rb - 1, 0, nrb - nkb) + i, col))

    def variant(rb):
        return jnp.where(rb == 0, 0, jnp.where(rb == nrb - 1, 2, 1))

    def call(body, bias):
        return pl.pallas_call(
            body,
            grid=(nrb, batch),
            in_specs=[pl.BlockSpec((qb, C_W), lambda rb, b: (b * nrb + rb, 0))]
            + [kspec(i, 1) for i in range(nkb)]
            + [kspec(i, 2) for i in range(nkb)]
            + [pl.BlockSpec((None, C_HEADS, qb, C_ROWS_K * GRID_W), lambda rb, b: (variant(rb), 0, 0, 0))],
            out_specs=pl.BlockSpec((qb, C_W), lambda rb, b: (b * nrb + rb, 0)),
            out_shape=jax.ShapeDtypeStruct((t, C_W), BF16),
            compiler_params=_cparams(2),
            name="natten",
        )(z, z, z, z, z, z, z, bias)

    norms = jnp.sqrt(jnp.max(stats[:, 0, :2 * C_HEADS], axis=0)) * C_BOUND_SLACK
    qk_bound = norms[:C_HEADS] * norms[C_HEADS:]
    rpb_max = jnp.max(rpb, axis=(1, 2))
    rpb_range = rpb_max - jnp.min(rpb, axis=(1, 2))
    safe = jnp.max(2.0 * qk_bound + rpb_range) <= C_BOUND_LIMIT
    return lax.cond(
        safe,
        lambda: call(_natten_bounded_kernel, _natten_bias(rpb, qk_bound + rpb_max - C_SHIFT_MARGIN)),
        lambda: call(_natten_kernel, _natten_bias(rpb, jnp.zeros_like(rpb_max))))


def _mem_kv_kernel(mem_ref, g_ref, w_ref, kv_ref):
    mb = _rms(mem_ref[...], g_ref[...]).astype(BF16)
    kv_ref[...] = _dot(mb, w_ref[...]).astype(BF16)


def _mem_kv(mem, g, w):
    batch, n_mem, _ = mem.shape
    return pl.pallas_call(
        _mem_kv_kernel,
        grid=(batch,),
        in_specs=[pl.BlockSpec((None, n_mem, D_MODEL), lambda b: (b, 0, 0)),
                  _const_spec((1, D_MODEL)),
                  _const_spec((D_MODEL, 2 * D_MODEL))],
        out_specs=pl.BlockSpec((None, n_mem, 2 * D_MODEL), lambda b: (b, 0, 0)),
        out_shape=jax.ShapeDtypeStruct((batch, n_mem, 2 * D_MODEL), BF16),
        compiler_params=_cparams(1),
        name="mem_kv",
    )(mem, g, w)


def _xattn_tail(x, g_ref, wq_ref, kv_ref, wo_ref, y_ref):
    q = _dot(_rms(x, g_ref[...]).astype(BF16), wq_ref[...]).astype(BF16)
    outs = []
    for h in range(X_HEADS):
        cols = slice(h * X_HEAD_DIM, (h + 1) * X_HEAD_DIM)
        s = _dot_nt(q[:, cols], kv_ref[:, cols])
        m = jnp.max(s, axis=-1, keepdims=True)
        p = jnp.exp(s - m)
        l = jnp.sum(p, axis=-1, keepdims=True)
        o = _dot(p.astype(BF16), kv_ref[:, D_MODEL + h * X_HEAD_DIM:D_MODEL + (h + 1) * X_HEAD_DIM])
        outs.append((o / l).astype(BF16))
    y_ref[...] = x + _dot(jnp.concatenate(outs, axis=-1), wo_ref[...])


def _mix_ab_xattn_kernel(x_ref, oa_ref, r0_ref, r1_ref, r2_ref, permT_ref, woa_ref, wob_ref,
                         g_ref, wq_ref, kv_ref, wo_ref, y_ref):
    ob = _merge_branches([r0_ref[...], r1_ref[...], r2_ref[...]], permT_ref)
    x = x_ref[...] + _dot(oa_ref[...], woa_ref[...]) + _dot(ob, wob_ref[...])
    _xattn_tail(x, g_ref, wq_ref, kv_ref, wo_ref, y_ref)


def _mix_c_xattn_kernel(x_ref, oc_ref, woc_ref, g_ref, wq_ref, kv_ref, wo_ref, y_ref):
    x = x_ref[...] + _dot(oc_ref[...], woc_ref[...])
    _xattn_tail(x, g_ref, wq_ref, kv_ref, wo_ref, y_ref)


def _mix_xattn(body, x, token_args, const_args, g, wq, kv, wo, seq):
    t = x.shape[0]
    tm = TOKEN_TILE
    n_seq_tiles = seq // tm
    n_mem = kv.shape[1]
    token_spec = lambda a: pl.BlockSpec((tm, a.shape[1]), lambda i: (i, 0))
    in_specs = ([token_spec(x)] + [token_spec(a) for a in token_args]
                + [_const_spec(a.shape) for a in const_args]
                + [_const_spec((1, D_MODEL)),
                   _const_spec((D_MODEL, D_MODEL)),
                   pl.BlockSpec((None, n_mem, 2 * D_MODEL), lambda i: (i // n_seq_tiles, 0, 0)),
                   _const_spec((D_MODEL, D_MODEL))])
    return pl.pallas_call(
        body,
        grid=(t // tm,),
        in_specs=in_specs,
        out_specs=pl.BlockSpec((tm, D_MODEL), lambda i: (i, 0)),
        out_shape=jax.ShapeDtypeStruct((t, D_MODEL), F32),
        compiler_params=_cparams(1),
        name="mix_xattn",
    )(x, *token_args, *const_args, g, wq, kv, wo)


def _swiglu_kernel(x_ref, g_ref, wg_ref, wu_ref, wd_ref, gf_ref, y_ref, *, final_norm):
    x = x_ref[...]
    hb = _rms(x, g_ref[...]).astype(BF16)
    gate = _dot(hb, wg_ref[...])
    up = _dot(hb, wu_ref[...])
    act = (gate / (1.0 + jnp.exp(-gate)) * up).astype(BF16)
    y = x + _dot(act, wd_ref[...])
    if final_norm:
        y = _rms(y, gf_ref[...])
    y_ref[...] = y


def _swiglu(x, g, wg, wu, wd, g_final, final_norm, tm=256):
    t = x.shape[0]
    return pl.pallas_call(
        functools.partial(_swiglu_kernel, final_norm=final_norm),
        grid=(t // tm,),
        in_specs=[pl.BlockSpec((tm, D_MODEL), lambda i: (i, 0)),
                  _const_spec((1, D_MODEL)),
                  _const_spec((D_MODEL, D_FF)),
                  _const_spec((D_MODEL, D_FF)),
                  _const_spec((D_FF, D_MODEL)),
                  _const_spec((1, D_MODEL))],
        out_specs=pl.BlockSpec((tm, D_MODEL), lambda i: (i, 0)),
        out_shape=jax.ShapeDtypeStruct((t, D_MODEL), F32),
        compiler_params=_cparams(1),
        name="swiglu",
    )(x, g, wg, wu, wd, g_final)


def _rope_tables(seq):
    tok = jnp.arange(seq, dtype=jnp.int32)
    row = (tok // GRID_W).astype(F32)
    col = (tok % GRID_W).astype(F32)
    axis_dim = HEAD_DIM // 2
    inv_freq = ROPE_THETA ** (-jnp.arange(0, axis_dim, 2, dtype=F32) / axis_dim)
    ang = jnp.concatenate([row[:, None] * inv_freq, col[:, None] * inv_freq], axis=-1)
    return jnp.cos(ang).T, jnp.sin(ang).T


def _prepare_ab(w_in, g_qn, g_kn, w_out):
    perm = np.concatenate([np.arange(0, HEAD_DIM, 2), np.arange(1, HEAD_DIM, 2)])
    cols = np.concatenate(
        [h * HEAD_DIM + perm for h in range(A_Q_HEADS + A_KV_HEADS)]
        + [np.arange(A_Q_W + A_KV_W, A_W)])
    waT = w_in[:, cols].T.astype(BF16)
    zbw = w_in[:, A_W:].reshape(D_MODEL, 3, B_BRANCHES, B_W)
    qscale = HEAD_DIM ** -0.5
    wb = jnp.stack([jnp.concatenate([zbw[:, 0, g] * qscale, zbw[:, 1, g], zbw[:, 2, g]], axis=1)
                    for g in range(B_BRANCHES)]).astype(BF16)
    gq = (g_qn[perm] * (qscale * LOG2E)).reshape(HEAD_DIM, 1).astype(F32)
    gk = g_kn[perm].reshape(HEAD_DIM, 1).astype(F32)
    perms = np.stack([_tile_perm(dil) for _, dil in B_PATTERNS[1:]])
    tile_perm = jnp.asarray(perms, BF16)
    tile_perm_t = jnp.asarray(perms.transpose(0, 2, 1), BF16)
    return waT, wb, tile_perm, tile_perm_t, gq, gk, w_out[:A_Q_W].astype(BF16), w_out[A_Q_W:].astype(BF16)


def _prepare_layers(p):
    depth = p["g_mix"].shape[0]
    row = lambda g: g.reshape(1, D_MODEL)
    layers = []
    for l in range(depth):
        lay = dict(g_mix=row(p["g_mix"][l]), g_xattn=row(p["g_xattn"][l]), g_mem=row(p["g_mem"][l]),
                   g_ffn=row(p["g_ffn"][l]),
                   wq=(p["wq_x"][l] * X_HEAD_DIM ** -0.5).astype(BF16), wkv=p["wkv_x"][l].astype(BF16),
                   wo=p["wo_x"][l].astype(BF16), wg=p["w_gu"][l][:, :D_FF].astype(BF16),
                   wu=p["w_gu"][l][:, D_FF:].astype(BF16), wd=p["w_down"][l].astype(BF16))
        i = l // 2
        if l % 2 == 0:
            lay["ab"] = _prepare_ab(p["w_in_ab"][i], p["g_qn"][i], p["g_kn"][i], p["w_out_ab"][i])
        else:
            w_in = p["w_in_c"][i]
            lay["w_in_c"] = jnp.concatenate(
                [w_in[:, :C_W] * HEAD_DIM ** -0.5, w_in[:, C_W:]], axis=1).astype(BF16)
            lay["rpb_c"] = p["rpb_c"][i]
            lay["w_out_c"] = p["w_out_c"][i].astype(BF16)
        layers.append(lay)
    return layers


def _trunk(x, mem, layers, g_final):
    batch, seq, _ = x.shape
    x = x.reshape(batch * seq, D_MODEL)
    for l, lay in enumerate(layers):
        if l % 2 == 0:
            waT, wb, perm, permT, gq, gk, wo_a, wo_b = lay["ab"]
            cosT, sinT = _rope_tables(seq)
            qT, k, vT, stats, zb = _inproj_ab(x, lay["g_mix"], waT, wb, perm, gq, gk, cosT, sinT, seq)
            oa = _gqa_attention(qT, k, vT, stats, batch, seq)
            rs = [_dilated_branch(zb, g, batch, seq) for g in range(B_BRANCHES)]
            body, token_args, const_args = _mix_ab_xattn_kernel, [oa] + rs, [permT, wo_a, wo_b]
        else:
            z, stats = _inproj(x, lay["g_mix"], lay["w_in_c"])
            oc = _natten(z, stats, lay["rpb_c"], batch, seq)
            body, token_args, const_args = _mix_c_xattn_kernel, [oc], [lay["w_out_c"]]
        kv = _mem_kv(mem, lay["g_mem"], lay["wkv"])
        x = _mix_xattn(body, x, token_args, const_args, lay["g_xattn"], lay["wq"], kv, lay["wo"], seq)
        x = _swiglu(x, lay["g_ffn"], lay["wg"], lay["wu"], lay["wd"], g_final,
                    final_norm=(l == len(layers) - 1))
    return x.reshape(batch, seq, D_MODEL)


def kernel(x_prompt, x_sample, mem_prompt, mem_sample, g_mix, w_in_ab, g_qn, g_kn, w_out_ab, w_in_c,
           rpb_c, w_out_c, g_xattn, g_mem, wq_x, wkv_x, wo_x, g_ffn, w_gu, w_down, g_final):
    layers = _prepare_layers(dict(
        g_mix=g_mix, w_in_ab=w_in_ab, g_qn=g_qn, g_kn=g_kn, w_out_ab=w_out_ab, w_in_c=w_in_c,
        rpb_c=rpb_c, w_out_c=w_out_c, g_xattn=g_xattn, g_mem=g_mem, wq_x=wq_x, wkv_x=wkv_x,
        wo_x=wo_x, g_ffn=g_ffn, w_gu=w_gu, w_down=w_down))
    g_final = g_final.reshape(1, D_MODEL)
    return (_trunk(x_prompt, mem_prompt, layers, g_final), _trunk(x_sample, mem_sample, layers, g_final))
```

```python
import functools
import math

import numpy as np
import jax
import jax.numpy as jnp
from jax import lax
from jax.experimental import pallas as pl
from jax.experimental.pallas import tpu as pltpu

F32 = jnp.float32
BF16 = jnp.bfloat16

D_MODEL = 1024
GRID_W = 64
HEAD_DIM = 64
EPS = 1e-6
NEG = -1e30
LOG2E = math.log2(math.e)
TOKEN_TILE = 512

A_Q_HEADS = 8
A_KV_HEADS = 2
A_GROUP = A_Q_HEADS // A_KV_HEADS
ROPE_THETA = 10000.0
A_Q_W = A_Q_HEADS * HEAD_DIM
A_KV_W = A_KV_HEADS * HEAD_DIM
A_W = A_Q_W + 2 * A_KV_W
A_TK = 256
A_V_ROWS = HEAD_DIM + 16
A_UNROLL = 6
A_UNROLL_BOUNDED = 16
A_SHIFT_MARGIN = 60.0
A_BOUND_LIMIT = 90.0
A_BOUND_SLACK = 1.0 + 2.0 ** -6

B_PATTERNS = ((128, 1), (512, 4), (2048, 16))
B_BRANCHES = len(B_PATTERNS)
B_HEADS = 4
B_HALF = 64
B_W = B_HEADS * HEAD_DIM
B_SUB = 2 * B_HALF
B_KEYS = 4 * B_HALF
LSE_REP = 2 * HEAD_DIM // B_HEADS
B_R = B_W + 4 * HEAD_DIM

C_HEADS = 16
C_WIN_ROWS = 8
C_WIN_COLS = 16
C_ROWS_Q = 4
C_ROWS_K = 12
C_W = C_HEADS * HEAD_DIM
C_SHIFT_MARGIN = 40.0
C_BOUND_LIMIT = 120.0
C_BOUND_SLACK = 1.0 + 2.0 ** -6

X_HEADS = 4
X_HEAD_DIM = D_MODEL // X_HEADS
D_FF = 2816

VMEM_LIMIT = 56 * 1024 * 1024


def _cparams(n_axes):
    return pltpu.CompilerParams(dimension_semantics=("arbitrary",) * n_axes,
                                vmem_limit_bytes=VMEM_LIMIT)


def _rms(x, g):
    ms = jnp.mean(x * x, axis=-1, keepdims=True)
    return x * lax.rsqrt(ms + EPS) * g


def _dot(a, b):
    return jnp.dot(a, b, preferred_element_type=F32)


def _dot_nt(a, b):
    return lax.dot_general(a, b, (((1,), (1,)), ((), ())), preferred_element_type=F32)


def _const_spec(shape):
    zeros = (0,) * len(shape)
    return pl.BlockSpec(shape, lambda *_: zeros)


def _inproj_ab_kernel(x_ref, g_ref, waT_ref, wb_ref, perm_ref, gq_ref, gk_ref, cos_ref, sin_ref,
                      qT_ref, k_ref, vT_ref, stats_ref, zb_ref):
    hb = _rms(x_ref[...], g_ref[...]).astype(BF16)
    zT = _dot_nt(waT_ref[...], hb)
    cos = cos_ref[...]
    sin = sin_ref[...]
    half = HEAD_DIM // 2

    def norm_rope(zh, gcol):
        ms = jnp.mean(zh * zh, axis=0, keepdims=True)
        y = zh * lax.rsqrt(ms + EPS) * gcol
        yr, yi = y[:half], y[half:]
        return jnp.concatenate([yr * cos - yi * sin, yr * sin + yi * cos], axis=0)

    def l2(v):
        return jnp.sqrt(jnp.sum(v * v, axis=0, keepdims=True))

    stats_ref[...] = jnp.zeros(stats_ref.shape, F32)
    for h in range(A_Q_HEADS):
        sl = slice(h * HEAD_DIM, (h + 1) * HEAD_DIM)
        q = norm_rope(zT[sl], gq_ref[...])
        qT_ref[sl, :] = q.astype(BF16)
        stats_ref[h // A_GROUP, h % A_GROUP:h % A_GROUP + 1, :] = l2(q)
    ks = [norm_rope(zT[A_Q_W + h * HEAD_DIM:A_Q_W + (h + 1) * HEAD_DIM], gk_ref[...])
          for h in range(A_KV_HEADS)]
    for h in range(A_KV_HEADS):
        stats_ref[h, A_GROUP:A_GROUP + 1, :] = l2(ks[h])
    kT = jnp.concatenate(ks, axis=0)
    k_nat = kT.T
    for h in range(A_KV_HEADS):
        k_ref[h] = k_nat[:, h * HEAD_DIM:(h + 1) * HEAD_DIM].astype(BF16)
    for h in range(A_KV_HEADS):
        for c in range(vT_ref.shape[1]):
            vT_ref[h, c] = zT[A_Q_W + A_KV_W + h * HEAD_DIM:A_Q_W + A_KV_W + (h + 1) * HEAD_DIM,
                              c * A_TK:(c + 1) * A_TK].astype(BF16)
    zb_ref[0] = _dot(hb, wb_ref[0]).astype(BF16)
    for g in range(1, B_BRANCHES):
        zb_ref[g] = _dot(perm_ref[g - 1], _dot(hb, wb_ref[g]).astype(BF16)).astype(BF16)


def _inproj_ab(x, g_mix, waT, wb, perm, gq, gk, cosT, sinT, seq):
    t = x.shape[0]
    tm = TOKEN_TILE
    n_seq_tiles = seq // tm
    return pl.pallas_call(
        _inproj_ab_kernel,
        grid=(t // tm,),
        in_specs=[
            pl.BlockSpec((tm, D_MODEL), lambda i: (i, 0)),
            _const_spec((1, D_MODEL)),
            _const_spec((A_W, D_MODEL)),
            _const_spec((B_BRANCHES, D_MODEL, 3 * B_W)),
            _const_spec((B_BRANCHES - 1, tm, tm)),
            _const_spec((HEAD_DIM, 1)),
            _const_spec((HEAD_DIM, 1)),
            pl.BlockSpec((HEAD_DIM // 2, tm), lambda i: (0, i % n_seq_tiles)),
            pl.BlockSpec((HEAD_DIM // 2, tm), lambda i: (0, i % n_seq_tiles)),
        ],
        out_specs=[
            pl.BlockSpec((A_Q_W, tm), lambda i: (0, i)),
            pl.BlockSpec((A_KV_HEADS, tm, HEAD_DIM), lambda i: (0, i, 0)),
            pl.BlockSpec((A_KV_HEADS, tm // A_TK, HEAD_DIM, A_TK), lambda i: (0, i, 0, 0)),
            pl.BlockSpec((A_KV_HEADS, 8, tm), lambda i: (0, 0, i)),
            pl.BlockSpec((B_BRANCHES, tm, 3 * B_W), lambda i: (0, i, 0)),
        ],
        out_shape=[
            jax.ShapeDtypeStruct((A_Q_W, t), BF16),
            jax.ShapeDtypeStruct((A_KV_HEADS, t, HEAD_DIM), BF16),
            jax.ShapeDtypeStruct((A_KV_HEADS, t // A_TK, HEAD_DIM, A_TK), BF16),
            jax.ShapeDtypeStruct((A_KV_HEADS, 8, t), F32),
            jax.ShapeDtypeStruct((B_BRANCHES, t, 3 * B_W), BF16),
        ],
        compiler_params=_cparams(1),
        name="inproj_ab",
    )(x, g_mix, waT, wb, perm, gq, gk, cosT, sinT)


def _gqa_kernel(qT_ref, k_ref, vT_ref, o_ref, s_scr, p_scr, cm_scr, alpha_scr, m_scr, acc_scr):
    tq = qT_ref.shape[1]
    tk = A_TK
    n_chunks = vT_ref.shape[0]
    ones = jnp.ones((A_V_ROWS - HEAD_DIM, tk), BF16)

    def scores(chunk, slot):
        k = k_ref[pl.ds(pl.multiple_of(chunk * tk, tk), tk), :]
        for h in range(A_GROUP):
            s = _dot(k, qT_ref[h * HEAD_DIM:(h + 1) * HEAD_DIM, :])
            s_scr[slot, h] = s
            cm_scr[slot, h] = jnp.max(s.reshape(tk // 8, 8, tq), axis=0)

    def probs(slot):
        for h in range(A_GROUP):
            m_old = m_scr[h]
            m_new = jnp.maximum(m_old, jnp.max(cm_scr[slot, h], axis=0, keepdims=True))
            alpha_scr[slot, h] = jnp.exp2(m_old - m_new)
            m_scr[h] = m_new
            p_scr[slot, h] = jnp.exp2(s_scr[slot, h] - m_new).astype(BF16)

    def accumulate(chunk, slot):
        vT_aug = jnp.concatenate([vT_ref[chunk], ones], axis=0)
        for h in range(A_GROUP):
            acc_scr[h] = alpha_scr[slot, h] * acc_scr[h] + _dot(vT_aug, p_scr[slot, h])

    def steady(c, count):
        for u in range(count):
            accumulate(c + u, u % 2)
            probs((u + 1) % 2)
            scores(c + u + 2, u % 2)

    m_scr[...] = jnp.full(m_scr.shape, NEG, F32)
    acc_scr[...] = jnp.zeros(acc_scr.shape, F32)
    scores(0, 0)
    scores(1, 1)
    probs(0)
    n_loop = (n_chunks - 2) // A_UNROLL

    def body(i, carry):
        steady(i * A_UNROLL, A_UNROLL)
        return carry

    if n_loop > 1:
        lax.fori_loop(0, n_loop, body, 0)
    else:
        steady(0, n_loop * A_UNROLL)
    steady(n_loop * A_UNROLL, (n_chunks - 2) % A_UNROLL)
    accumulate(n_chunks - 2, 0)
    probs(1)
    accumulate(n_chunks - 1, 1)
    oT = jnp.concatenate(
        [acc_scr[h, :HEAD_DIM, :] / acc_scr[h, HEAD_DIM:HEAD_DIM + 1, :] for h in range(A_GROUP)], axis=0)
    o_ref[...] = oT.T.astype(BF16)


def _gqa_bounded_kernel(qT_ref, k_ref, vT_ref, shift_ref, o_ref, p_scr, acc_scr):
    tq = qT_ref.shape[1]
    tk = A_TK
    n_chunks = vT_ref.shape[0]
    ones = jnp.ones((A_V_ROWS - HEAD_DIM, tk), BF16)

    def probs(chunk, slot):
        k = k_ref[pl.ds(pl.multiple_of(chunk * tk, tk), tk), :]
        for h in range(A_GROUP):
            s = _dot(k, qT_ref[h * HEAD_DIM:(h + 1) * HEAD_DIM, :])
            p_scr[slot, h] = jnp.exp2(s - shift_ref[h:h + 1, :]).astype(BF16)

    def accumulate(chunk, slot):
        vT_aug = jnp.concatenate([vT_ref[chunk], ones], axis=0)
        for h in range(A_GROUP):
            acc_scr[h] += _dot(vT_aug, p_scr[slot, h])

    def steady(c, count):
        for u in range(count):
            accumulate(c + u, u % 2)
            probs(c + u + 2, u % 2)

    acc_scr[...] = jnp.zeros(acc_scr.shape, F32)
    probs(0, 0)
    probs(1, 1)
    n_loop = (n_chunks - 2) // A_UNROLL_BOUNDED

    def body(i, carry):
        steady(i * A_UNROLL_BOUNDED, A_UNROLL_BOUNDED)
        return carry

    if n_loop > 1:
        lax.fori_loop(0, n_loop, body, 0)
    else:
        steady(0, n_loop * A_UNROLL_BOUNDED)
    steady(n_loop * A_UNROLL_BOUNDED, (n_chunks - 2) % A_UNROLL_BOUNDED)
    accumulate(n_chunks - 2, 0)
    accumulate(n_chunks - 1, 1)
    oT = jnp.concatenate(
        [acc_scr[h, :HEAD_DIM, :] / acc_scr[h, HEAD_DIM:HEAD_DIM + 1, :] for h in range(A_GROUP)], axis=0)
    o_ref[...] = oT.T.astype(BF16)


def _gqa_attention(qT, k, vT, stats, batch, seq, tq=256):
    t = batch * seq
    nq = seq // tq
    k4 = k.reshape(A_KV_HEADS, batch, seq, HEAD_DIM)
    v5 = vT.reshape(A_KV_HEADS, batch, seq // A_TK, HEAD_DIM, A_TK)
    kmax = jnp.max(stats[:, A_GROUP].reshape(A_KV_HEADS, batch, seq), axis=-1)
    bound = (stats.reshape(A_KV_HEADS, 8, batch, seq) * kmax[:, None, :, None]
             * A_BOUND_SLACK).reshape(A_KV_HEADS, 8, t)
    in_specs = [
        pl.BlockSpec((A_GROUP * HEAD_DIM, tq), lambda b, g, i: (g, b * nq + i)),
        pl.BlockSpec((None, None, seq, HEAD_DIM), lambda b, g, i: (g, b, 0, 0)),
        pl.BlockSpec((None, None, seq // A_TK, HEAD_DIM, A_TK), lambda b, g, i: (g, b, 0, 0, 0)),
    ]
    common = dict(
        grid=(batch, A_KV_HEADS, nq),
        out_specs=pl.BlockSpec((tq, A_GROUP * HEAD_DIM), lambda b, g, i: (b * nq + i, g)),
        out_shape=jax.ShapeDtypeStruct((t, A_Q_W), BF16),
        compiler_params=_cparams(3),
    )

    def bounded():
        return pl.pallas_call(
            _gqa_bounded_kernel,
            in_specs=in_specs + [pl.BlockSpec((None, 8, tq), lambda b, g, i: (g, 0, b * nq + i))],
            scratch_shapes=[pltpu.VMEM((2, A_GROUP, A_TK, tq), BF16),
                            pltpu.VMEM((A_GROUP, A_V_ROWS, tq), F32)],
            name="gqa_bounded", **common,
        )(qT, k4, v5, bound - A_SHIFT_MARGIN)

    def online():
        return pl.pallas_call(
            _gqa_kernel,
            in_specs=in_specs,
            scratch_shapes=[pltpu.VMEM((2, A_GROUP, A_TK, tq), F32),
                            pltpu.VMEM((2, A_GROUP, A_TK, tq), BF16),
                            pltpu.VMEM((2, A_GROUP, 8, tq), F32),
                            pltpu.VMEM((2, A_GROUP, 1, tq), F32),
                            pltpu.VMEM((A_GROUP, 1, tq), F32),
                            pltpu.VMEM((A_GROUP, A_V_ROWS, tq), F32)],
            name="gqa_attention", **common,
        )(qT, k4, v5)

    safe = jnp.max(bound[:, :A_GROUP]) <= A_BOUND_LIMIT
    return lax.cond(safe, bounded, online)


def _tile_perm(dil):
    c = TOKEN_TILE // dil
    dst = np.arange(TOKEN_TILE)
    src = (dst % c) * dil + dst // c
    p = np.zeros((TOKEN_TILE, TOKEN_TILE), np.float32)
    p[dst, src] = 1.0
    return p


def _seq_rows(ref, start, n, cols):
    c = ref.shape[1]
    if n <= c:
        return ref[start // c, start % c:start % c + n, cols]
    blocks = ref[start // c:(start + n) // c, :, cols]
    return blocks.reshape(n, blocks.shape[-1])


def _store_seq_rows(ref, start, cols, val):
    c = ref.shape[1]
    n = val.shape[0]
    if n <= c:
        ref[start // c, start % c:start % c + n, cols] = val
    else:
        ref[start // c:(start + n) // c, :, cols] = val.reshape(n // c, c, val.shape[-1])


def _dilated_kernel(main_ref, prev_ref, next_ref, bias_ref, r_ref, kv_scr, *, seq_len, cq):
    n = pl.program_id(2)
    kv_cols = slice(B_W, 3 * B_W)
    pair_w = 2 * HEAD_DIM
    lane = lax.broadcasted_iota(jnp.int32, (1, pair_w), 1)
    lo = lane < HEAD_DIM
    key_col = lax.broadcasted_iota(jnp.int32, (1, B_KEYS), 1)

    for rr in range(main_ref.shape[1]):
        main, res = main_ref.at[:, rr], r_ref.at[:, rr]
        kv_scr[0:B_HALF, :] = prev_ref[..., rr, :, kv_cols].reshape(B_HALF, 2 * B_W)
        for j in range(cq // B_SUB):
            kv_scr[B_HALF + j * B_SUB:B_HALF + (j + 1) * B_SUB, :] = _seq_rows(main, j * B_SUB, B_SUB, kv_cols)
        kv_scr[B_HALF + cq:, :] = next_ref[..., rr, :, kv_cols].reshape(B_HALF, 2 * B_W)

        for j in range(cq // B_SUB):
            key_pos = key_col + (n * cq + j * B_SUB - B_HALF)
            edge = jnp.where((key_pos >= 0) & (key_pos < seq_len), 0.0, NEG).astype(F32)
            krows = slice(j * B_SUB, j * B_SUB + B_KEYS)
            lses = []
            for pr in range(B_HEADS // 2):
                cols = slice(pr * pair_w, (pr + 1) * pair_w)
                q2 = _seq_rows(main, j * B_SUB, B_SUB, cols)
                k2 = kv_scr[krows, cols]
                v2 = kv_scr[krows, B_W + pr * pair_w:B_W + (pr + 1) * pair_w]
                o_pair = []
                for hh in range(2):
                    qm = jnp.where(lo if hh == 0 else ~lo, q2, jnp.zeros_like(q2))
                    s = _dot_nt(qm, k2) + bias_ref[pr * 2 + hh] + edge
                    m = jnp.max(s, axis=-1, keepdims=True)
                    p = jnp.exp(s - m)
                    l = jnp.sum(p, axis=-1, keepdims=True)
                    o_pair.append(_dot(p.astype(BF16), v2) / l)
                    lses.append(m + jnp.log(l))
                _store_seq_rows(res, j * B_SUB, cols, jnp.where(lo, o_pair[0], o_pair[1]).astype(BF16))
            tile = lses[B_HEADS - 1]
            for head in range(B_HEADS - 2, -1, -1):
                tile = jnp.where(lane < (head + 1) * LSE_REP, lses[head], tile)
            hi = tile.astype(BF16)
            _store_seq_rows(res, j * B_SUB, slice(B_W, B_W + pair_w), hi)
            _store_seq_rows(res, j * B_SUB, slice(B_W + pair_w, B_R),
                            (tile - hi.astype(F32)).astype(BF16))


def _dilated_bias(branch):
    _, dil = B_PATTERNS[branch]
    slopes = np.exp2(-8.0 * np.arange(1, B_BRANCHES * B_HEADS + 1, dtype=np.float64)
                     / (B_BRANCHES * B_HEADS)).reshape(B_BRANCHES, B_HEADS)[branch]
    a = np.arange(B_SUB)[:, None]
    c = np.arange(B_KEYS)[None, :]
    rel = (c - B_HALF) - a
    bias = -slopes[:, None, None] * (np.abs(rel) * dil).astype(np.float64)[None]
    bias = np.where((np.abs(rel) <= B_HALF)[None], bias, NEG)
    return jnp.asarray(bias, F32)


def _dilated_branch(zb, branch, batch, seq):
    _, dil = B_PATTERNS[branch]
    t = batch * seq
    seq_len = seq // dil
    c = TOKEN_TILE // dil
    n_tiles = seq // TOKEN_TILE
    cq = min(512, seq_len)
    nchunk = seq_len // cq
    n_halo = seq_len // B_HALF
    per = cq // B_HALF
    rb = min(dil, max(1, 512 // cq))
    zv = zb.reshape(B_BRANCHES, batch, n_tiles, dil, c, 3 * B_W)

    def halo_spec(pos):
        if c >= B_HALF:
            sub = c // B_HALF
            z6 = zb.reshape(B_BRANCHES, batch, n_tiles, dil, sub, B_HALF, 3 * B_W)
            return z6, pl.BlockSpec((None, None, None, rb, None, B_HALF, 3 * B_W),
                                    lambda b, r, n: (branch, b, pos(n) // sub, r, pos(n) % sub, 0, 0))
        return zv, pl.BlockSpec((None, None, B_HALF // c, rb, c, 3 * B_W),
                                lambda b, r, n: (branch, b, pos(n), r, 0, 0))

    prev_arr, prev_spec = halo_spec(lambda n: jnp.maximum(n * per - 1, 0))
    next_arr, next_spec = halo_spec(lambda n: jnp.minimum((n + 1) * per, n_halo - 1))
    res = pl.pallas_call(
        functools.partial(_dilated_kernel, seq_len=seq_len, cq=cq),
        grid=(batch, dil // rb, nchunk),
        in_specs=[pl.BlockSpec((None, None, cq // c, rb, c, 3 * B_W), lambda b, r, n: (branch, b, n, r, 0, 0)),
                  prev_spec, next_spec, _const_spec((B_HEADS, B_SUB, B_KEYS))],
        out_specs=pl.BlockSpec((None, cq // c, rb, c, B_R), lambda b, r, n: (b, n, r, 0, 0)),
        out_shape=jax.ShapeDtypeStruct((batch, n_tiles, dil, c, B_R), BF16),
        scratch_shapes=[pltpu.VMEM((cq + 2 * B_HALF, 2 * B_W), BF16)],
        compiler_params=_cparams(3),
        name=f"dilated_branch{branch}",
    )(zv, prev_arr, next_arr, _dilated_bias(branch))
    return res.reshape(t, B_R)


def _merge_branches(rs, permT_ref):
    pair_w = 2 * HEAD_DIM
    lo = lax.broadcasted_iota(jnp.int32, (1, pair_w), 1) < HEAD_DIM
    tiles = [rs[0].astype(F32)] + [_dot(permT_ref[g - 1], rs[g]) for g in range(1, B_BRANCHES)]
    lses = [tl[:, B_W:B_W + pair_w] + tl[:, B_W + pair_w:] for tl in tiles]
    mx = functools.reduce(jnp.maximum, lses)
    es = [jnp.exp(l - mx) for l in lses]
    inv = 1.0 / functools.reduce(lambda u, v: u + v, es)
    pairs = []
    for pr in range(B_HEADS // 2):
        acc = None
        for tl, e in zip(tiles, es):
            w = e * inv
            w2 = jnp.where(lo, w[:, 2 * pr * LSE_REP:2 * pr * LSE_REP + 1],
                           w[:, (2 * pr + 1) * LSE_REP:(2 * pr + 1) * LSE_REP + 1])
            term = w2 * tl[:, pr * pair_w:(pr + 1) * pair_w]
            acc = term if acc is None else acc + term
        pairs.append(acc)
    return jnp.concatenate(pairs, axis=-1).astype(BF16)


def _inproj_kernel(x_ref, g_ref, w_ref, z_ref, stats_ref):
    hb = _rms(x_ref[...], g_ref[...]).astype(BF16)
    z = _dot(hb, w_ref[...])
    z_ref[...] = z.astype(BF16)
    pair_w = 2 * HEAD_DIM
    lane = lax.broadcasted_iota(jnp.int32, (1, pair_w), 1)
    lo = lane < HEAD_DIM
    out = jnp.zeros((1, pair_w), F32)
    for b in range(2 * C_W // pair_w):
        sq = z[:, b * pair_w:(b + 1) * pair_w]
        sq = sq * sq
        tot = jnp.sum(sq, axis=-1, keepdims=True)
        first = jnp.sum(jnp.where(lo, sq, 0.0), axis=-1, keepdims=True)
        m_first = jnp.max(first, axis=0, keepdims=True)
        m_second = jnp.max(tot - first, axis=0, keepdims=True)
        out = jnp.where(lane == 2 * b, m_first, jnp.where(lane == 2 * b + 1, m_second, out))
    stats_ref[...] = jnp.broadcast_to(out, stats_ref.shape)


def _inproj(x, g, w):
    t = x.shape[0]
    tm = TOKEN_TILE
    n_out = w.shape[1]
    return pl.pallas_call(
        _inproj_kernel,
        grid=(t // tm,),
        in_specs=[pl.BlockSpec((tm, D_MODEL), lambda i: (i, 0)),
                  _const_spec((1, D_MODEL)),
                  _const_spec((D_MODEL, n_out))],
        out_specs=[pl.BlockSpec((tm, n_out), lambda i: (i, 0)),
                   pl.BlockSpec((None, 8, 2 * HEAD_DIM), lambda i: (i, 0, 0))],
        out_shape=[jax.ShapeDtypeStruct((t, n_out), BF16),
                   jax.ShapeDtypeStruct((t // tm, 8, 2 * HEAD_DIM), F32)],
        compiler_params=_cparams(1),
        name="inproj_c",
    )(x, g, w)


def _natten_kernel(q_ref, k0_ref, k1_ref, k2_ref, v0_ref, v1_ref, v2_ref, bias_ref, o_ref):
    k_refs = (k0_ref, k1_ref, k2_ref)
    v_refs = (v0_ref, v1_ref, v2_ref)
    lane = lax.broadcasted_iota(jnp.int32, (1, 2 * HEAD_DIM), 1)
    lo = lane < HEAD_DIM
    kb = k0_ref.shape[0]
    for pr in range(C_HEADS // 2):
        cols = slice(pr * 2 * HEAD_DIM, (pr + 1) * 2 * HEAD_DIM)
        q2 = q_ref[:, cols]
        ks = [r[:, cols] for r in k_refs]
        vs = [r[:, cols] for r in v_refs]
        o_pair = []
        for hh in range(2):
            qm = jnp.where(lo if hh == 0 else ~lo, q2, jnp.zeros_like(q2))
            s = jnp.concatenate([_dot_nt(qm, kk) for kk in ks], axis=-1) + bias_ref[pr * 2 + hh]
            m = jnp.max(s, axis=-1, keepdims=True)
            p = jnp.exp(s - m)
            l = jnp.sum(p, axis=-1, keepdims=True)
            pb = p.astype(BF16)
            o = _dot(pb[:, :kb], vs[0])
            for i in range(1, len(vs)):
                o = o + _dot(pb[:, i * kb:(i + 1) * kb], vs[i])
            o_pair.append(o / l)
        o_ref[:, cols] = jnp.where(lo, o_pair[0], o_pair[1]).astype(BF16)


def _natten_bounded_kernel(q_ref, k0_ref, k1_ref, k2_ref, v0_ref, v1_ref, v2_ref, bias_ref, shift_ref, o_ref):
    k_refs = (k0_ref, k1_ref, k2_ref)
    v_refs = (v0_ref, v1_ref, v2_ref)
    pair_w = 2 * HEAD_DIM
    lo = lax.broadcasted_iota(jnp.int32, (1, pair_w), 1) < HEAD_DIM
    kb = k0_ref.shape[0]
    one = jnp.ones((), BF16)
    for pr in range(C_HEADS // 2):
        cols = slice(pr * pair_w, (pr + 1) * pair_w)
        q2 = q_ref[:, cols]
        ks = [r[:, cols] for r in k_refs]
        vs = [r[:, cols] for r in v_refs]
        o_pair = []
        for hh in range(2):
            own = lo if hh == 0 else ~lo
            qm = jnp.where(own, q2, jnp.zeros_like(q2))
            head = pr * 2 + hh
            p = jnp.exp(jnp.concatenate([_dot_nt(qm, kk) for kk in ks], axis=-1)
                        + (bias_ref[head] - shift_ref[head:head + 1, :])).astype(BF16)
            r = None
            for i in range(len(vs)):
                t = _dot(p[:, i * kb:(i + 1) * kb], jnp.where(own, vs[i], one))
                r = t if r is None else r + t
            l = r[:, HEAD_DIM:HEAD_DIM + 1] if hh == 0 else r[:, 0:1]
            o_pair.append(r / l)
        o_ref[:, cols] = jnp.where(lo, o_pair[0], o_pair[1]).astype(BF16)


def _natten_bias(rpb):
    c = np.arange(GRID_W)[:, None]
    kc = np.arange(GRID_W)[None, :]
    cstart = np.clip(c - C_WIN_COLS // 2, 0, GRID_W - C_WIN_COLS)
    col_valid = (kc >= cstart) & (kc < cstart + C_WIN_COLS)
    padded = jnp.pad(rpb.astype(F32), ((0, 0), (0, 0), (GRID_W, GRID_W)))
    base = GRID_W + C_WIN_COLS - 1
    blocks = jnp.stack([padded[:, :, base - q:base - q + GRID_W] for q in range(GRID_W)], axis=2)
    blocks = jnp.where(col_valid[None, None], blocks, NEG)
    masked = jnp.full((rpb.shape[0], GRID_W, GRID_W), NEG, F32)
    tables = []
    for off, first in ((0, lambda dr: 0), (-4, lambda dr: dr), (-8, lambda dr: 4)):
        rows = []
        for dr in range(C_ROWS_Q):
            rows.append(jnp.concatenate(
                [blocks[:, off + kri - dr + C_WIN_ROWS - 1]
                 if first(dr) <= kri < first(dr) + C_WIN_ROWS else masked
                 for kri in range(C_ROWS_K)], axis=-1))
        tables.append(jnp.concatenate(rows, axis=1))
    return jnp.stack(tables)


def _natten(z, stats, rpb, bias, batch, seq):
    t = batch * seq
    qb = C_ROWS_Q * GRID_W
    nrb = seq // qb
    nkb = C_ROWS_K // C_ROWS_Q

    def kspec(i, col):
        return pl.BlockSpec(
            (qb, C_W), lambda rb, b: (b * nrb + jnp.clip(rb - 1, 0, nrb - nkb) + i, col))

    def variant(rb):
        return jnp.where(rb == 0, 0, jnp.where(rb == nrb - 1, 2, 1))

    def call(body, *extra):
        return pl.pallas_call(
            body,
            grid=(nrb, batch),
            in_specs=[pl.BlockSpec((qb, C_W), lambda rb, b: (b * nrb + rb, 0))]
            + [kspec(i, 1) for i in range(nkb)]
            + [kspec(i, 2) for i in range(nkb)]
            + [pl.BlockSpec((None, C_HEADS, qb, C_ROWS_K * GRID_W), lambda rb, b: (variant(rb), 0, 0, 0))]
            + [_const_spec(e.shape) for e in extra],
            out_specs=pl.BlockSpec((qb, C_W), lambda rb, b: (b * nrb + rb, 0)),
            out_shape=jax.ShapeDtypeStruct((t, C_W), BF16),
            compiler_params=_cparams(2),
            name="natten",
        )(z, z, z, z, z, z, z, bias, *extra)

    norms = jnp.sqrt(jnp.max(stats[:, 0, :2 * C_HEADS], axis=0)) * C_BOUND_SLACK
    qk_bound = norms[:C_HEADS] * norms[C_HEADS:]
    rpb_max = jnp.max(rpb, axis=(1, 2))
    rpb_range = rpb_max - jnp.min(rpb, axis=(1, 2))
    safe = jnp.max(2.0 * qk_bound + rpb_range) <= C_BOUND_LIMIT
    shift = jnp.broadcast_to((qk_bound + rpb_max - C_SHIFT_MARGIN)[:, None], (C_HEADS, C_ROWS_K * GRID_W))
    return lax.cond(safe, lambda: call(_natten_bounded_kernel, shift), lambda: call(_natten_kernel))


def _mem_kv_kernel(mem_ref, g_ref, w_ref, kv_ref):
    mb = _rms(mem_ref[...], g_ref[...]).astype(BF16)
    kv_ref[...] = _dot(mb, w_ref[...]).astype(BF16)


def _mem_kv(mem, g, w):
    batch, n_mem, _ = mem.shape
    return pl.pallas_call(
        _mem_kv_kernel,
        grid=(batch,),
        in_specs=[pl.BlockSpec((None, n_mem, D_MODEL), lambda b: (b, 0, 0)),
                  _const_spec((1, D_MODEL)),
                  _const_spec((D_MODEL, 2 * D_MODEL))],
        out_specs=pl.BlockSpec((None, n_mem, 2 * D_MODEL), lambda b: (b, 0, 0)),
        out_shape=jax.ShapeDtypeStruct((batch, n_mem, 2 * D_MODEL), BF16),
        compiler_params=_cparams(1),
        name="mem_kv",
    )(mem, g, w)


def _xattn_tail(x, g_ref, wq_ref, kv_ref, wo_ref, y_ref):
    q = _dot(_rms(x, g_ref[...]).astype(BF16), wq_ref[...]).astype(BF16)
    outs = []
    for h in range(X_HEADS):
        cols = slice(h * X_HEAD_DIM, (h + 1) * X_HEAD_DIM)
        s = _dot_nt(q[:, cols], kv_ref[:, cols])
        m = jnp.max(s, axis=-1, keepdims=True)
        p = jnp.exp(s - m)
        l = jnp.sum(p, axis=-1, keepdims=True)
        o = _dot(p.astype(BF16), kv_ref[:, D_MODEL + h * X_HEAD_DIM:D_MODEL + (h + 1) * X_HEAD_DIM])
        outs.append((o / l).astype(BF16))
    y_ref[...] = x + _dot(jnp.concatenate(outs, axis=-1), wo_ref[...])


def _mix_ab_xattn_kernel(x_ref, oa_ref, r0_ref, r1_ref, r2_ref, permT_ref, woa_ref, wob_ref,
                         g_ref, wq_ref, kv_ref, wo_ref, y_ref):
    ob = _merge_branches([r0_ref[...], r1_ref[...], r2_ref[...]], permT_ref)
    x = x_ref[...] + _dot(oa_ref[...], woa_ref[...]) + _dot(ob, wob_ref[...])
    _xattn_tail(x, g_ref, wq_ref, kv_ref, wo_ref, y_ref)


def _mix_c_xattn_kernel(x_ref, oc_ref, woc_ref, g_ref, wq_ref, kv_ref, wo_ref, y_ref):
    x = x_ref[...] + _dot(oc_ref[...], woc_ref[...])
    _xattn_tail(x, g_ref, wq_ref, kv_ref, wo_ref, y_ref)


def _mix_xattn(body, x, token_args, const_args, g, wq, kv, wo, seq):
    t = x.shape[0]
    tm = TOKEN_TILE
    n_seq_tiles = seq // tm
    n_mem = kv.shape[1]
    token_spec = lambda a: pl.BlockSpec((tm, a.shape[1]), lambda i: (i, 0))
    in_specs = ([token_spec(x)] + [token_spec(a) for a in token_args]
                + [_const_spec(a.shape) for a in const_args]
                + [_const_spec((1, D_MODEL)),
                   _const_spec((D_MODEL, D_MODEL)),
                   pl.BlockSpec((None, n_mem, 2 * D_MODEL), lambda i: (i // n_seq_tiles, 0, 0)),
                   _const_spec((D_MODEL, D_MODEL))])
    return pl.pallas_call(
        body,
        grid=(t // tm,),
        in_specs=in_specs,
        out_specs=pl.BlockSpec((tm, D_MODEL), lambda i: (i, 0)),
        out_shape=jax.ShapeDtypeStruct((t, D_MODEL), F32),
        compiler_params=_cparams(1),
        name="mix_xattn",
    )(x, *token_args, *const_args, g, wq, kv, wo)


def _swiglu_kernel(x_ref, g_ref, wg_ref, wu_ref, wd_ref, gf_ref, y_ref, *, final_norm):
    x = x_ref[...]
    hb = _rms(x, g_ref[...]).astype(BF16)
    gate = _dot(hb, wg_ref[...])
    up = _dot(hb, wu_ref[...])
    act = (gate / (1.0 + jnp.exp(-gate)) * up).astype(BF16)
    y = x + _dot(act, wd_ref[...])
    if final_norm:
        y = _rms(y, gf_ref[...])
    y_ref[...] = y


def _swiglu(x, g, wg, wu, wd, g_final, final_norm, tm=256):
    t = x.shape[0]
    return pl.pallas_call(
        functools.partial(_swiglu_kernel, final_norm=final_norm),
        grid=(t // tm,),
        in_specs=[pl.BlockSpec((tm, D_MODEL), lambda i: (i, 0)),
                  _const_spec((1, D_MODEL)),
                  _const_spec((D_MODEL, D_FF)),
                  _const_spec((D_MODEL, D_FF)),
                  _const_spec((D_FF, D_MODEL)),
                  _const_spec((1, D_MODEL))],
        out_specs=pl.BlockSpec((tm, D_MODEL), lambda i: (i, 0)),
        out_shape=jax.ShapeDtypeStruct((t, D_MODEL), F32),
        compiler_params=_cparams(1),
        name="swiglu",
    )(x, g, wg, wu, wd, g_final)


def _rope_tables(seq):
    tok = jnp.arange(seq, dtype=jnp.int32)
    row = (tok // GRID_W).astype(F32)
    col = (tok % GRID_W).astype(F32)
    axis_dim = HEAD_DIM // 2
    inv_freq = ROPE_THETA ** (-jnp.arange(0, axis_dim, 2, dtype=F32) / axis_dim)
    ang = jnp.concatenate([row[:, None] * inv_freq, col[:, None] * inv_freq], axis=-1)
    return jnp.cos(ang).T, jnp.sin(ang).T


def _prepare_ab(w_in, g_qn, g_kn, w_out):
    perm = np.concatenate([np.arange(0, HEAD_DIM, 2), np.arange(1, HEAD_DIM, 2)])
    cols = np.concatenate(
        [h * HEAD_DIM + perm for h in range(A_Q_HEADS + A_KV_HEADS)]
        + [np.arange(A_Q_W + A_KV_W, A_W)])
    waT = w_in[:, cols].T.astype(BF16)
    zbw = w_in[:, A_W:].reshape(D_MODEL, 3, B_BRANCHES, B_W)
    qscale = HEAD_DIM ** -0.5
    wb = jnp.stack([jnp.concatenate([zbw[:, 0, g] * qscale, zbw[:, 1, g], zbw[:, 2, g]], axis=1)
                    for g in range(B_BRANCHES)]).astype(BF16)
    gq = (g_qn[perm] * (qscale * LOG2E)).reshape(HEAD_DIM, 1).astype(F32)
    gk = g_kn[perm].reshape(HEAD_DIM, 1).astype(F32)
    perms = np.stack([_tile_perm(dil) for _, dil in B_PATTERNS[1:]])
    tile_perm = jnp.asarray(perms, BF16)
    tile_perm_t = jnp.asarray(perms.transpose(0, 2, 1), BF16)
    return waT, wb, tile_perm, tile_perm_t, gq, gk, w_out[:A_Q_W].astype(BF16), w_out[A_Q_W:].astype(BF16)


def _prepare_layers(p):
    depth = p["g_mix"].shape[0]
    row = lambda g: g.reshape(1, D_MODEL)
    layers = []
    for l in range(depth):
        lay = dict(g_mix=row(p["g_mix"][l]), g_xattn=row(p["g_xattn"][l]), g_mem=row(p["g_mem"][l]),
                   g_ffn=row(p["g_ffn"][l]),
                   wq=(p["wq_x"][l] * X_HEAD_DIM ** -0.5).astype(BF16), wkv=p["wkv_x"][l].astype(BF16),
                   wo=p["wo_x"][l].astype(BF16), wg=p["w_gu"][l][:, :D_FF].astype(BF16),
                   wu=p["w_gu"][l][:, D_FF:].astype(BF16), wd=p["w_down"][l].astype(BF16))
        i = l // 2
        if l % 2 == 0:
            lay["ab"] = _prepare_ab(p["w_in_ab"][i], p["g_qn"][i], p["g_kn"][i], p["w_out_ab"][i])
        else:
            w_in = p["w_in_c"][i]
            lay["w_in_c"] = jnp.concatenate(
                [w_in[:, :C_W] * HEAD_DIM ** -0.5, w_in[:, C_W:]], axis=1).astype(BF16)
            lay["rpb_c"] = p["rpb_c"][i]
            lay["bias_c"] = _natten_bias(p["rpb_c"][i])
            lay["w_out_c"] = p["w_out_c"][i].astype(BF16)
        layers.append(lay)
    return layers


def _trunk(x, mem, layers, g_final):
    batch, seq, _ = x.shape
    x = x.reshape(batch * seq, D_MODEL)
    for l, lay in enumerate(layers):
        if l % 2 == 0:
            waT, wb, perm, permT, gq, gk, wo_a, wo_b = lay["ab"]
            cosT, sinT = _rope_tables(seq)
            qT, k, vT, stats, zb = _inproj_ab(x, lay["g_mix"], waT, wb, perm, gq, gk, cosT, sinT, seq)
            oa = _gqa_attention(qT, k, vT, stats, batch, seq)
            rs = [_dilated_branch(zb, g, batch, seq) for g in range(B_BRANCHES)]
            body, token_args, const_args = _mix_ab_xattn_kernel, [oa] + rs, [permT, wo_a, wo_b]
        else:
            z, stats = _inproj(x, lay["g_mix"], lay["w_in_c"])
            oc = _natten(z, stats, lay["rpb_c"], lay["bias_c"], batch, seq)
            body, token_args, const_args = _mix_c_xattn_kernel, [oc], [lay["w_out_c"]]
        kv = _mem_kv(mem, lay["g_mem"], lay["wkv"])
        x = _mix_xattn(body, x, token_args, const_args, lay["g_xattn"], lay["wq"], kv, lay["wo"], seq)
        x = _swiglu(x, lay["g_ffn"], lay["wg"], lay["wu"], lay["wd"], g_final,
                    final_norm=(l == len(layers) - 1))
    return x.reshape(batch, seq, D_MODEL)


def kernel(x_prompt, x_sample, mem_prompt, mem_sample, g_mix, w_in_ab, g_qn, g_kn, w_out_ab, w_in_c,
           rpb_c, w_out_c, g_xattn, g_mem, wq_x, wkv_x, wo_x, g_ffn, w_gu, w_down, g_final):
    layers = _prepare_layers(dict(
        g_mix=g_mix, w_in_ab=w_in_ab, g_qn=g_qn, g_kn=g_kn, w_out_ab=w_out_ab, w_in_c=w_in_c,
        rpb_c=rpb_c, w_out_c=w_out_c, g_xattn=g_xattn, g_mem=g_mem, wq_x=wq_x, wkv_x=wkv_x,
        wo_x=wo_x, g_ffn=g_ffn, w_gu=w_gu, w_down=w_down))
    g_final = g_final.reshape(1, D_MODEL)
    return (_trunk(x_prompt, mem_prompt, layers, g_final), _trunk(x_sample, mem_sample, layers, g_final))
```

```python
import functools
import math

import numpy as np
import jax
import jax.numpy as jnp
from jax import lax
from jax.experimental import pallas as pl
from jax.experimental.pallas import tpu as pltpu

F32 = jnp.float32
BF16 = jnp.bfloat16

D_MODEL = 1024
GRID_W = 64
HEAD_DIM = 64
EPS = 1e-6
NEG = -1e30
LOG2E = math.log2(math.e)
TOKEN_TILE = 512

A_Q_HEADS = 8
A_KV_HEADS = 2
A_GROUP = A_Q_HEADS // A_KV_HEADS
ROPE_THETA = 10000.0
A_Q_W = A_Q_HEADS * HEAD_DIM
A_KV_W = A_KV_HEADS * HEAD_DIM
A_W = A_Q_W + 2 * A_KV_W
A_TQ = 1024
A_TK = 256
A_V_ROWS = HEAD_DIM + 16
A_UNROLL = 6
A_UNROLL_BOUNDED = 16
A_SHIFT_MARGIN = 60.0
A_BOUND_LIMIT = 90.0
A_BOUND_SLACK = 1.0 + 2.0 ** -6

B_PATTERNS = ((128, 1), (512, 4), (2048, 16))
B_BRANCHES = len(B_PATTERNS)
B_HEADS = 4
B_HALF = 64
B_W = B_HEADS * HEAD_DIM
B_SUB = 2 * B_HALF
B_KEYS = 4 * B_HALF
B_CHUNK = 1024
LSE_REP = 2 * HEAD_DIM // B_HEADS
B_R = B_W + 4 * HEAD_DIM

C_HEADS = 16
C_WIN_ROWS = 8
C_WIN_COLS = 16
C_ROWS_Q = 4
C_ROWS_K = 12
C_W = C_HEADS * HEAD_DIM
C_SHIFT_MARGIN = 40.0
C_BOUND_LIMIT = 120.0
C_BOUND_SLACK = 1.0 + 2.0 ** -6

X_HEADS = 4
X_HEAD_DIM = D_MODEL // X_HEADS
D_FF = 2816

VMEM_LIMIT = 56 * 1024 * 1024


def _cparams(n_axes):
    return pltpu.CompilerParams(dimension_semantics=("arbitrary",) * n_axes,
                                vmem_limit_bytes=VMEM_LIMIT)


def _rms(x, g):
    ms = jnp.mean(x * x, axis=-1, keepdims=True)
    return x * lax.rsqrt(ms + EPS) * g


def _dot(a, b):
    return jnp.dot(a, b, preferred_element_type=F32)


def _dot_nt(a, b):
    return lax.dot_general(a, b, (((1,), (1,)), ((), ())), preferred_element_type=F32)


def _const_spec(shape):
    zeros = (0,) * len(shape)
    return pl.BlockSpec(shape, lambda *_: zeros)


def _inproj_ab_kernel(x_ref, g_ref, waT_ref, wb_ref, perm_ref, gq_ref, gk_ref, cos_ref, sin_ref,
                      qT_ref, k_ref, vT_ref, stats_ref, zb_ref):
    hb = _rms(x_ref[...], g_ref[...]).astype(BF16)
    zT = _dot_nt(waT_ref[...], hb)
    cos = cos_ref[...]
    sin = sin_ref[...]
    half = HEAD_DIM // 2

    def norm_rope(zh, gcol):
        ms = jnp.mean(zh * zh, axis=0, keepdims=True)
        y = zh * lax.rsqrt(ms + EPS) * gcol
        yr, yi = y[:half], y[half:]
        return jnp.concatenate([yr * cos - yi * sin, yr * sin + yi * cos], axis=0)

    def l2(v):
        return jnp.sqrt(jnp.sum(v * v, axis=0, keepdims=True))

    stats_ref[...] = jnp.zeros(stats_ref.shape, F32)
    for h in range(A_Q_HEADS):
        sl = slice(h * HEAD_DIM, (h + 1) * HEAD_DIM)
        q = norm_rope(zT[sl], gq_ref[...])
        qT_ref[sl, :] = q.astype(BF16)
        stats_ref[h // A_GROUP, h % A_GROUP:h % A_GROUP + 1, :] = l2(q)
    ks = [norm_rope(zT[A_Q_W + h * HEAD_DIM:A_Q_W + (h + 1) * HEAD_DIM], gk_ref[...])
          for h in range(A_KV_HEADS)]
    for h in range(A_KV_HEADS):
        stats_ref[h, A_GROUP:A_GROUP + 1, :] = l2(ks[h])
    kT = jnp.concatenate(ks, axis=0)
    k_nat = kT.T
    for h in range(A_KV_HEADS):
        k_ref[h] = k_nat[:, h * HEAD_DIM:(h + 1) * HEAD_DIM].astype(BF16)
    for h in range(A_KV_HEADS):
        for c in range(vT_ref.shape[1]):
            vT_ref[h, c] = zT[A_Q_W + A_KV_W + h * HEAD_DIM:A_Q_W + A_KV_W + (h + 1) * HEAD_DIM,
                              c * A_TK:(c + 1) * A_TK].astype(BF16)
    zb_ref[0] = _dot(hb, wb_ref[0]).astype(BF16)
    for g in range(1, B_BRANCHES):
        zb_ref[g] = _dot(perm_ref[g - 1], _dot(hb, wb_ref[g]).astype(BF16)).astype(BF16)


def _inproj_ab(x, g_mix, waT, wb, perm, gq, gk, cosT, sinT, seq):
    t = x.shape[0]
    tm = TOKEN_TILE
    n_seq_tiles = seq // tm
    return pl.pallas_call(
        _inproj_ab_kernel,
        grid=(t // tm,),
        in_specs=[
            pl.BlockSpec((tm, D_MODEL), lambda i: (i, 0)),
            _const_spec((1, D_MODEL)),
            _const_spec((A_W, D_MODEL)),
            _const_spec((B_BRANCHES, D_MODEL, 3 * B_W)),
            _const_spec((B_BRANCHES - 1, tm, tm)),
            _const_spec((HEAD_DIM, 1)),
            _const_spec((HEAD_DIM, 1)),
            pl.BlockSpec((HEAD_DIM // 2, tm), lambda i: (0, i % n_seq_tiles)),
            pl.BlockSpec((HEAD_DIM // 2, tm), lambda i: (0, i % n_seq_tiles)),
        ],
        out_specs=[
            pl.BlockSpec((A_Q_W, tm), lambda i: (0, i)),
            pl.BlockSpec((A_KV_HEADS, tm, HEAD_DIM), lambda i: (0, i, 0)),
            pl.BlockSpec((A_KV_HEADS, tm // A_TK, HEAD_DIM, A_TK), lambda i: (0, i, 0, 0)),
            pl.BlockSpec((A_KV_HEADS, 8, tm), lambda i: (0, 0, i)),
            pl.BlockSpec((B_BRANCHES, tm, 3 * B_W), lambda i: (0, i, 0)),
        ],
        out_shape=[
            jax.ShapeDtypeStruct((A_Q_W, t), BF16),
            jax.ShapeDtypeStruct((A_KV_HEADS, t, HEAD_DIM), BF16),
            jax.ShapeDtypeStruct((A_KV_HEADS, t // A_TK, HEAD_DIM, A_TK), BF16),
            jax.ShapeDtypeStruct((A_KV_HEADS, 8, t), F32),
            jax.ShapeDtypeStruct((B_BRANCHES, t, 3 * B_W), BF16),
        ],
        compiler_params=_cparams(1),
        name="inproj_ab",
    )(x, g_mix, waT, wb, perm, gq, gk, cosT, sinT)


def _gqa_kernel(qT_ref, k_ref, vT_ref, o_ref, s_scr, p_scr, cm_scr, alpha_scr, m_scr, acc_scr):
    tq = qT_ref.shape[1]
    tk = A_TK
    n_chunks = vT_ref.shape[0]
    ones = jnp.ones((A_V_ROWS - HEAD_DIM, tk), BF16)

    def scores(chunk, slot):
        k = k_ref[pl.ds(pl.multiple_of(chunk * tk, tk), tk), :]
        for h in range(A_GROUP):
            s = _dot(k, qT_ref[h * HEAD_DIM:(h + 1) * HEAD_DIM, :])
            s_scr[slot, h] = s
            cm_scr[slot, h] = jnp.max(s.reshape(tk // 8, 8, tq), axis=0)

    def probs(slot):
        for h in range(A_GROUP):
            m_old = m_scr[h]
            m_new = jnp.maximum(m_old, jnp.max(cm_scr[slot, h], axis=0, keepdims=True))
            alpha_scr[slot, h] = jnp.exp2(m_old - m_new)
            m_scr[h] = m_new
            p_scr[slot, h] = jnp.exp2(s_scr[slot, h] - m_new).astype(BF16)

    def accumulate(chunk, slot):
        vT_aug = jnp.concatenate([vT_ref[chunk], ones], axis=0)
        for h in range(A_GROUP):
            acc_scr[h] = alpha_scr[slot, h] * acc_scr[h] + _dot(vT_aug, p_scr[slot, h])

    def steady(c, count):
        for u in range(count):
            accumulate(c + u, u % 2)
            probs((u + 1) % 2)
            scores(c + u + 2, u % 2)

    m_scr[...] = jnp.full(m_scr.shape, NEG, F32)
    acc_scr[...] = jnp.zeros(acc_scr.shape, F32)
    scores(0, 0)
    scores(1, 1)
    probs(0)
    n_loop = (n_chunks - 2) // A_UNROLL

    def body(i, carry):
        steady(i * A_UNROLL, A_UNROLL)
        return carry

    if n_loop > 1:
        lax.fori_loop(0, n_loop, body, 0)
    else:
        steady(0, n_loop * A_UNROLL)
    steady(n_loop * A_UNROLL, (n_chunks - 2) % A_UNROLL)
    accumulate(n_chunks - 2, 0)
    probs(1)
    accumulate(n_chunks - 1, 1)
    oT = jnp.concatenate(
        [acc_scr[h, :HEAD_DIM, :] / acc_scr[h, HEAD_DIM:HEAD_DIM + 1, :] for h in range(A_GROUP)], axis=0)
    o_ref[...] = oT.T.astype(BF16)


def _gqa_bounded_kernel(qT_ref, k_ref, vT_ref, shift_ref, o_ref, p_scr, acc_scr):
    tq = qT_ref.shape[1]
    tk = A_TK
    n_chunks = vT_ref.shape[0]
    ones = jnp.ones((A_V_ROWS - HEAD_DIM, tk), BF16)

    def probs(chunk, slot):
        k = k_ref[pl.ds(pl.multiple_of(chunk * tk, tk), tk), :]
        for h in range(A_GROUP):
            s = _dot(k, qT_ref[h * HEAD_DIM:(h + 1) * HEAD_DIM, :])
            p_scr[slot, h] = jnp.exp2(s - shift_ref[h:h + 1, :]).astype(BF16)

    def accumulate(chunk, slot):
        vT_aug = jnp.concatenate([vT_ref[chunk], ones], axis=0)
        for h in range(A_GROUP):
            acc_scr[h] += _dot(vT_aug, p_scr[slot, h])

    def steady(c, count):
        for u in range(count):
            accumulate(c + u, u % 2)
            probs(c + u + 2, u % 2)

    acc_scr[...] = jnp.zeros(acc_scr.shape, F32)
    probs(0, 0)
    probs(1, 1)
    n_loop = (n_chunks - 2) // A_UNROLL_BOUNDED

    def body(i, carry):
        steady(i * A_UNROLL_BOUNDED, A_UNROLL_BOUNDED)
        return carry

    if n_loop > 1:
        lax.fori_loop(0, n_loop, body, 0)
    else:
        steady(0, n_loop * A_UNROLL_BOUNDED)
    steady(n_loop * A_UNROLL_BOUNDED, (n_chunks - 2) % A_UNROLL_BOUNDED)
    accumulate(n_chunks - 2, 0)
    accumulate(n_chunks - 1, 1)
    oT = jnp.concatenate(
        [acc_scr[h, :HEAD_DIM, :] / acc_scr[h, HEAD_DIM:HEAD_DIM + 1, :] for h in range(A_GROUP)], axis=0)
    o_ref[...] = oT.T.astype(BF16)


def _gqa_attention(qT, k, vT, stats, batch, seq, tq=A_TQ):
    t = batch * seq
    nq = seq // tq
    k4 = k.reshape(A_KV_HEADS, batch, seq, HEAD_DIM)
    v5 = vT.reshape(A_KV_HEADS, batch, seq // A_TK, HEAD_DIM, A_TK)
    kmax = jnp.max(stats[:, A_GROUP].reshape(A_KV_HEADS, batch, seq), axis=-1)
    bound = (stats.reshape(A_KV_HEADS, 8, batch, seq) * kmax[:, None, :, None]
             * A_BOUND_SLACK).reshape(A_KV_HEADS, 8, t)
    in_specs = [
        pl.BlockSpec((A_GROUP * HEAD_DIM, tq), lambda b, g, i: (g, b * nq + i)),
        pl.BlockSpec((None, None, seq, HEAD_DIM), lambda b, g, i: (g, b, 0, 0)),
        pl.BlockSpec((None, None, seq // A_TK, HEAD_DIM, A_TK), lambda b, g, i: (g, b, 0, 0, 0)),
    ]
    common = dict(
        grid=(batch, A_KV_HEADS, nq),
        out_specs=pl.BlockSpec((tq, A_GROUP * HEAD_DIM), lambda b, g, i: (b * nq + i, g)),
        out_shape=jax.ShapeDtypeStruct((t, A_Q_W), BF16),
        compiler_params=_cparams(3),
    )

    def bounded():
        return pl.pallas_call(
            _gqa_bounded_kernel,
            in_specs=in_specs + [pl.BlockSpec((None, 8, tq), lambda b, g, i: (g, 0, b * nq + i))],
            scratch_shapes=[pltpu.VMEM((2, A_GROUP, A_TK, tq), BF16),
                            pltpu.VMEM((A_GROUP, A_V_ROWS, tq), F32)],
            name="gqa_bounded", **common,
        )(qT, k4, v5, bound - A_SHIFT_MARGIN)

    def online():
        return pl.pallas_call(
            _gqa_kernel,
            in_specs=in_specs,
            scratch_shapes=[pltpu.VMEM((2, A_GROUP, A_TK, tq), F32),
                            pltpu.VMEM((2, A_GROUP, A_TK, tq), BF16),
                            pltpu.VMEM((2, A_GROUP, 8, tq), F32),
                            pltpu.VMEM((2, A_GROUP, 1, tq), F32),
                            pltpu.VMEM((A_GROUP, 1, tq), F32),
                            pltpu.VMEM((A_GROUP, A_V_ROWS, tq), F32)],
            name="gqa_attention", **common,
        )(qT, k4, v5)

    safe = jnp.max(bound[:, :A_GROUP]) <= A_BOUND_LIMIT
    return lax.cond(safe, bounded, online)


def _tile_perm(dil):
    c = TOKEN_TILE // dil
    dst = np.arange(TOKEN_TILE)
    src = (dst % c) * dil + dst // c
    p = np.zeros((TOKEN_TILE, TOKEN_TILE), np.float32)
    p[dst, src] = 1.0
    return p


def _seq_rows(ref, start, n, cols):
    c = ref.shape[1]
    if n <= c:
        return ref[start // c, start % c:start % c + n, cols]
    blocks = ref[start // c:(start + n) // c, :, cols]
    return blocks.reshape(n, blocks.shape[-1])


def _store_seq_rows(ref, start, cols, val):
    c = ref.shape[1]
    n = val.shape[0]
    if n <= c:
        ref[start // c, start % c:start % c + n, cols] = val
    else:
        ref[start // c:(start + n) // c, :, cols] = val.reshape(n // c, c, val.shape[-1])


def _dilated_kernel(main_ref, prev_ref, next_ref, bias_ref, r_ref, kv_scr, *, seq_len, cq):
    n = pl.program_id(2)
    kv_cols = slice(B_W, 3 * B_W)
    pair_w = 2 * HEAD_DIM
    lane = lax.broadcasted_iota(jnp.int32, (1, pair_w), 1)
    lo = lane < HEAD_DIM
    key_col = lax.broadcasted_iota(jnp.int32, (1, B_KEYS), 1)

    for rr in range(main_ref.shape[1]):
        main, res = main_ref.at[:, rr], r_ref.at[:, rr]
        kv_scr[0:B_HALF, :] = prev_ref[..., rr, :, kv_cols].reshape(B_HALF, 2 * B_W)
        for j in range(cq // B_SUB):
            kv_scr[B_HALF + j * B_SUB:B_HALF + (j + 1) * B_SUB, :] = _seq_rows(main, j * B_SUB, B_SUB, kv_cols)
        kv_scr[B_HALF + cq:, :] = next_ref[..., rr, :, kv_cols].reshape(B_HALF, 2 * B_W)

        for j in range(cq // B_SUB):
            key_pos = key_col + (n * cq + j * B_SUB - B_HALF)
            edge = jnp.where((key_pos >= 0) & (key_pos < seq_len), 0.0, NEG).astype(F32)
            krows = slice(j * B_SUB, j * B_SUB + B_KEYS)
            lses = []
            for pr in range(B_HEADS // 2):
                cols = slice(pr * pair_w, (pr + 1) * pair_w)
                q2 = _seq_rows(main, j * B_SUB, B_SUB, cols)
                k2 = kv_scr[krows, cols]
                v2 = kv_scr[krows, B_W + pr * pair_w:B_W + (pr + 1) * pair_w]
                o_pair = []
                for hh in range(2):
                    qm = jnp.where(lo if hh == 0 else ~lo, q2, jnp.zeros_like(q2))
                    s = _dot_nt(qm, k2) + bias_ref[pr * 2 + hh] + edge
                    m = jnp.max(s, axis=-1, keepdims=True)
                    p = jnp.exp(s - m)
                    l = jnp.sum(p, axis=-1, keepdims=True)
                    o_pair.append(_dot(p.astype(BF16), v2) / l)
                    lses.append(m + jnp.log(l))
                _store_seq_rows(res, j * B_SUB, cols, jnp.where(lo, o_pair[0], o_pair[1]).astype(BF16))
            tile = lses[B_HEADS - 1]
            for head in range(B_HEADS - 2, -1, -1):
                tile = jnp.where(lane < (head + 1) * LSE_REP, lses[head], tile)
            hi = tile.astype(BF16)
            _store_seq_rows(res, j * B_SUB, slice(B_W, B_W + pair_w), hi)
            _store_seq_rows(res, j * B_SUB, slice(B_W + pair_w, B_R),
                            (tile - hi.astype(F32)).astype(BF16))


def _dilated_bias(branch):
    _, dil = B_PATTERNS[branch]
    slopes = np.exp2(-8.0 * np.arange(1, B_BRANCHES * B_HEADS + 1, dtype=np.float64)
                     / (B_BRANCHES * B_HEADS)).reshape(B_BRANCHES, B_HEADS)[branch]
    a = np.arange(B_SUB)[:, None]
    c = np.arange(B_KEYS)[None, :]
    rel = (c - B_HALF) - a
    bias = -slopes[:, None, None] * (np.abs(rel) * dil).astype(np.float64)[None]
    bias = np.where((np.abs(rel) <= B_HALF)[None], bias, NEG)
    return jnp.asarray(bias, F32)


def _dilated_branch(zb, branch, batch, seq):
    _, dil = B_PATTERNS[branch]
    t = batch * seq
    seq_len = seq // dil
    c = TOKEN_TILE // dil
    n_tiles = seq // TOKEN_TILE
    cq = min(B_CHUNK, seq_len)
    nchunk = seq_len // cq
    n_halo = seq_len // B_HALF
    per = cq // B_HALF
    rb = min(dil, max(1, B_CHUNK // cq))
    zv = zb.reshape(B_BRANCHES, batch, n_tiles, dil, c, 3 * B_W)

    def halo_spec(pos):
        if c >= B_HALF:
            sub = c // B_HALF
            z6 = zb.reshape(B_BRANCHES, batch, n_tiles, dil, sub, B_HALF, 3 * B_W)
            return z6, pl.BlockSpec((None, None, None, rb, None, B_HALF, 3 * B_W),
                                    lambda b, r, n: (branch, b, pos(n) // sub, r, pos(n) % sub, 0, 0))
        return zv, pl.BlockSpec((None, None, B_HALF // c, rb, c, 3 * B_W),
                                lambda b, r, n: (branch, b, pos(n), r, 0, 0))

    prev_arr, prev_spec = halo_spec(lambda n: jnp.maximum(n * per - 1, 0))
    next_arr, next_spec = halo_spec(lambda n: jnp.minimum((n + 1) * per, n_halo - 1))
    res = pl.pallas_call(
        functools.partial(_dilated_kernel, seq_len=seq_len, cq=cq),
        grid=(batch, dil // rb, nchunk),
        in_specs=[pl.BlockSpec((None, None, cq // c, rb, c, 3 * B_W), lambda b, r, n: (branch, b, n, r, 0, 0)),
                  prev_spec, next_spec, _const_spec((B_HEADS, B_SUB, B_KEYS))],
        out_specs=pl.BlockSpec((None, cq // c, rb, c, B_R), lambda b, r, n: (b, n, r, 0, 0)),
        out_shape=jax.ShapeDtypeStruct((batch, n_tiles, dil, c, B_R), BF16),
        scratch_shapes=[pltpu.VMEM((cq + 2 * B_HALF, 2 * B_W), BF16)],
        compiler_params=_cparams(3),
        name=f"dilated_branch{branch}",
    )(zv, prev_arr, next_arr, _dilated_bias(branch))
    return res.reshape(t, B_R)


def _merge_branches(rs, permT_ref):
    pair_w = 2 * HEAD_DIM
    lo = lax.broadcasted_iota(jnp.int32, (1, pair_w), 1) < HEAD_DIM
    tiles = [rs[0].astype(F32)] + [_dot(permT_ref[g - 1], rs[g]) for g in range(1, B_BRANCHES)]
    lses = [tl[:, B_W:B_W + pair_w] + tl[:, B_W + pair_w:] for tl in tiles]
    mx = functools.reduce(jnp.maximum, lses)
    es = [jnp.exp(l - mx) for l in lses]
    inv = 1.0 / functools.reduce(lambda u, v: u + v, es)
    pairs = []
    for pr in range(B_HEADS // 2):
        acc = None
        for tl, e in zip(tiles, es):
            w = e * inv
            w2 = jnp.where(lo, w[:, 2 * pr * LSE_REP:2 * pr * LSE_REP + 1],
                           w[:, (2 * pr + 1) * LSE_REP:(2 * pr + 1) * LSE_REP + 1])
            term = w2 * tl[:, pr * pair_w:(pr + 1) * pair_w]
            acc = term if acc is None else acc + term
        pairs.append(acc)
    return jnp.concatenate(pairs, axis=-1).astype(BF16)


def _inproj_kernel(x_ref, g_ref, w_ref, z_ref, stats_ref):
    hb = _rms(x_ref[...], g_ref[...]).astype(BF16)
    z = _dot(hb, w_ref[...])
    z_ref[...] = z.astype(BF16)
    pair_w = 2 * HEAD_DIM
    lane = lax.broadcasted_iota(jnp.int32, (1, pair_w), 1)
    lo = lane < HEAD_DIM
    out = jnp.zeros((1, pair_w), F32)
    for b in range(2 * C_W // pair_w):
        sq = z[:, b * pair_w:(b + 1) * pair_w]
        sq = sq * sq
        tot = jnp.sum(sq, axis=-1, keepdims=True)
        first = jnp.sum(jnp.where(lo, sq, 0.0), axis=-1, keepdims=True)
        m_first = jnp.max(first, axis=0, keepdims=True)
        m_second = jnp.max(tot - first, axis=0, keepdims=True)
        out = jnp.where(lane == 2 * b, m_first, jnp.where(lane == 2 * b + 1, m_second, out))
    stats_ref[...] = jnp.broadcast_to(out, stats_ref.shape)


def _inproj(x, g, w):
    t = x.shape[0]
    tm = TOKEN_TILE
    n_out = w.shape[1]
    return pl.pallas_call(
        _inproj_kernel,
        grid=(t // tm,),
        in_specs=[pl.BlockSpec((tm, D_MODEL), lambda i: (i, 0)),
                  _const_spec((1, D_MODEL)),
                  _const_spec((D_MODEL, n_out))],
        out_specs=[pl.BlockSpec((tm, n_out), lambda i: (i, 0)),
                   pl.BlockSpec((None, 8, 2 * HEAD_DIM), lambda i: (i, 0, 0))],
        out_shape=[jax.ShapeDtypeStruct((t, n_out), BF16),
                   jax.ShapeDtypeStruct((t // tm, 8, 2 * HEAD_DIM), F32)],
        compiler_params=_cparams(1),
        name="inproj_c",
    )(x, g, w)


def _natten_kernel(q_ref, k0_ref, k1_ref, k2_ref, v0_ref, v1_ref, v2_ref, bias_ref, o_ref):
    k_refs = (k0_ref, k1_ref, k2_ref)
    v_refs = (v0_ref, v1_ref, v2_ref)
    lane = lax.broadcasted_iota(jnp.int32, (1, 2 * HEAD_DIM), 1)
    lo = lane < HEAD_DIM
    kb = k0_ref.shape[0]
    for pr in range(C_HEADS // 2):
        cols = slice(pr * 2 * HEAD_DIM, (pr + 1) * 2 * HEAD_DIM)
        q2 = q_ref[:, cols]
        ks = [r[:, cols] for r in k_refs]
        vs = [r[:, cols] for r in v_refs]
        o_pair = []
        for hh in range(2):
            qm = jnp.where(lo if hh == 0 else ~lo, q2, jnp.zeros_like(q2))
            s = jnp.concatenate([_dot_nt(qm, kk) for kk in ks], axis=-1) + bias_ref[pr * 2 + hh]
            m = jnp.max(s, axis=-1, keepdims=True)
            p = jnp.exp(s - m)
            l = jnp.sum(p, axis=-1, keepdims=True)
            pb = p.astype(BF16)
            o = _dot(pb[:, :kb], vs[0])
            for i in range(1, len(vs)):
                o = o + _dot(pb[:, i * kb:(i + 1) * kb], vs[i])
            o_pair.append(o / l)
        o_ref[:, cols] = jnp.where(lo, o_pair[0], o_pair[1]).astype(BF16)


def _natten_bounded_kernel(q_ref, k0_ref, k1_ref, k2_ref, v0_ref, v1_ref, v2_ref, bias_ref, shift_ref, o_ref):
    k_refs = (k0_ref, k1_ref, k2_ref)
    v_refs = (v0_ref, v1_ref, v2_ref)
    pair_w = 2 * HEAD_DIM
    lo = lax.broadcasted_iota(jnp.int32, (1, pair_w), 1) < HEAD_DIM
    kb = k0_ref.shape[0]
    one = jnp.ones((), BF16)
    for pr in range(C_HEADS // 2):
        cols = slice(pr * pair_w, (pr + 1) * pair_w)
        q2 = q_ref[:, cols]
        ks = [r[:, cols] for r in k_refs]
        vs = [r[:, cols] for r in v_refs]
        o_pair = []
        for hh in range(2):
            own = lo if hh == 0 else ~lo
            qm = jnp.where(own, q2, jnp.zeros_like(q2))
            head = pr * 2 + hh
            p = jnp.exp(jnp.concatenate([_dot_nt(qm, kk) for kk in ks], axis=-1)
                        + (bias_ref[head] - shift_ref[head:head + 1, :])).astype(BF16)
            r = None
            for i in range(len(vs)):
                t = _dot(p[:, i * kb:(i + 1) * kb], jnp.where(own, vs[i], one))
                r = t if r is None else r + t
            l = r[:, HEAD_DIM:HEAD_DIM + 1] if hh == 0 else r[:, 0:1]
            o_pair.append(r / l)
        o_ref[:, cols] = jnp.where(lo, o_pair[0], o_pair[1]).astype(BF16)


def _natten_bias(rpb):
    c = np.arange(GRID_W)[:, None]
    kc = np.arange(GRID_W)[None, :]
    cstart = np.clip(c - C_WIN_COLS // 2, 0, GRID_W - C_WIN_COLS)
    col_valid = (kc >= cstart) & (kc < cstart + C_WIN_COLS)
    padded = jnp.pad(rpb.astype(F32), ((0, 0), (0, 0), (GRID_W, GRID_W)))
    base = GRID_W + C_WIN_COLS - 1
    blocks = jnp.stack([padded[:, :, base - q:base - q + GRID_W] for q in range(GRID_W)], axis=2)
    blocks = jnp.where(col_valid[None, None], blocks, NEG)
    masked = jnp.full((rpb.shape[0], GRID_W, GRID_W), NEG, F32)
    tables = []
    for off, first in ((0, lambda dr: 0), (-4, lambda dr: dr), (-8, lambda dr: 4)):
        rows = []
        for dr in range(C_ROWS_Q):
            rows.append(jnp.concatenate(
                [blocks[:, off + kri - dr + C_WIN_ROWS - 1]
                 if first(dr) <= kri < first(dr) + C_WIN_ROWS else masked
                 for kri in range(C_ROWS_K)], axis=-1))
        tables.append(jnp.concatenate(rows, axis=1))
    return jnp.stack(tables)


def _natten(z, stats, rpb, bias, batch, seq):
    t = batch * seq
    qb = C_ROWS_Q * GRID_W
    nrb = seq // qb
    nkb = C_ROWS_K // C_ROWS_Q

    def kspec(i, col):
        return pl.BlockSpec(
            (qb, C_W), lambda rb, b: (b * nrb + jnp.clip(rb - 1, 0, nrb - nkb) + i, col))

    def variant(rb):
        return jnp.where(rb == 0, 0, jnp.where(rb == nrb - 1, 2, 1))

    def call(body, *extra):
        return pl.pallas_call(
            body,
            grid=(nrb, batch),
            in_specs=[pl.BlockSpec((qb, C_W), lambda rb, b: (b * nrb + rb, 0))]
            + [kspec(i, 1) for i in range(nkb)]
            + [kspec(i, 2) for i in range(nkb)]
            + [pl.BlockSpec((None, C_HEADS, qb, C_ROWS_K * GRID_W), lambda rb, b: (variant(rb), 0, 0, 0))]
            + [_const_spec(e.shape) for e in extra],
            out_specs=pl.BlockSpec((qb, C_W), lambda rb, b: (b * nrb + rb, 0)),
            out_shape=jax.ShapeDtypeStruct((t, C_W), BF16),
            compiler_params=_cparams(2),
            name="natten",
        )(z, z, z, z, z, z, z, bias, *extra)

    norms = jnp.sqrt(jnp.max(stats[:, 0, :2 * C_HEADS], axis=0)) * C_BOUND_SLACK
    qk_bound = norms[:C_HEADS] * norms[C_HEADS:]
    rpb_max = jnp.max(rpb, axis=(1, 2))
    rpb_range = rpb_max - jnp.min(rpb, axis=(1, 2))
    safe = jnp.max(2.0 * qk_bound + rpb_range) <= C_BOUND_LIMIT
    shift = jnp.broadcast_to((qk_bound + rpb_max - C_SHIFT_MARGIN)[:, None], (C_HEADS, C_ROWS_K * GRID_W))
    return lax.cond(safe, lambda: call(_natten_bounded_kernel, shift), lambda: call(_natten_kernel))


def _mem_kv_kernel(mem_ref, g_ref, w_ref, kv_ref):
    mb = _rms(mem_ref[...], g_ref[...]).astype(BF16)
    kv_ref[...] = _dot(mb, w_ref[...]).astype(BF16)


def _mem_kv(mem, g, w):
    batch, n_mem, _ = mem.shape
    return pl.pallas_call(
        _mem_kv_kernel,
        grid=(batch,),
        in_specs=[pl.BlockSpec((None, n_mem, D_MODEL), lambda b: (b, 0, 0)),
                  _const_spec((1, D_MODEL)),
                  _const_spec((D_MODEL, 2 * D_MODEL))],
        out_specs=pl.BlockSpec((None, n_mem, 2 * D_MODEL), lambda b: (b, 0, 0)),
        out_shape=jax.ShapeDtypeStruct((batch, n_mem, 2 * D_MODEL), BF16),
        compiler_params=_cparams(1),
        name="mem_kv",
    )(mem, g, w)


def _xattn_tail(x, g_ref, wq_ref, kv_ref, wo_ref, y_ref):
    q = _dot(_rms(x, g_ref[...]).astype(BF16), wq_ref[...]).astype(BF16)
    outs = []
    for h in range(X_HEADS):
        cols = slice(h * X_HEAD_DIM, (h + 1) * X_HEAD_DIM)
        s = _dot_nt(q[:, cols], kv_ref[:, cols])
        m = jnp.max(s, axis=-1, keepdims=True)
        p = jnp.exp(s - m)
        l = jnp.sum(p, axis=-1, keepdims=True)
        o = _dot(p.astype(BF16), kv_ref[:, D_MODEL + h * X_HEAD_DIM:D_MODEL + (h + 1) * X_HEAD_DIM])
        outs.append((o / l).astype(BF16))
    y_ref[...] = x + _dot(jnp.concatenate(outs, axis=-1), wo_ref[...])


def _mix_ab_xattn_kernel(x_ref, oa_ref, r0_ref, r1_ref, r2_ref, permT_ref, woa_ref, wob_ref,
                         g_ref, wq_ref, kv_ref, wo_ref, y_ref):
    ob = _merge_branches([r0_ref[...], r1_ref[...], r2_ref[...]], permT_ref)
    x = x_ref[...] + _dot(oa_ref[...], woa_ref[...]) + _dot(ob, wob_ref[...])
    _xattn_tail(x, g_ref, wq_ref, kv_ref, wo_ref, y_ref)


def _mix_c_xattn_kernel(x_ref, oc_ref, woc_ref, g_ref, wq_ref, kv_ref, wo_ref, y_ref):
    x = x_ref[...] + _dot(oc_ref[...], woc_ref[...])
    _xattn_tail(x, g_ref, wq_ref, kv_ref, wo_ref, y_ref)


def _mix_xattn(body, x, token_args, const_args, g, wq, kv, wo, seq):
    t = x.shape[0]
    tm = TOKEN_TILE
    n_seq_tiles = seq // tm
    n_mem = kv.shape[1]
    token_spec = lambda a: pl.BlockSpec((tm, a.shape[1]), lambda i: (i, 0))
    in_specs = ([token_spec(x)] + [token_spec(a) for a in token_args]
                + [_const_spec(a.shape) for a in const_args]
                + [_const_spec((1, D_MODEL)),
                   _const_spec((D_MODEL, D_MODEL)),
                   pl.BlockSpec((None, n_mem, 2 * D_MODEL), lambda i: (i // n_seq_tiles, 0, 0)),
                   _const_spec((D_MODEL, D_MODEL))])
    return pl.pallas_call(
        body,
        grid=(t // tm,),
        in_specs=in_specs,
        out_specs=pl.BlockSpec((tm, D_MODEL), lambda i: (i, 0)),
        out_shape=jax.ShapeDtypeStruct((t, D_MODEL), F32),
        compiler_params=_cparams(1),
        name="mix_xattn",
    )(x, *token_args, *const_args, g, wq, kv, wo)


def _swiglu_kernel(x_ref, g_ref, wg_ref, wu_ref, wd_ref, gf_ref, y_ref, *, final_norm):
    x = x_ref[...]
    hb = _rms(x, g_ref[...]).astype(BF16)
    gate = _dot(hb, wg_ref[...])
    up = _dot(hb, wu_ref[...])
    act = (gate / (1.0 + jnp.exp(-gate)) * up).astype(BF16)
    y = x + _dot(act, wd_ref[...])
    if final_norm:
        y = _rms(y, gf_ref[...])
    y_ref[...] = y


def _swiglu(x, g, wg, wu, wd, g_final, final_norm, tm=256):
    t = x.shape[0]
    return pl.pallas_call(
        functools.partial(_swiglu_kernel, final_norm=final_norm),
        grid=(t // tm,),
        in_specs=[pl.BlockSpec((tm, D_MODEL), lambda i: (i, 0)),
                  _const_spec((1, D_MODEL)),
                  _const_spec((D_MODEL, D_FF)),
                  _const_spec((D_MODEL, D_FF)),
                  _const_spec((D_FF, D_MODEL)),
                  _const_spec((1, D_MODEL))],
        out_specs=pl.BlockSpec((tm, D_MODEL), lambda i: (i, 0)),
        out_shape=jax.ShapeDtypeStruct((t, D_MODEL), F32),
        compiler_params=_cparams(1),
        name="swiglu",
    )(x, g, wg, wu, wd, g_final)


def _rope_tables(seq):
    tok = jnp.arange(seq, dtype=jnp.int32)
    row = (tok // GRID_W).astype(F32)
    col = (tok % GRID_W).astype(F32)
    axis_dim = HEAD_DIM // 2
    inv_freq = ROPE_THETA ** (-jnp.arange(0, axis_dim, 2, dtype=F32) / axis_dim)
    ang = jnp.concatenate([row[:, None] * inv_freq, col[:, None] * inv_freq], axis=-1)
    return jnp.cos(ang).T, jnp.sin(ang).T


def _prepare_ab(w_in, g_qn, g_kn, w_out):
    perm = np.concatenate([np.arange(0, HEAD_DIM, 2), np.arange(1, HEAD_DIM, 2)])
    cols = np.concatenate(
        [h * HEAD_DIM + perm for h in range(A_Q_HEADS + A_KV_HEADS)]
        + [np.arange(A_Q_W + A_KV_W, A_W)])
    waT = w_in[:, cols].T.astype(BF16)
    zbw = w_in[:, A_W:].reshape(D_MODEL, 3, B_BRANCHES, B_W)
    qscale = HEAD_DIM ** -0.5
    wb = jnp.stack([jnp.concatenate([zbw[:, 0, g] * qscale, zbw[:, 1, g], zbw[:, 2, g]], axis=1)
                    for g in range(B_BRANCHES)]).astype(BF16)
    gq = (g_qn[perm] * (qscale * LOG2E)).reshape(HEAD_DIM, 1).astype(F32)
    gk = g_kn[perm].reshape(HEAD_DIM, 1).astype(F32)
    perms = np.stack([_tile_perm(dil) for _, dil in B_PATTERNS[1:]])
    tile_perm = jnp.asarray(perms, BF16)
    tile_perm_t = jnp.asarray(perms.transpose(0, 2, 1), BF16)
    return waT, wb, tile_perm, tile_perm_t, gq, gk, w_out[:A_Q_W].astype(BF16), w_out[A_Q_W:].astype(BF16)


def _prepare_layers(p):
    depth = p["g_mix"].shape[0]
    row = lambda g: g.reshape(1, D_MODEL)
    layers = []
    for l in range(depth):
        lay = dict(g_mix=row(p["g_mix"][l]), g_xattn=row(p["g_xattn"][l]), g_mem=row(p["g_mem"][l]),
                   g_ffn=row(p["g_ffn"][l]),
                   wq=(p["wq_x"][l] * X_HEAD_DIM ** -0.5).astype(BF16), wkv=p["wkv_x"][l].astype(BF16),
                   wo=p["wo_x"][l].astype(BF16), wg=p["w_gu"][l][:, :D_FF].astype(BF16),
                   wu=p["w_gu"][l][:, D_FF:].astype(BF16), wd=p["w_down"][l].astype(BF16))
        i = l // 2
        if l % 2 == 0:
            lay["ab"] = _prepare_ab(p["w_in_ab"][i], p["g_qn"][i], p["g_kn"][i], p["w_out_ab"][i])
        else:
            w_in = p["w_in_c"][i]
            lay["w_in_c"] = jnp.concatenate(
                [w_in[:, :C_W] * HEAD_DIM ** -0.5, w_in[:, C_W:]], axis=1).astype(BF16)
            lay["rpb_c"] = p["rpb_c"][i]
            lay["bias_c"] = _natten_bias(p["rpb_c"][i])
            lay["w_out_c"] = p["w_out_c"][i].astype(BF16)
        layers.append(lay)
    return layers


def _trunk(x, mem, layers, g_final):
    batch, seq, _ = x.shape
    x = x.reshape(batch * seq, D_MODEL)
    for l, lay in enumerate(layers):
        if l % 2 == 0:
            waT, wb, perm, permT, gq, gk, wo_a, wo_b = lay["ab"]
            cosT, sinT = _rope_tables(seq)
            qT, k, vT, stats, zb = _inproj_ab(x, lay["g_mix"], waT, wb, perm, gq, gk, cosT, sinT, seq)
            oa = _gqa_attention(qT, k, vT, stats, batch, seq)
            rs = [_dilated_branch(zb, g, batch, seq) for g in range(B_BRANCHES)]
            body, token_args, const_args = _mix_ab_xattn_kernel, [oa] + rs, [permT, wo_a, wo_b]
        else:
            z, stats = _inproj(x, lay["g_mix"], lay["w_in_c"])
            oc = _natten(z, stats, lay["rpb_c"], lay["bias_c"], batch, seq)
            body, token_args, const_args = _mix_c_xattn_kernel, [oc], [lay["w_out_c"]]
        kv = _mem_kv(mem, lay["g_mem"], lay["wkv"])
        x = _mix_xattn(body, x, token_args, const_args, lay["g_xattn"], lay["wq"], kv, lay["wo"], seq)
        x = _swiglu(x, lay["g_ffn"], lay["wg"], lay["wu"], lay["wd"], g_final,
                    final_norm=(l == len(layers) - 1))
    return x.reshape(batch, seq, D_MODEL)


def kernel(x_prompt, x_sample, mem_prompt, mem_sample, g_mix, w_in_ab, g_qn, g_kn, w_out_ab, w_in_c,
           rpb_c, w_out_c, g_xattn, g_mem, wq_x, wkv_x, wo_x, g_ffn, w_gu, w_down, g_final):
    layers = _prepare_layers(dict(
        g_mix=g_mix, w_in_ab=w_in_ab, g_qn=g_qn, g_kn=g_kn, w_out_ab=w_out_ab, w_in_c=w_in_c,
        rpb_c=rpb_c, w_out_c=w_out_c, g_xattn=g_xattn, g_mem=g_mem, wq_x=wq_x, wkv_x=wkv_x,
        wo_x=wo_x, g_ffn=g_ffn, w_gu=w_gu, w_down=w_down))
    g_final = g_final.reshape(1, D_MODEL)
    return (_trunk(x_prompt, mem_prompt, layers, g_final), _trunk(x_sample, mem_sample, layers, g_final))
```

```python
import functools
import math

import numpy as np
import jax
import jax.numpy as jnp
from jax import lax
from jax.experimental import pallas as pl
from jax.experimental.pallas import tpu as pltpu

F32 = jnp.float32
BF16 = jnp.bfloat16

D_MODEL = 1024
GRID_W = 64
HEAD_DIM = 64
EPS = 1e-6
NEG = -1e30
LOG2E = math.log2(math.e)
TOKEN_TILE = 512

A_Q_HEADS = 8
A_KV_HEADS = 2
A_GROUP = A_Q_HEADS // A_KV_HEADS
ROPE_THETA = 10000.0
A_Q_W = A_Q_HEADS * HEAD_DIM
A_KV_W = A_KV_HEADS * HEAD_DIM
A_W = A_Q_W + 2 * A_KV_W
A_TQ = 1024
A_TK = 256
A_V_ROWS = HEAD_DIM + 16
A_UNROLL = 6
A_UNROLL_BOUNDED = 16
A_SHIFT_MARGIN = 60.0
A_BOUND_LIMIT = 90.0
A_BOUND_SLACK = 1.0 + 2.0 ** -6

B_PATTERNS = ((128, 1), (512, 4), (2048, 16))
B_BRANCHES = len(B_PATTERNS)
B_HEADS = 4
B_HALF = 64
B_W = B_HEADS * HEAD_DIM
B_SUB = 2 * B_HALF
B_KEYS = 4 * B_HALF
B_CHUNK = 1024
LSE_REP = 2 * HEAD_DIM // B_HEADS
B_R = B_W + 4 * HEAD_DIM

C_HEADS = 16
C_WIN_ROWS = 8
C_WIN_COLS = 16
C_ROWS_Q = 4
C_ROWS_K = 12
C_W = C_HEADS * HEAD_DIM
C_SHIFT_MARGIN = 40.0
C_BOUND_LIMIT = 120.0
C_BOUND_SLACK = 1.0 + 2.0 ** -6

X_HEADS = 4
X_HEAD_DIM = D_MODEL // X_HEADS
D_FF = 2816
FFN_SPLIT = 2

VMEM_LIMIT = 56 * 1024 * 1024


def _cparams(n_axes):
    return pltpu.CompilerParams(dimension_semantics=("arbitrary",) * n_axes,
                                vmem_limit_bytes=VMEM_LIMIT)


def _rms(x, g):
    ms = jnp.mean(x * x, axis=-1, keepdims=True)
    return x * lax.rsqrt(ms + EPS) * g


def _dot(a, b):
    return jnp.dot(a, b, preferred_element_type=F32)


def _dot_nt(a, b):
    return lax.dot_general(a, b, (((1,), (1,)), ((), ())), preferred_element_type=F32)


def _const_spec(shape):
    zeros = (0,) * len(shape)
    return pl.BlockSpec(shape, lambda *_: zeros)


def _inproj_ab_kernel(x_ref, g_ref, waT_ref, wb_ref, perm_ref, gq_ref, gk_ref, cos_ref, sin_ref,
                      qT_ref, k_ref, vT_ref, stats_ref, zb_ref):
    hb = _rms(x_ref[...], g_ref[...]).astype(BF16)
    zT = _dot_nt(waT_ref[...], hb)
    cos = cos_ref[...]
    sin = sin_ref[...]
    half = HEAD_DIM // 2

    def norm_rope(zh, gcol):
        ms = jnp.mean(zh * zh, axis=0, keepdims=True)
        y = zh * lax.rsqrt(ms + EPS) * gcol
        yr, yi = y[:half], y[half:]
        return jnp.concatenate([yr * cos - yi * sin, yr * sin + yi * cos], axis=0)

    def l2(v):
        return jnp.sqrt(jnp.sum(v * v, axis=0, keepdims=True))

    stats_ref[...] = jnp.zeros(stats_ref.shape, F32)
    for h in range(A_Q_HEADS):
        sl = slice(h * HEAD_DIM, (h + 1) * HEAD_DIM)
        q = norm_rope(zT[sl], gq_ref[...])
        qT_ref[sl, :] = q.astype(BF16)
        stats_ref[h // A_GROUP, h % A_GROUP:h % A_GROUP + 1, :] = l2(q)
    ks = [norm_rope(zT[A_Q_W + h * HEAD_DIM:A_Q_W + (h + 1) * HEAD_DIM], gk_ref[...])
          for h in range(A_KV_HEADS)]
    for h in range(A_KV_HEADS):
        stats_ref[h, A_GROUP:A_GROUP + 1, :] = l2(ks[h])
    kT = jnp.concatenate(ks, axis=0)
    k_nat = kT.T
    for h in range(A_KV_HEADS):
        k_ref[h] = k_nat[:, h * HEAD_DIM:(h + 1) * HEAD_DIM].astype(BF16)
    for h in range(A_KV_HEADS):
        for c in range(vT_ref.shape[1]):
            vT_ref[h, c] = zT[A_Q_W + A_KV_W + h * HEAD_DIM:A_Q_W + A_KV_W + (h + 1) * HEAD_DIM,
                              c * A_TK:(c + 1) * A_TK].astype(BF16)
    zb_ref[0] = _dot(hb, wb_ref[0]).astype(BF16)
    for g in range(1, B_BRANCHES):
        zb_ref[g] = _dot(perm_ref[g - 1], _dot(hb, wb_ref[g]).astype(BF16)).astype(BF16)


def _inproj_ab(x, g_mix, waT, wb, perm, gq, gk, cosT, sinT, seq):
    t = x.shape[0]
    tm = TOKEN_TILE
    n_seq_tiles = seq // tm
    return pl.pallas_call(
        _inproj_ab_kernel,
        grid=(t // tm,),
        in_specs=[
            pl.BlockSpec((tm, D_MODEL), lambda i: (i, 0)),
            _const_spec((1, D_MODEL)),
            _const_spec((A_W, D_MODEL)),
            _const_spec((B_BRANCHES, D_MODEL, 3 * B_W)),
            _const_spec((B_BRANCHES - 1, tm, tm)),
            _const_spec((HEAD_DIM, 1)),
            _const_spec((HEAD_DIM, 1)),
            pl.BlockSpec((HEAD_DIM // 2, tm), lambda i: (0, i % n_seq_tiles)),
            pl.BlockSpec((HEAD_DIM // 2, tm), lambda i: (0, i % n_seq_tiles)),
        ],
        out_specs=[
            pl.BlockSpec((A_Q_W, tm), lambda i: (0, i)),
            pl.BlockSpec((A_KV_HEADS, tm, HEAD_DIM), lambda i: (0, i, 0)),
            pl.BlockSpec((A_KV_HEADS, tm // A_TK, HEAD_DIM, A_TK), lambda i: (0, i, 0, 0)),
            pl.BlockSpec((A_KV_HEADS, 8, tm), lambda i: (0, 0, i)),
            pl.BlockSpec((B_BRANCHES, tm, 3 * B_W), lambda i: (0, i, 0)),
        ],
        out_shape=[
            jax.ShapeDtypeStruct((A_Q_W, t), BF16),
            jax.ShapeDtypeStruct((A_KV_HEADS, t, HEAD_DIM), BF16),
            jax.ShapeDtypeStruct((A_KV_HEADS, t // A_TK, HEAD_DIM, A_TK), BF16),
            jax.ShapeDtypeStruct((A_KV_HEADS, 8, t), F32),
            jax.ShapeDtypeStruct((B_BRANCHES, t, 3 * B_W), BF16),
        ],
        compiler_params=_cparams(1),
        name="inproj_ab",
    )(x, g_mix, waT, wb, perm, gq, gk, cosT, sinT)


def _gqa_kernel(qT_ref, k_ref, vT_ref, o_ref, s_scr, p_scr, cm_scr, alpha_scr, m_scr, acc_scr):
    tq = qT_ref.shape[1]
    tk = A_TK
    n_chunks = vT_ref.shape[0]
    ones = jnp.ones((A_V_ROWS - HEAD_DIM, tk), BF16)

    def scores(chunk, slot):
        k = k_ref[pl.ds(pl.multiple_of(chunk * tk, tk), tk), :]
        for h in range(A_GROUP):
            s = _dot(k, qT_ref[h * HEAD_DIM:(h + 1) * HEAD_DIM, :])
            s_scr[slot, h] = s
            cm_scr[slot, h] = jnp.max(s.reshape(tk // 8, 8, tq), axis=0)

    def probs(slot):
        for h in range(A_GROUP):
            m_old = m_scr[h]
            m_new = jnp.maximum(m_old, jnp.max(cm_scr[slot, h], axis=0, keepdims=True))
            alpha_scr[slot, h] = jnp.exp2(m_old - m_new)
            m_scr[h] = m_new
            p_scr[slot, h] = jnp.exp2(s_scr[slot, h] - m_new).astype(BF16)

    def accumulate(chunk, slot):
        vT_aug = jnp.concatenate([vT_ref[chunk], ones], axis=0)
        for h in range(A_GROUP):
            acc_scr[h] = alpha_scr[slot, h] * acc_scr[h] + _dot(vT_aug, p_scr[slot, h])

    def steady(c, count):
        for u in range(count):
            accumulate(c + u, u % 2)
            probs((u + 1) % 2)
            scores(c + u + 2, u % 2)

    m_scr[...] = jnp.full(m_scr.shape, NEG, F32)
    acc_scr[...] = jnp.zeros(acc_scr.shape, F32)
    scores(0, 0)
    scores(1, 1)
    probs(0)
    n_loop = (n_chunks - 2) // A_UNROLL

    def body(i, carry):
        steady(i * A_UNROLL, A_UNROLL)
        return carry

    if n_loop > 1:
        lax.fori_loop(0, n_loop, body, 0)
    else:
        steady(0, n_loop * A_UNROLL)
    steady(n_loop * A_UNROLL, (n_chunks - 2) % A_UNROLL)
    accumulate(n_chunks - 2, 0)
    probs(1)
    accumulate(n_chunks - 1, 1)
    oT = jnp.concatenate(
        [acc_scr[h, :HEAD_DIM, :] / acc_scr[h, HEAD_DIM:HEAD_DIM + 1, :] for h in range(A_GROUP)], axis=0)
    o_ref[...] = oT.T.astype(BF16)


def _gqa_bounded_kernel(qT_ref, k_ref, vT_ref, shift_ref, o_ref, p_scr, acc_scr):
    tq = qT_ref.shape[1]
    tk = A_TK
    n_chunks = vT_ref.shape[0]
    ones = jnp.ones((A_V_ROWS - HEAD_DIM, tk), BF16)

    def probs(chunk, slot):
        k = k_ref[pl.ds(pl.multiple_of(chunk * tk, tk), tk), :]
        for h in range(A_GROUP):
            s = _dot(k, qT_ref[h * HEAD_DIM:(h + 1) * HEAD_DIM, :])
            p_scr[slot, h] = jnp.exp2(s - shift_ref[h:h + 1, :]).astype(BF16)

    def accumulate(chunk, slot):
        vT_aug = jnp.concatenate([vT_ref[chunk], ones], axis=0)
        for h in range(A_GROUP):
            acc_scr[h] += _dot(vT_aug, p_scr[slot, h])

    def steady(c, count):
        for u in range(count):
            accumulate(c + u, u % 2)
            probs(c + u + 2, u % 2)

    acc_scr[...] = jnp.zeros(acc_scr.shape, F32)
    probs(0, 0)
    probs(1, 1)
    n_loop = (n_chunks - 2) // A_UNROLL_BOUNDED

    def body(i, carry):
        steady(i * A_UNROLL_BOUNDED, A_UNROLL_BOUNDED)
        return carry

    if n_loop > 1:
        lax.fori_loop(0, n_loop, body, 0)
    else:
        steady(0, n_loop * A_UNROLL_BOUNDED)
    steady(n_loop * A_UNROLL_BOUNDED, (n_chunks - 2) % A_UNROLL_BOUNDED)
    accumulate(n_chunks - 2, 0)
    accumulate(n_chunks - 1, 1)
    oT = jnp.concatenate(
        [acc_scr[h, :HEAD_DIM, :] / acc_scr[h, HEAD_DIM:HEAD_DIM + 1, :] for h in range(A_GROUP)], axis=0)
    o_ref[...] = oT.T.astype(BF16)


def _gqa_attention(qT, k, vT, stats, batch, seq, tq=A_TQ):
    t = batch * seq
    nq = seq // tq
    k4 = k.reshape(A_KV_HEADS, batch, seq, HEAD_DIM)
    v5 = vT.reshape(A_KV_HEADS, batch, seq // A_TK, HEAD_DIM, A_TK)
    kmax = jnp.max(stats[:, A_GROUP].reshape(A_KV_HEADS, batch, seq), axis=-1)
    bound = (stats.reshape(A_KV_HEADS, 8, batch, seq) * kmax[:, None, :, None]
             * A_BOUND_SLACK).reshape(A_KV_HEADS, 8, t)
    in_specs = [
        pl.BlockSpec((A_GROUP * HEAD_DIM, tq), lambda b, g, i: (g, b * nq + i)),
        pl.BlockSpec((None, None, seq, HEAD_DIM), lambda b, g, i: (g, b, 0, 0)),
        pl.BlockSpec((None, None, seq // A_TK, HEAD_DIM, A_TK), lambda b, g, i: (g, b, 0, 0, 0)),
    ]
    common = dict(
        grid=(batch, A_KV_HEADS, nq),
        out_specs=pl.BlockSpec((tq, A_GROUP * HEAD_DIM), lambda b, g, i: (b * nq + i, g)),
        out_shape=jax.ShapeDtypeStruct((t, A_Q_W), BF16),
        compiler_params=_cparams(3),
    )

    def bounded():
        return pl.pallas_call(
            _gqa_bounded_kernel,
            in_specs=in_specs + [pl.BlockSpec((None, 8, tq), lambda b, g, i: (g, 0, b * nq + i))],
            scratch_shapes=[pltpu.VMEM((2, A_GROUP, A_TK, tq), BF16),
                            pltpu.VMEM((A_GROUP, A_V_ROWS, tq), F32)],
            name="gqa_bounded", **common,
        )(qT, k4, v5, bound - A_SHIFT_MARGIN)

    def online():
        return pl.pallas_call(
            _gqa_kernel,
            in_specs=in_specs,
            scratch_shapes=[pltpu.VMEM((2, A_GROUP, A_TK, tq), F32),
                            pltpu.VMEM((2, A_GROUP, A_TK, tq), BF16),
                            pltpu.VMEM((2, A_GROUP, 8, tq), F32),
                            pltpu.VMEM((2, A_GROUP, 1, tq), F32),
                            pltpu.VMEM((A_GROUP, 1, tq), F32),
                            pltpu.VMEM((A_GROUP, A_V_ROWS, tq), F32)],
            name="gqa_attention", **common,
        )(qT, k4, v5)

    safe = jnp.max(bound[:, :A_GROUP]) <= A_BOUND_LIMIT
    return lax.cond(safe, bounded, online)


def _tile_perm(dil):
    c = TOKEN_TILE // dil
    dst = np.arange(TOKEN_TILE)
    src = (dst % c) * dil + dst // c
    p = np.zeros((TOKEN_TILE, TOKEN_TILE), np.float32)
    p[dst, src] = 1.0
    return p


def _seq_rows(ref, start, n, cols):
    c = ref.shape[1]
    if n <= c:
        return ref[start // c, start % c:start % c + n, cols]
    blocks = ref[start // c:(start + n) // c, :, cols]
    return blocks.reshape(n, blocks.shape[-1])


def _store_seq_rows(ref, start, cols, val):
    c = ref.shape[1]
    n = val.shape[0]
    if n <= c:
        ref[start // c, start % c:start % c + n, cols] = val
    else:
        ref[start // c:(start + n) // c, :, cols] = val.reshape(n // c, c, val.shape[-1])


def _dilated_kernel(main_ref, prev_ref, next_ref, bias_ref, r_ref, kv_scr, *, seq_len, cq):
    n = pl.program_id(2)
    kv_cols = slice(B_W, 3 * B_W)
    pair_w = 2 * HEAD_DIM
    lane = lax.broadcasted_iota(jnp.int32, (1, pair_w), 1)
    lo = lane < HEAD_DIM
    key_col = lax.broadcasted_iota(jnp.int32, (1, B_KEYS), 1)

    for rr in range(main_ref.shape[1]):
        main, res = main_ref.at[:, rr], r_ref.at[:, rr]
        kv_scr[0:B_HALF, :] = prev_ref[..., rr, :, kv_cols].reshape(B_HALF, 2 * B_W)
        for j in range(cq // B_SUB):
            kv_scr[B_HALF + j * B_SUB:B_HALF + (j + 1) * B_SUB, :] = _seq_rows(main, j * B_SUB, B_SUB, kv_cols)
        kv_scr[B_HALF + cq:, :] = next_ref[..., rr, :, kv_cols].reshape(B_HALF, 2 * B_W)

        for j in range(cq // B_SUB):
            key_pos = key_col + (n * cq + j * B_SUB - B_HALF)
            edge = jnp.where((key_pos >= 0) & (key_pos < seq_len), 0.0, NEG).astype(F32)
            krows = slice(j * B_SUB, j * B_SUB + B_KEYS)
            lses = []
            for pr in range(B_HEADS // 2):
                cols = slice(pr * pair_w, (pr + 1) * pair_w)
                q2 = _seq_rows(main, j * B_SUB, B_SUB, cols)
                k2 = kv_scr[krows, cols]
                v2 = kv_scr[krows, B_W + pr * pair_w:B_W + (pr + 1) * pair_w]
                o_pair = []
                for hh in range(2):
                    qm = jnp.where(lo if hh == 0 else ~lo, q2, jnp.zeros_like(q2))
                    s = _dot_nt(qm, k2) + bias_ref[pr * 2 + hh] + edge
                    m = jnp.max(s, axis=-1, keepdims=True)
                    p = jnp.exp(s - m)
                    l = jnp.sum(p, axis=-1, keepdims=True)
                    o_pair.append(_dot(p.astype(BF16), v2) / l)
                    lses.append(m + jnp.log(l))
                _store_seq_rows(res, j * B_SUB, cols, jnp.where(lo, o_pair[0], o_pair[1]).astype(BF16))
            tile = lses[B_HEADS - 1]
            for head in range(B_HEADS - 2, -1, -1):
                tile = jnp.where(lane < (head + 1) * LSE_REP, lses[head], tile)
            hi = tile.astype(BF16)
            _store_seq_rows(res, j * B_SUB, slice(B_W, B_W + pair_w), hi)
            _store_seq_rows(res, j * B_SUB, slice(B_W + pair_w, B_R),
                            (tile - hi.astype(F32)).astype(BF16))


def _dilated_bias(branch):
    _, dil = B_PATTERNS[branch]
    slopes = np.exp2(-8.0 * np.arange(1, B_BRANCHES * B_HEADS + 1, dtype=np.float64)
                     / (B_BRANCHES * B_HEADS)).reshape(B_BRANCHES, B_HEADS)[branch]
    a = np.arange(B_SUB)[:, None]
    c = np.arange(B_KEYS)[None, :]
    rel = (c - B_HALF) - a
    bias = -slopes[:, None, None] * (np.abs(rel) * dil).astype(np.float64)[None]
    bias = np.where((np.abs(rel) <= B_HALF)[None], bias, NEG)
    return jnp.asarray(bias, F32)


def _dilated_branch(zb, branch, batch, seq):
    _, dil = B_PATTERNS[branch]
    t = batch * seq
    seq_len = seq // dil
    c = TOKEN_TILE // dil
    n_tiles = seq // TOKEN_TILE
    cq = min(B_CHUNK, seq_len)
    nchunk = seq_len // cq
    n_halo = seq_len // B_HALF
    per = cq // B_HALF
    rb = min(dil, max(1, B_CHUNK // cq))
    zv = zb.reshape(B_BRANCHES, batch, n_tiles, dil, c, 3 * B_W)

    def halo_spec(pos):
        if c >= B_HALF:
            sub = c // B_HALF
            z6 = zb.reshape(B_BRANCHES, batch, n_tiles, dil, sub, B_HALF, 3 * B_W)
            return z6, pl.BlockSpec((None, None, None, rb, None, B_HALF, 3 * B_W),
                                    lambda b, r, n: (branch, b, pos(n) // sub, r, pos(n) % sub, 0, 0))
        return zv, pl.BlockSpec((None, None, B_HALF // c, rb, c, 3 * B_W),
                                lambda b, r, n: (branch, b, pos(n), r, 0, 0))

    prev_arr, prev_spec = halo_spec(lambda n: jnp.maximum(n * per - 1, 0))
    next_arr, next_spec = halo_spec(lambda n: jnp.minimum((n + 1) * per, n_halo - 1))
    res = pl.pallas_call(
        functools.partial(_dilated_kernel, seq_len=seq_len, cq=cq),
        grid=(batch, dil // rb, nchunk),
        in_specs=[pl.BlockSpec((None, None, cq // c, rb, c, 3 * B_W), lambda b, r, n: (branch, b, n, r, 0, 0)),
                  prev_spec, next_spec, _const_spec((B_HEADS, B_SUB, B_KEYS))],
        out_specs=pl.BlockSpec((None, cq // c, rb, c, B_R), lambda b, r, n: (b, n, r, 0, 0)),
        out_shape=jax.ShapeDtypeStruct((batch, n_tiles, dil, c, B_R), BF16),
        scratch_shapes=[pltpu.VMEM((cq + 2 * B_HALF, 2 * B_W), BF16)],
        compiler_params=_cparams(3),
        name=f"dilated_branch{branch}",
    )(zv, prev_arr, next_arr, _dilated_bias(branch))
    return res.reshape(t, B_R)


def _merge_branches(rs, permT_ref):
    pair_w = 2 * HEAD_DIM
    lo = lax.broadcasted_iota(jnp.int32, (1, pair_w), 1) < HEAD_DIM
    tiles = [rs[0].astype(F32)] + [_dot(permT_ref[g - 1], rs[g]) for g in range(1, B_BRANCHES)]
    lses = [tl[:, B_W:B_W + pair_w] + tl[:, B_W + pair_w:] for tl in tiles]
    mx = functools.reduce(jnp.maximum, lses)
    es = [jnp.exp(l - mx) for l in lses]
    inv = 1.0 / functools.reduce(lambda u, v: u + v, es)
    pairs = []
    for pr in range(B_HEADS // 2):
        acc = None
        for tl, e in zip(tiles, es):
            w = e * inv
            w2 = jnp.where(lo, w[:, 2 * pr * LSE_REP:2 * pr * LSE_REP + 1],
                           w[:, (2 * pr + 1) * LSE_REP:(2 * pr + 1) * LSE_REP + 1])
            term = w2 * tl[:, pr * pair_w:(pr + 1) * pair_w]
            acc = term if acc is None else acc + term
        pairs.append(acc)
    return jnp.concatenate(pairs, axis=-1).astype(BF16)


def _inproj_kernel(x_ref, g_ref, w_ref, z_ref, stats_ref):
    hb = _rms(x_ref[...], g_ref[...]).astype(BF16)
    z = _dot(hb, w_ref[...])
    z_ref[...] = z.astype(BF16)
    pair_w = 2 * HEAD_DIM
    lane = lax.broadcasted_iota(jnp.int32, (1, pair_w), 1)
    lo = lane < HEAD_DIM
    out = jnp.zeros((1, pair_w), F32)
    for b in range(2 * C_W // pair_w):
        sq = z[:, b * pair_w:(b + 1) * pair_w]
        sq = sq * sq
        tot = jnp.sum(sq, axis=-1, keepdims=True)
        first = jnp.sum(jnp.where(lo, sq, 0.0), axis=-1, keepdims=True)
        m_first = jnp.max(first, axis=0, keepdims=True)
        m_second = jnp.max(tot - first, axis=0, keepdims=True)
        out = jnp.where(lane == 2 * b, m_first, jnp.where(lane == 2 * b + 1, m_second, out))
    stats_ref[...] = jnp.broadcast_to(out, stats_ref.shape)


def _inproj(x, g, w):
    t = x.shape[0]
    tm = TOKEN_TILE
    n_out = w.shape[1]
    return pl.pallas_call(
        _inproj_kernel,
        grid=(t // tm,),
        in_specs=[pl.BlockSpec((tm, D_MODEL), lambda i: (i, 0)),
                  _const_spec((1, D_MODEL)),
                  _const_spec((D_MODEL, n_out))],
        out_specs=[pl.BlockSpec((tm, n_out), lambda i: (i, 0)),
                   pl.BlockSpec((None, 8, 2 * HEAD_DIM), lambda i: (i, 0, 0))],
        out_shape=[jax.ShapeDtypeStruct((t, n_out), BF16),
                   jax.ShapeDtypeStruct((t // tm, 8, 2 * HEAD_DIM), F32)],
        compiler_params=_cparams(1),
        name="inproj_c",
    )(x, g, w)


def _natten_kernel(q_ref, k0_ref, k1_ref, k2_ref, v0_ref, v1_ref, v2_ref, bias_ref, o_ref):
    k_refs = (k0_ref, k1_ref, k2_ref)
    v_refs = (v0_ref, v1_ref, v2_ref)
    lane = lax.broadcasted_iota(jnp.int32, (1, 2 * HEAD_DIM), 1)
    lo = lane < HEAD_DIM
    kb = k0_ref.shape[0]
    for pr in range(C_HEADS // 2):
        cols = slice(pr * 2 * HEAD_DIM, (pr + 1) * 2 * HEAD_DIM)
        q2 = q_ref[:, cols]
        ks = [r[:, cols] for r in k_refs]
        vs = [r[:, cols] for r in v_refs]
        o_pair = []
        for hh in range(2):
            qm = jnp.where(lo if hh == 0 else ~lo, q2, jnp.zeros_like(q2))
            s = jnp.concatenate([_dot_nt(qm, kk) for kk in ks], axis=-1) + bias_ref[pr * 2 + hh]
            m = jnp.max(s, axis=-1, keepdims=True)
            p = jnp.exp(s - m)
            l = jnp.sum(p, axis=-1, keepdims=True)
            pb = p.astype(BF16)
            o = _dot(pb[:, :kb], vs[0])
            for i in range(1, len(vs)):
                o = o + _dot(pb[:, i * kb:(i + 1) * kb], vs[i])
            o_pair.append(o / l)
        o_ref[:, cols] = jnp.where(lo, o_pair[0], o_pair[1]).astype(BF16)


def _natten_bounded_kernel(q_ref, k0_ref, k1_ref, k2_ref, v0_ref, v1_ref, v2_ref, bias_ref, shift_ref, o_ref):
    k_refs = (k0_ref, k1_ref, k2_ref)
    v_refs = (v0_ref, v1_ref, v2_ref)
    pair_w = 2 * HEAD_DIM
    lo = lax.broadcasted_iota(jnp.int32, (1, pair_w), 1) < HEAD_DIM
    kb = k0_ref.shape[0]
    one = jnp.ones((), BF16)
    for pr in range(C_HEADS // 2):
        cols = slice(pr * pair_w, (pr + 1) * pair_w)
        q2 = q_ref[:, cols]
        ks = [r[:, cols] for r in k_refs]
        vs = [r[:, cols] for r in v_refs]
        o_pair = []
        for hh in range(2):
            own = lo if hh == 0 else ~lo
            qm = jnp.where(own, q2, jnp.zeros_like(q2))
            head = pr * 2 + hh
            p = jnp.exp(jnp.concatenate([_dot_nt(qm, kk) for kk in ks], axis=-1)
                        + (bias_ref[head] - shift_ref[head:head + 1, :])).astype(BF16)
            r = None
            for i in range(len(vs)):
                t = _dot(p[:, i * kb:(i + 1) * kb], jnp.where(own, vs[i], one))
                r = t if r is None else r + t
            l = r[:, HEAD_DIM:HEAD_DIM + 1] if hh == 0 else r[:, 0:1]
            o_pair.append(r / l)
        o_ref[:, cols] = jnp.where(lo, o_pair[0], o_pair[1]).astype(BF16)


def _natten_bias(rpb):
    c = np.arange(GRID_W)[:, None]
    kc = np.arange(GRID_W)[None, :]
    cstart = np.clip(c - C_WIN_COLS // 2, 0, GRID_W - C_WIN_COLS)
    col_valid = (kc >= cstart) & (kc < cstart + C_WIN_COLS)
    padded = jnp.pad(rpb.astype(F32), ((0, 0), (0, 0), (GRID_W, GRID_W)))
    base = GRID_W + C_WIN_COLS - 1
    blocks = jnp.stack([padded[:, :, base - q:base - q + GRID_W] for q in range(GRID_W)], axis=2)
    blocks = jnp.where(col_valid[None, None], blocks, NEG)
    masked = jnp.full((rpb.shape[0], GRID_W, GRID_W), NEG, F32)
    tables = []
    for off, first in ((0, lambda dr: 0), (-4, lambda dr: dr), (-8, lambda dr: 4)):
        rows = []
        for dr in range(C_ROWS_Q):
            rows.append(jnp.concatenate(
                [blocks[:, off + kri - dr + C_WIN_ROWS - 1]
                 if first(dr) <= kri < first(dr) + C_WIN_ROWS else masked
                 for kri in range(C_ROWS_K)], axis=-1))
        tables.append(jnp.concatenate(rows, axis=1))
    return jnp.stack(tables)


def _natten(z, stats, rpb, bias, batch, seq):
    t = batch * seq
    qb = C_ROWS_Q * GRID_W
    nrb = seq // qb
    nkb = C_ROWS_K // C_ROWS_Q

    def kspec(i, col):
        return pl.BlockSpec(
            (qb, C_W), lambda rb, b: (b * nrb + jnp.clip(rb - 1, 0, nrb - nkb) + i, col))

    def variant(rb):
        return jnp.where(rb == 0, 0, jnp.where(rb == nrb - 1, 2, 1))

    def call(body, *extra):
        return pl.pallas_call(
            body,
            grid=(nrb, batch),
            in_specs=[pl.BlockSpec((qb, C_W), lambda rb, b: (b * nrb + rb, 0))]
            + [kspec(i, 1) for i in range(nkb)]
            + [kspec(i, 2) for i in range(nkb)]
            + [pl.BlockSpec((None, C_HEADS, qb, C_ROWS_K * GRID_W), lambda rb, b: (variant(rb), 0, 0, 0))]
            + [_const_spec(e.shape) for e in extra],
            out_specs=pl.BlockSpec((qb, C_W), lambda rb, b: (b * nrb + rb, 0)),
            out_shape=jax.ShapeDtypeStruct((t, C_W), BF16),
            compiler_params=_cparams(2),
            name="natten",
        )(z, z, z, z, z, z, z, bias, *extra)

    norms = jnp.sqrt(jnp.max(stats[:, 0, :2 * C_HEADS], axis=0)) * C_BOUND_SLACK
    qk_bound = norms[:C_HEADS] * norms[C_HEADS:]
    rpb_max = jnp.max(rpb, axis=(1, 2))
    rpb_range = rpb_max - jnp.min(rpb, axis=(1, 2))
    safe = jnp.max(2.0 * qk_bound + rpb_range) <= C_BOUND_LIMIT
    shift = jnp.broadcast_to((qk_bound + rpb_max - C_SHIFT_MARGIN)[:, None], (C_HEADS, C_ROWS_K * GRID_W))
    return lax.cond(safe, lambda: call(_natten_bounded_kernel, shift), lambda: call(_natten_kernel))


def _mem_kv_kernel(mem_ref, g_ref, w_ref, kv_ref):
    mb = _rms(mem_ref[...], g_ref[...]).astype(BF16)
    kv_ref[...] = _dot(mb, w_ref[...]).astype(BF16)


def _mem_kv(mem, g, w):
    batch, n_mem, _ = mem.shape
    return pl.pallas_call(
        _mem_kv_kernel,
        grid=(batch,),
        in_specs=[pl.BlockSpec((None, n_mem, D_MODEL), lambda b: (b, 0, 0)),
                  _const_spec((1, D_MODEL)),
                  _const_spec((D_MODEL, 2 * D_MODEL))],
        out_specs=pl.BlockSpec((None, n_mem, 2 * D_MODEL), lambda b: (b, 0, 0)),
        out_shape=jax.ShapeDtypeStruct((batch, n_mem, 2 * D_MODEL), BF16),
        compiler_params=_cparams(1),
        name="mem_kv",
    )(mem, g, w)


def _xattn_tail(x, g_ref, wq_ref, kv_ref, wo_ref, y_ref):
    q = _dot(_rms(x, g_ref[...]).astype(BF16), wq_ref[...]).astype(BF16)
    outs = []
    for h in range(X_HEADS):
        cols = slice(h * X_HEAD_DIM, (h + 1) * X_HEAD_DIM)
        s = _dot_nt(q[:, cols], kv_ref[:, cols])
        m = jnp.max(s, axis=-1, keepdims=True)
        p = jnp.exp(s - m)
        l = jnp.sum(p, axis=-1, keepdims=True)
        o = _dot(p.astype(BF16), kv_ref[:, D_MODEL + h * X_HEAD_DIM:D_MODEL + (h + 1) * X_HEAD_DIM])
        outs.append((o / l).astype(BF16))
    y_ref[...] = x + _dot(jnp.concatenate(outs, axis=-1), wo_ref[...])


def _mix_ab_xattn_kernel(x_ref, oa_ref, r0_ref, r1_ref, r2_ref, permT_ref, woa_ref, wob_ref,
                         g_ref, wq_ref, kv_ref, wo_ref, y_ref):
    ob = _merge_branches([r0_ref[...], r1_ref[...], r2_ref[...]], permT_ref)
    x = x_ref[...] + _dot(oa_ref[...], woa_ref[...]) + _dot(ob, wob_ref[...])
    _xattn_tail(x, g_ref, wq_ref, kv_ref, wo_ref, y_ref)


def _mix_c_xattn_kernel(x_ref, oc_ref, woc_ref, g_ref, wq_ref, kv_ref, wo_ref, y_ref):
    x = x_ref[...] + _dot(oc_ref[...], woc_ref[...])
    _xattn_tail(x, g_ref, wq_ref, kv_ref, wo_ref, y_ref)


def _mix_xattn(body, x, token_args, const_args, g, wq, kv, wo, seq):
    t = x.shape[0]
    tm = TOKEN_TILE
    n_seq_tiles = seq // tm
    n_mem = kv.shape[1]
    token_spec = lambda a: pl.BlockSpec((tm, a.shape[1]), lambda i: (i, 0))
    in_specs = ([token_spec(x)] + [token_spec(a) for a in token_args]
                + [_const_spec(a.shape) for a in const_args]
                + [_const_spec((1, D_MODEL)),
                   _const_spec((D_MODEL, D_MODEL)),
                   pl.BlockSpec((None, n_mem, 2 * D_MODEL), lambda i: (i // n_seq_tiles, 0, 0)),
                   _const_spec((D_MODEL, D_MODEL))])
    return pl.pallas_call(
        body,
        grid=(t // tm,),
        in_specs=in_specs,
        out_specs=pl.BlockSpec((tm, D_MODEL), lambda i: (i, 0)),
        out_shape=jax.ShapeDtypeStruct((t, D_MODEL), F32),
        compiler_params=_cparams(1),
        name="mix_xattn",
    )(x, *token_args, *const_args, g, wq, kv, wo)


def _swiglu_kernel(x_ref, g_ref, wg_ref, wu_ref, wd_ref, gf_ref, y_ref, *, final_norm):
    sub = x_ref.shape[0] // FFN_SPLIT
    for i in range(FFN_SPLIT):
        rows = slice(i * sub, (i + 1) * sub)
        x = x_ref[rows, :]
        hb = _rms(x, g_ref[...]).astype(BF16)
        gate = _dot(hb, wg_ref[...])
        up = _dot(hb, wu_ref[...])
        act = (gate / (1.0 + jnp.exp(-gate)) * up).astype(BF16)
        y = x + _dot(act, wd_ref[...])
        if final_norm:
            y = _rms(y, gf_ref[...])
        y_ref[rows, :] = y


def _swiglu(x, g, wg, wu, wd, g_final, final_norm):
    t = x.shape[0]
    tm = TOKEN_TILE
    return pl.pallas_call(
        functools.partial(_swiglu_kernel, final_norm=final_norm),
        grid=(t // tm,),
        in_specs=[pl.BlockSpec((tm, D_MODEL), lambda i: (i, 0)),
                  _const_spec((1, D_MODEL)),
                  _const_spec((D_MODEL, D_FF)),
                  _const_spec((D_MODEL, D_FF)),
                  _const_spec((D_FF, D_MODEL)),
                  _const_spec((1, D_MODEL))],
        out_specs=pl.BlockSpec((tm, D_MODEL), lambda i: (i, 0)),
        out_shape=jax.ShapeDtypeStruct((t, D_MODEL), F32),
        compiler_params=_cparams(1),
        name="swiglu",
    )(x, g, wg, wu, wd, g_final)


def _rope_tables(seq):
    tok = jnp.arange(seq, dtype=jnp.int32)
    row = (tok // GRID_W).astype(F32)
    col = (tok % GRID_W).astype(F32)
    axis_dim = HEAD_DIM // 2
    inv_freq = ROPE_THETA ** (-jnp.arange(0, axis_dim, 2, dtype=F32) / axis_dim)
    ang = jnp.concatenate([row[:, None] * inv_freq, col[:, None] * inv_freq], axis=-1)
    return jnp.cos(ang).T, jnp.sin(ang).T


def _prepare_ab(w_in, g_qn, g_kn, w_out):
    perm = np.concatenate([np.arange(0, HEAD_DIM, 2), np.arange(1, HEAD_DIM, 2)])
    cols = np.concatenate(
        [h * HEAD_DIM + perm for h in range(A_Q_HEADS + A_KV_HEADS)]
        + [np.arange(A_Q_W + A_KV_W, A_W)])
    waT = w_in[:, cols].T.astype(BF16)
    zbw = w_in[:, A_W:].reshape(D_MODEL, 3, B_BRANCHES, B_W)
    qscale = HEAD_DIM ** -0.5
    wb = jnp.stack([jnp.concatenate([zbw[:, 0, g] * qscale, zbw[:, 1, g], zbw[:, 2, g]], axis=1)
                    for g in range(B_BRANCHES)]).astype(BF16)
    gq = (g_qn[perm] * (qscale * LOG2E)).reshape(HEAD_DIM, 1).astype(F32)
    gk = g_kn[perm].reshape(HEAD_DIM, 1).astype(F32)
    perms = np.stack([_tile_perm(dil) for _, dil in B_PATTERNS[1:]])
    tile_perm = jnp.asarray(perms, BF16)
    tile_perm_t = jnp.asarray(perms.transpose(0, 2, 1), BF16)
    return waT, wb, tile_perm, tile_perm_t, gq, gk, w_out[:A_Q_W].astype(BF16), w_out[A_Q_W:].astype(BF16)


def _prepare_layers(p):
    depth = p["g_mix"].shape[0]
    row = lambda g: g.reshape(1, D_MODEL)
    layers = []
    for l in range(depth):
        lay = dict(g_mix=row(p["g_mix"][l]), g_xattn=row(p["g_xattn"][l]), g_mem=row(p["g_mem"][l]),
                   g_ffn=row(p["g_ffn"][l]),
                   wq=(p["wq_x"][l] * X_HEAD_DIM ** -0.5).astype(BF16), wkv=p["wkv_x"][l].astype(BF16),
                   wo=p["wo_x"][l].astype(BF16), wg=p["w_gu"][l][:, :D_FF].astype(BF16),
                   wu=p["w_gu"][l][:, D_FF:].astype(BF16), wd=p["w_down"][l].astype(BF16))
        i = l // 2
        if l % 2 == 0:
            lay["ab"] = _prepare_ab(p["w_in_ab"][i], p["g_qn"][i], p["g_kn"][i], p["w_out_ab"][i])
        else:
            w_in = p["w_in_c"][i]
            lay["w_in_c"] = jnp.concatenate(
                [w_in[:, :C_W] * HEAD_DIM ** -0.5, w_in[:, C_W:]], axis=1).astype(BF16)
            lay["rpb_c"] = p["rpb_c"][i]
            lay["bias_c"] = _natten_bias(p["rpb_c"][i])
            lay["w_out_c"] = p["w_out_c"][i].astype(BF16)
        layers.append(lay)
    return layers


def _trunk(x, mem, layers, g_final):
    batch, seq, d_model = x.shape
    assert d_model == D_MODEL and mem.shape[0] == batch and mem.shape[2] == D_MODEL
    assert seq % A_TQ == 0 and seq % TOKEN_TILE == 0 and seq >= 2 * A_TK
    assert seq % (C_ROWS_Q * GRID_W) == 0 and seq >= C_ROWS_K * GRID_W
    assert all(seq % (dil * B_SUB) == 0 for _, dil in B_PATTERNS)
    x = x.reshape(batch * seq, D_MODEL)
    for l, lay in enumerate(layers):
        if l % 2 == 0:
            waT, wb, perm, permT, gq, gk, wo_a, wo_b = lay["ab"]
            cosT, sinT = _rope_tables(seq)
            qT, k, vT, stats, zb = _inproj_ab(x, lay["g_mix"], waT, wb, perm, gq, gk, cosT, sinT, seq)
            oa = _gqa_attention(qT, k, vT, stats, batch, seq)
            rs = [_dilated_branch(zb, g, batch, seq) for g in range(B_BRANCHES)]
            body, token_args, const_args = _mix_ab_xattn_kernel, [oa] + rs, [permT, wo_a, wo_b]
        else:
            z, stats = _inproj(x, lay["g_mix"], lay["w_in_c"])
            oc = _natten(z, stats, lay["rpb_c"], lay["bias_c"], batch, seq)
            body, token_args, const_args = _mix_c_xattn_kernel, [oc], [lay["w_out_c"]]
        kv = _mem_kv(mem, lay["g_mem"], lay["wkv"])
        x = _mix_xattn(body, x, token_args, const_args, lay["g_xattn"], lay["wq"], kv, lay["wo"], seq)
        x = _swiglu(x, lay["g_ffn"], lay["wg"], lay["wu"], lay["wd"], g_final,
                    final_norm=(l == len(layers) - 1))
    return x.reshape(batch, seq, D_MODEL)


def kernel(x_prompt, x_sample, mem_prompt, mem_sample, g_mix, w_in_ab, g_qn, g_kn, w_out_ab, w_in_c,
           rpb_c, w_out_c, g_xattn, g_mem, wq_x, wkv_x, wo_x, g_ffn, w_gu, w_down, g_final):
    layers = _prepare_layers(dict(
        g_mix=g_mix, w_in_ab=w_in_ab, g_qn=g_qn, g_kn=g_kn, w_out_ab=w_out_ab, w_in_c=w_in_c,
        rpb_c=rpb_c, w_out_c=w_out_c, g_xattn=g_xattn, g_mem=g_mem, wq_x=wq_x, wkv_x=wkv_x,
        wo_x=wo_x, g_ffn=g_ffn, w_gu=w_gu, w_down=w_down))
    g_final = g_final.reshape(1, D_MODEL)
    return (_trunk(x_prompt, mem_prompt, layers, g_final), _trunk(x_sample, mem_sample, layers, g_final))
```

```python
import functools
import math

import numpy as np
import jax
import jax.numpy as jnp
from jax import lax
from jax.experimental import pallas as pl
from jax.experimental.pallas import tpu as pltpu

F32 = jnp.float32
BF16 = jnp.bfloat16

D_MODEL = 1024
GRID_W = 64
HEAD_DIM = 64
EPS = 1e-6
NEG = -1e30
LOG2E = math.log2(math.e)
TOKEN_TILE = 512

A_Q_HEADS = 8
A_KV_HEADS = 2
A_GROUP = A_Q_HEADS // A_KV_HEADS
ROPE_THETA = 10000.0
A_Q_W = A_Q_HEADS * HEAD_DIM
A_KV_W = A_KV_HEADS * HEAD_DIM
A_W = A_Q_W + 2 * A_KV_W
A_TQ = 1024
A_TK = 256
A_V_ROWS = HEAD_DIM + 16
A_UNROLL = 6
A_UNROLL_BOUNDED = 16
A_SHIFT_MARGIN = 60.0
A_BOUND_LIMIT = 90.0
A_BOUND_SLACK = 1.0 + 2.0 ** -6

B_PATTERNS = ((128, 1), (512, 4), (2048, 16))
B_BRANCHES = len(B_PATTERNS)
B_HEADS = 4
B_HALF = 64
B_W = B_HEADS * HEAD_DIM
B_SUB = 2 * B_HALF
B_KEYS = 4 * B_HALF
B_CHUNK = 1024
LSE_REP = 2 * HEAD_DIM // B_HEADS
B_R = B_W + 4 * HEAD_DIM

C_HEADS = 16
C_WIN_ROWS = 8
C_WIN_COLS = 16
C_ROWS_Q = 4
C_ROWS_K = 12
C_W = C_HEADS * HEAD_DIM
C_SHIFT_MARGIN = 40.0
C_BOUND_LIMIT = 120.0
C_BOUND_SLACK = 1.0 + 2.0 ** -6

X_HEADS = 4
X_HEAD_DIM = D_MODEL // X_HEADS
D_FF = 2816
FFN_SPLIT = 2

VMEM_LIMIT = 56 * 1024 * 1024


def _cparams(n_axes):
    return pltpu.CompilerParams(dimension_semantics=("arbitrary",) * n_axes,
                                vmem_limit_bytes=VMEM_LIMIT)


def _rms(x, g):
    ms = jnp.mean(x * x, axis=-1, keepdims=True)
    return x * lax.rsqrt(ms + EPS) * g


def _dot(a, b):
    return jnp.dot(a, b, preferred_element_type=F32)


def _dot_nt(a, b):
    return lax.dot_general(a, b, (((1,), (1,)), ((), ())), preferred_element_type=F32)


def _const_spec(shape):
    zeros = (0,) * len(shape)
    return pl.BlockSpec(shape, lambda *_: zeros)


def _inproj_ab_kernel(x_ref, g_ref, waT_ref, wb_ref, perm_ref, gq_ref, gk_ref, cos_ref, sin_ref,
                      qT_ref, k_ref, vT_ref, stats_ref, zb_ref):
    hb = _rms(x_ref[...], g_ref[...]).astype(BF16)
    zT = _dot_nt(waT_ref[...], hb)
    cos = cos_ref[...]
    sin = sin_ref[...]
    half = HEAD_DIM // 2

    def norm_rope(zh, gcol):
        ms = jnp.mean(zh * zh, axis=0, keepdims=True)
        y = zh * lax.rsqrt(ms + EPS) * gcol
        yr, yi = y[:half], y[half:]
        return jnp.concatenate([yr * cos - yi * sin, yr * sin + yi * cos], axis=0)

    def l2(v):
        return jnp.sqrt(jnp.sum(v * v, axis=0, keepdims=True))

    stats_ref[...] = jnp.zeros(stats_ref.shape, F32)
    for h in range(A_Q_HEADS):
        sl = slice(h * HEAD_DIM, (h + 1) * HEAD_DIM)
        q = norm_rope(zT[sl], gq_ref[...])
        qT_ref[sl, :] = q.astype(BF16)
        stats_ref[h // A_GROUP, h % A_GROUP:h % A_GROUP + 1, :] = l2(q)
    ks = [norm_rope(zT[A_Q_W + h * HEAD_DIM:A_Q_W + (h + 1) * HEAD_DIM], gk_ref[...])
          for h in range(A_KV_HEADS)]
    for h in range(A_KV_HEADS):
        stats_ref[h, A_GROUP:A_GROUP + 1, :] = l2(ks[h])
    kT = jnp.concatenate(ks, axis=0)
    k_nat = kT.T
    for h in range(A_KV_HEADS):
        k_ref[h] = k_nat[:, h * HEAD_DIM:(h + 1) * HEAD_DIM].astype(BF16)
    for h in range(A_KV_HEADS):
        for c in range(vT_ref.shape[1]):
            vT_ref[h, c] = zT[A_Q_W + A_KV_W + h * HEAD_DIM:A_Q_W + A_KV_W + (h + 1) * HEAD_DIM,
                              c * A_TK:(c + 1) * A_TK].astype(BF16)
    zb_ref[0] = _dot(hb, wb_ref[0]).astype(BF16)
    for g in range(1, B_BRANCHES):
        zb_ref[g] = _dot(perm_ref[g - 1], _dot(hb, wb_ref[g]).astype(BF16)).astype(BF16)


def _inproj_ab(x, g_mix, waT, wb, perm, gq, gk, cosT, sinT, seq):
    t = x.shape[0]
    tm = TOKEN_TILE
    n_seq_tiles = seq // tm
    return pl.pallas_call(
        _inproj_ab_kernel,
        grid=(t // tm,),
        in_specs=[
            pl.BlockSpec((tm, D_MODEL), lambda i: (i, 0)),
            _const_spec((1, D_MODEL)),
            _const_spec((A_W, D_MODEL)),
            _const_spec((B_BRANCHES, D_MODEL, 3 * B_W)),
            _const_spec((B_BRANCHES - 1, tm, tm)),
            _const_spec((HEAD_DIM, 1)),
            _const_spec((HEAD_DIM, 1)),
            pl.BlockSpec((HEAD_DIM // 2, tm), lambda i: (0, i % n_seq_tiles)),
            pl.BlockSpec((HEAD_DIM // 2, tm), lambda i: (0, i % n_seq_tiles)),
        ],
        out_specs=[
            pl.BlockSpec((A_Q_W, tm), lambda i: (0, i)),
            pl.BlockSpec((A_KV_HEADS, tm, HEAD_DIM), lambda i: (0, i, 0)),
            pl.BlockSpec((A_KV_HEADS, tm // A_TK, HEAD_DIM, A_TK), lambda i: (0, i, 0, 0)),
            pl.BlockSpec((A_KV_HEADS, 8, tm), lambda i: (0, 0, i)),
            pl.BlockSpec((B_BRANCHES, tm, 3 * B_W), lambda i: (0, i, 0)),
        ],
        out_shape=[
            jax.ShapeDtypeStruct((A_Q_W, t), BF16),
            jax.ShapeDtypeStruct((A_KV_HEADS, t, HEAD_DIM), BF16),
            jax.ShapeDtypeStruct((A_KV_HEADS, t // A_TK, HEAD_DIM, A_TK), BF16),
            jax.ShapeDtypeStruct((A_KV_HEADS, 8, t), F32),
            jax.ShapeDtypeStruct((B_BRANCHES, t, 3 * B_W), BF16),
        ],
        compiler_params=_cparams(1),
        name="inproj_ab",
    )(x, g_mix, waT, wb, perm, gq, gk, cosT, sinT)


def _gqa_kernel(qT_ref, k_ref, vT_ref, o_ref, s_scr, p_scr, cm_scr, alpha_scr, m_scr, acc_scr):
    tq = qT_ref.shape[1]
    tk = A_TK
    n_chunks = vT_ref.shape[0]
    ones = jnp.ones((A_V_ROWS - HEAD_DIM, tk), BF16)

    def scores(chunk, slot):
        k = k_ref[pl.ds(pl.multiple_of(chunk * tk, tk), tk), :]
        for h in range(A_GROUP):
            s = _dot(k, qT_ref[h * HEAD_DIM:(h + 1) * HEAD_DIM, :])
            s_scr[slot, h] = s
            cm_scr[slot, h] = jnp.max(s.reshape(tk // 8, 8, tq), axis=0)

    def probs(slot):
        for h in range(A_GROUP):
            m_old = m_scr[h]
            m_new = jnp.maximum(m_old, jnp.max(cm_scr[slot, h], axis=0, keepdims=True))
            alpha_scr[slot, h] = jnp.exp2(m_old - m_new)
            m_scr[h] = m_new
            p_scr[slot, h] = jnp.exp2(s_scr[slot, h] - m_new).astype(BF16)

    def accumulate(chunk, slot):
        vT_aug = jnp.concatenate([vT_ref[chunk], ones], axis=0)
        for h in range(A_GROUP):
            acc_scr[h] = alpha_scr[slot, h] * acc_scr[h] + _dot(vT_aug, p_scr[slot, h])

    def steady(c, count):
        for u in range(count):
            accumulate(c + u, u % 2)
            probs((u + 1) % 2)
            scores(c + u + 2, u % 2)

    m_scr[...] = jnp.full(m_scr.shape, NEG, F32)
    acc_scr[...] = jnp.zeros(acc_scr.shape, F32)
    scores(0, 0)
    scores(1, 1)
    probs(0)
    n_loop = (n_chunks - 2) // A_UNROLL

    def body(i, carry):
        steady(i * A_UNROLL, A_UNROLL)
        return carry

    if n_loop > 1:
        lax.fori_loop(0, n_loop, body, 0)
    else:
        steady(0, n_loop * A_UNROLL)
    steady(n_loop * A_UNROLL, (n_chunks - 2) % A_UNROLL)
    accumulate(n_chunks - 2, 0)
    probs(1)
    accumulate(n_chunks - 1, 1)
    oT = jnp.concatenate(
        [acc_scr[h, :HEAD_DIM, :] / acc_scr[h, HEAD_DIM:HEAD_DIM + 1, :] for h in range(A_GROUP)], axis=0)
    o_ref[...] = oT.T.astype(BF16)


def _gqa_bounded_kernel(qT_ref, k_ref, vT_ref, shift_ref, o_ref, p_scr, acc_scr):
    tq = qT_ref.shape[1]
    tk = A_TK
    n_chunks = vT_ref.shape[0]
    ones = jnp.ones((A_V_ROWS - HEAD_DIM, tk), BF16)

    def probs(chunk, slot):
        k = k_ref[pl.ds(pl.multiple_of(chunk * tk, tk), tk), :]
        for h in range(A_GROUP):
            s = _dot(k, qT_ref[h * HEAD_DIM:(h + 1) * HEAD_DIM, :])
            p_scr[slot, h] = jnp.exp2(s - shift_ref[h:h + 1, :]).astype(BF16)

    def accumulate(chunk, slot):
        vT_aug = jnp.concatenate([vT_ref[chunk], ones], axis=0)
        for h in range(A_GROUP):
            acc_scr[h] += _dot(vT_aug, p_scr[slot, h])

    def steady(c, count):
        for u in range(count):
            accumulate(c + u, u % 2)
            probs(c + u + 2, u % 2)

    acc_scr[...] = jnp.zeros(acc_scr.shape, F32)
    probs(0, 0)
    probs(1, 1)
    n_loop = (n_chunks - 2) // A_UNROLL_BOUNDED

    def body(i, carry):
        steady(i * A_UNROLL_BOUNDED, A_UNROLL_BOUNDED)
        return carry

    if n_loop > 1:
        lax.fori_loop(0, n_loop, body, 0)
    else:
        steady(0, n_loop * A_UNROLL_BOUNDED)
    steady(n_loop * A_UNROLL_BOUNDED, (n_chunks - 2) % A_UNROLL_BOUNDED)
    accumulate(n_chunks - 2, 0)
    accumulate(n_chunks - 1, 1)
    oT = jnp.concatenate(
        [acc_scr[h, :HEAD_DIM, :] / acc_scr[h, HEAD_DIM:HEAD_DIM + 1, :] for h in range(A_GROUP)], axis=0)
    o_ref[...] = oT.T.astype(BF16)


def _gqa_attention(qT, k, vT, stats, batch, seq, tq=A_TQ):
    t = batch * seq
    nq = seq // tq
    k4 = k.reshape(A_KV_HEADS, batch, seq, HEAD_DIM)
    v5 = vT.reshape(A_KV_HEADS, batch, seq // A_TK, HEAD_DIM, A_TK)
    kmax = jnp.max(stats[:, A_GROUP].reshape(A_KV_HEADS, batch, seq), axis=-1)
    bound = (stats.reshape(A_KV_HEADS, 8, batch, seq) * kmax[:, None, :, None]
             * A_BOUND_SLACK).reshape(A_KV_HEADS, 8, t)
    in_specs = [
        pl.BlockSpec((A_GROUP * HEAD_DIM, tq), lambda b, g, i: (g, b * nq + i)),
        pl.BlockSpec((None, None, seq, HEAD_DIM), lambda b, g, i: (g, b, 0, 0)),
        pl.BlockSpec((None, None, seq // A_TK, HEAD_DIM, A_TK), lambda b, g, i: (g, b, 0, 0, 0)),
    ]
    common = dict(
        grid=(batch, A_KV_HEADS, nq),
        out_specs=pl.BlockSpec((tq, A_GROUP * HEAD_DIM), lambda b, g, i: (b * nq + i, g)),
        out_shape=jax.ShapeDtypeStruct((t, A_Q_W), BF16),
        compiler_params=_cparams(3),
    )

    def bounded():
        return pl.pallas_call(
            _gqa_bounded_kernel,
            in_specs=in_specs + [pl.BlockSpec((None, 8, tq), lambda b, g, i: (g, 0, b * nq + i))],
            scratch_shapes=[pltpu.VMEM((2, A_GROUP, A_TK, tq), BF16),
                            pltpu.VMEM((A_GROUP, A_V_ROWS, tq), F32)],
            name="gqa_bounded", **common,
        )(qT, k4, v5, bound - A_SHIFT_MARGIN)

    def online():
        return pl.pallas_call(
            _gqa_kernel,
            in_specs=in_specs,
            scratch_shapes=[pltpu.VMEM((2, A_GROUP, A_TK, tq), F32),
                            pltpu.VMEM((2, A_GROUP, A_TK, tq), BF16),
                            pltpu.VMEM((2, A_GROUP, 8, tq), F32),
                            pltpu.VMEM((2, A_GROUP, 1, tq), F32),
                            pltpu.VMEM((A_GROUP, 1, tq), F32),
                            pltpu.VMEM((A_GROUP, A_V_ROWS, tq), F32)],
            name="gqa_attention", **common,
        )(qT, k4, v5)

    safe = jnp.max(bound[:, :A_GROUP]) <= A_BOUND_LIMIT
    return lax.cond(safe, bounded, online)


def _tile_perm(dil):
    c = TOKEN_TILE // dil
    dst = np.arange(TOKEN_TILE)
    src = (dst % c) * dil + dst // c
    p = np.zeros((TOKEN_TILE, TOKEN_TILE), np.float32)
    p[dst, src] = 1.0
    return p


def _seq_rows(ref, start, n, cols):
    c = ref.shape[1]
    if n <= c:
        return ref[start // c, start % c:start % c + n, cols]
    blocks = ref[start // c:(start + n) // c, :, cols]
    return blocks.reshape(n, blocks.shape[-1])


def _store_seq_rows(ref, start, cols, val):
    c = ref.shape[1]
    n = val.shape[0]
    if n <= c:
        ref[start // c, start % c:start % c + n, cols] = val
    else:
        ref[start // c:(start + n) // c, :, cols] = val.reshape(n // c, c, val.shape[-1])


def _dilated_kernel(main_ref, prev_ref, next_ref, bias_ref, r_ref, kv_scr, *, seq_len, cq):
    n = pl.program_id(2)
    kv_cols = slice(B_W, 3 * B_W)
    pair_w = 2 * HEAD_DIM
    lane = lax.broadcasted_iota(jnp.int32, (1, pair_w), 1)
    lo = lane < HEAD_DIM
    key_col = lax.broadcasted_iota(jnp.int32, (1, B_KEYS), 1)

    for rr in range(main_ref.shape[1]):
        main, res = main_ref.at[:, rr], r_ref.at[:, rr]
        kv_scr[0:B_HALF, :] = prev_ref[..., rr, :, kv_cols].reshape(B_HALF, 2 * B_W)
        for j in range(cq // B_SUB):
            kv_scr[B_HALF + j * B_SUB:B_HALF + (j + 1) * B_SUB, :] = _seq_rows(main, j * B_SUB, B_SUB, kv_cols)
        kv_scr[B_HALF + cq:, :] = next_ref[..., rr, :, kv_cols].reshape(B_HALF, 2 * B_W)

        for j in range(cq // B_SUB):
            key_pos = key_col + (n * cq + j * B_SUB - B_HALF)
            edge = jnp.where((key_pos >= 0) & (key_pos < seq_len), 0.0, NEG).astype(F32)
            krows = slice(j * B_SUB, j * B_SUB + B_KEYS)
            lses = []
            for pr in range(B_HEADS // 2):
                cols = slice(pr * pair_w, (pr + 1) * pair_w)
                q2 = _seq_rows(main, j * B_SUB, B_SUB, cols)
                k2 = kv_scr[krows, cols]
                v2 = kv_scr[krows, B_W + pr * pair_w:B_W + (pr + 1) * pair_w]
                o_pair = []
                for hh in range(2):
                    qm = jnp.where(lo if hh == 0 else ~lo, q2, jnp.zeros_like(q2))
                    s = _dot_nt(qm, k2) + bias_ref[pr * 2 + hh] + edge
                    m = jnp.max(s, axis=-1, keepdims=True)
                    p = jnp.exp(s - m)
                    l = jnp.sum(p, axis=-1, keepdims=True)
                    o_pair.append(_dot(p.astype(BF16), v2) / l)
                    lses.append(m + jnp.log(l))
                _store_seq_rows(res, j * B_SUB, cols, jnp.where(lo, o_pair[0], o_pair[1]).astype(BF16))
            tile = lses[B_HEADS - 1]
            for head in range(B_HEADS - 2, -1, -1):
                tile = jnp.where(lane < (head + 1) * LSE_REP, lses[head], tile)
            hi = tile.astype(BF16)
            _store_seq_rows(res, j * B_SUB, slice(B_W, B_W + pair_w), hi)
            _store_seq_rows(res, j * B_SUB, slice(B_W + pair_w, B_R),
                            (tile - hi.astype(F32)).astype(BF16))


def _dilated_bias(branch):
    _, dil = B_PATTERNS[branch]
    slopes = np.exp2(-8.0 * np.arange(1, B_BRANCHES * B_HEADS + 1, dtype=np.float64)
                     / (B_BRANCHES * B_HEADS)).reshape(B_BRANCHES, B_HEADS)[branch]
    a = np.arange(B_SUB)[:, None]
    c = np.arange(B_KEYS)[None, :]
    rel = (c - B_HALF) - a
    bias = -slopes[:, None, None] * (np.abs(rel) * dil).astype(np.float64)[None]
    bias = np.where((np.abs(rel) <= B_HALF)[None], bias, NEG)
    return jnp.asarray(bias, F32)


def _dilated_branch(zb, branch, batch, seq):
    _, dil = B_PATTERNS[branch]
    t = batch * seq
    seq_len = seq // dil
    c = TOKEN_TILE // dil
    n_tiles = seq // TOKEN_TILE
    cq = min(B_CHUNK, seq_len)
    nchunk = seq_len // cq
    n_halo = seq_len // B_HALF
    per = cq // B_HALF
    rb = min(dil, max(1, B_CHUNK // cq))
    zv = zb.reshape(B_BRANCHES, batch, n_tiles, dil, c, 3 * B_W)

    def halo_spec(pos):
        if c >= B_HALF:
            sub = c // B_HALF
            z6 = zb.reshape(B_BRANCHES, batch, n_tiles, dil, sub, B_HALF, 3 * B_W)
            return z6, pl.BlockSpec((None, None, None, rb, None, B_HALF, 3 * B_W),
                                    lambda b, r, n: (branch, b, pos(n) // sub, r, pos(n) % sub, 0, 0))
        return zv, pl.BlockSpec((None, None, B_HALF // c, rb, c, 3 * B_W),
                                lambda b, r, n: (branch, b, pos(n), r, 0, 0))

    prev_arr, prev_spec = halo_spec(lambda n: jnp.maximum(n * per - 1, 0))
    next_arr, next_spec = halo_spec(lambda n: jnp.minimum((n + 1) * per, n_halo - 1))
    res = pl.pallas_call(
        functools.partial(_dilated_kernel, seq_len=seq_len, cq=cq),
        grid=(batch, dil // rb, nchunk),
        in_specs=[pl.BlockSpec((None, None, cq // c, rb, c, 3 * B_W), lambda b, r, n: (branch, b, n, r, 0, 0)),
                  prev_spec, next_spec, _const_spec((B_HEADS, B_SUB, B_KEYS))],
        out_specs=pl.BlockSpec((None, cq // c, rb, c, B_R), lambda b, r, n: (b, n, r, 0, 0)),
        out_shape=jax.ShapeDtypeStruct((batch, n_tiles, dil, c, B_R), BF16),
        scratch_shapes=[pltpu.VMEM((cq + 2 * B_HALF, 2 * B_W), BF16)],
        compiler_params=_cparams(3),
        name=f"dilated_branch{branch}",
    )(zv, prev_arr, next_arr, _dilated_bias(branch))
    return res.reshape(t, B_R)


def _merge_branches(rs, permT_ref):
    pair_w = 2 * HEAD_DIM
    lo = lax.broadcasted_iota(jnp.int32, (1, pair_w), 1) < HEAD_DIM
    tiles = [rs[0].astype(F32)] + [_dot(permT_ref[g - 1], rs[g]) for g in range(1, B_BRANCHES)]
    lses = [tl[:, B_W:B_W + pair_w] + tl[:, B_W + pair_w:] for tl in tiles]
    mx = functools.reduce(jnp.maximum, lses)
    es = [jnp.exp(l - mx) for l in lses]
    inv = 1.0 / functools.reduce(lambda u, v: u + v, es)
    pairs = []
    for pr in range(B_HEADS // 2):
        acc = None
        for tl, e in zip(tiles, es):
            w = e * inv
            w2 = jnp.where(lo, w[:, 2 * pr * LSE_REP:2 * pr * LSE_REP + 1],
                           w[:, (2 * pr + 1) * LSE_REP:(2 * pr + 1) * LSE_REP + 1])
            term = w2 * tl[:, pr * pair_w:(pr + 1) * pair_w]
            acc = term if acc is None else acc + term
        pairs.append(acc)
    return jnp.concatenate(pairs, axis=-1).astype(BF16)


def _inproj_kernel(x_ref, g_ref, w_ref, z_ref, stats_ref):
    hb = _rms(x_ref[...], g_ref[...]).astype(BF16)
    z = _dot(hb, w_ref[...])
    z_ref[...] = z.astype(BF16)
    pair_w = 2 * HEAD_DIM
    lane = lax.broadcasted_iota(jnp.int32, (1, pair_w), 1)
    lo = lane < HEAD_DIM
    out = jnp.zeros((1, pair_w), F32)
    for b in range(2 * C_W // pair_w):
        sq = z[:, b * pair_w:(b + 1) * pair_w]
        sq = sq * sq
        tot = jnp.sum(sq, axis=-1, keepdims=True)
        first = jnp.sum(jnp.where(lo, sq, 0.0), axis=-1, keepdims=True)
        m_first = jnp.max(first, axis=0, keepdims=True)
        m_second = jnp.max(tot - first, axis=0, keepdims=True)
        out = jnp.where(lane == 2 * b, m_first, jnp.where(lane == 2 * b + 1, m_second, out))
    stats_ref[...] = jnp.broadcast_to(out, stats_ref.shape)


def _inproj(x, g, w):
    t = x.shape[0]
    tm = TOKEN_TILE
    n_out = w.shape[1]
    return pl.pallas_call(
        _inproj_kernel,
        grid=(t // tm,),
        in_specs=[pl.BlockSpec((tm, D_MODEL), lambda i: (i, 0)),
                  _const_spec((1, D_MODEL)),
                  _const_spec((D_MODEL, n_out))],
        out_specs=[pl.BlockSpec((tm, n_out), lambda i: (i, 0)),
                   pl.BlockSpec((None, 8, 2 * HEAD_DIM), lambda i: (i, 0, 0))],
        out_shape=[jax.ShapeDtypeStruct((t, n_out), BF16),
                   jax.ShapeDtypeStruct((t // tm, 8, 2 * HEAD_DIM), F32)],
        compiler_params=_cparams(1),
        name="inproj_c",
    )(x, g, w)


def _natten_kernel(q_ref, k0_ref, k1_ref, k2_ref, v0_ref, v1_ref, v2_ref, bias_ref, o_ref):
    k_refs = (k0_ref, k1_ref, k2_ref)
    v_refs = (v0_ref, v1_ref, v2_ref)
    lane = lax.broadcasted_iota(jnp.int32, (1, 2 * HEAD_DIM), 1)
    lo = lane < HEAD_DIM
    kb = k0_ref.shape[0]
    for pr in range(C_HEADS // 2):
        cols = slice(pr * 2 * HEAD_DIM, (pr + 1) * 2 * HEAD_DIM)
        q2 = q_ref[:, cols]
        ks = [r[:, cols] for r in k_refs]
        vs = [r[:, cols] for r in v_refs]
        o_pair = []
        for hh in range(2):
            qm = jnp.where(lo if hh == 0 else ~lo, q2, jnp.zeros_like(q2))
            s = jnp.concatenate([_dot_nt(qm, kk) for kk in ks], axis=-1) + bias_ref[pr * 2 + hh]
            m = jnp.max(s, axis=-1, keepdims=True)
            p = jnp.exp(s - m)
            l = jnp.sum(p, axis=-1, keepdims=True)
            pb = p.astype(BF16)
            o = _dot(pb[:, :kb], vs[0])
            for i in range(1, len(vs)):
                o = o + _dot(pb[:, i * kb:(i + 1) * kb], vs[i])
            o_pair.append(o / l)
        o_ref[:, cols] = jnp.where(lo, o_pair[0], o_pair[1]).astype(BF16)


def _natten_bounded_kernel(q_ref, k0_ref, k1_ref, k2_ref, v0_ref, v1_ref, v2_ref, bias_ref, shift_ref, o_ref):
    k_refs = (k0_ref, k1_ref, k2_ref)
    v_refs = (v0_ref, v1_ref, v2_ref)
    pair_w = 2 * HEAD_DIM
    lo = lax.broadcasted_iota(jnp.int32, (1, pair_w), 1) < HEAD_DIM
    kb = k0_ref.shape[0]
    one = jnp.ones((), BF16)
    n_pairs = C_HEADS // 2

    def scores(pr):
        cols = slice(pr * pair_w, (pr + 1) * pair_w)
        q2 = q_ref[:, cols]
        ks = [r[:, cols] for r in k_refs]
        out = []
        for hh in range(2):
            qm = jnp.where(lo if hh == 0 else ~lo, q2, jnp.zeros_like(q2))
            out.append(jnp.concatenate([_dot_nt(qm, kk) for kk in ks], axis=-1))
        return out

    def finish(pr, ss):
        cols = slice(pr * pair_w, (pr + 1) * pair_w)
        vs = [r[:, cols] for r in v_refs]
        o_pair = []
        for hh in range(2):
            own = lo if hh == 0 else ~lo
            head = pr * 2 + hh
            p = jnp.exp(ss[hh] + (bias_ref[head] - shift_ref[head:head + 1, :])).astype(BF16)
            r = None
            for i in range(len(vs)):
                t = _dot(p[:, i * kb:(i + 1) * kb], jnp.where(own, vs[i], one))
                r = t if r is None else r + t
            l = r[:, HEAD_DIM:HEAD_DIM + 1] if hh == 0 else r[:, 0:1]
            o_pair.append(r / l)
        o_ref[:, cols] = jnp.where(lo, o_pair[0], o_pair[1]).astype(BF16)

    ss = scores(0)
    for pr in range(n_pairs):
        nxt = scores(pr + 1) if pr + 1 < n_pairs else None
        finish(pr, ss)
        ss = nxt


def _natten_bias(rpb):
    c = np.arange(GRID_W)[:, None]
    kc = np.arange(GRID_W)[None, :]
    cstart = np.clip(c - C_WIN_COLS // 2, 0, GRID_W - C_WIN_COLS)
    col_valid = (kc >= cstart) & (kc < cstart + C_WIN_COLS)
    padded = jnp.pad(rpb.astype(F32), ((0, 0), (0, 0), (GRID_W, GRID_W)))
    base = GRID_W + C_WIN_COLS - 1
    blocks = jnp.stack([padded[:, :, base - q:base - q + GRID_W] for q in range(GRID_W)], axis=2)
    blocks = jnp.where(col_valid[None, None], blocks, NEG)
    masked = jnp.full((rpb.shape[0], GRID_W, GRID_W), NEG, F32)
    tables = []
    for off, first in ((0, lambda dr: 0), (-4, lambda dr: dr), (-8, lambda dr: 4)):
        rows = []
        for dr in range(C_ROWS_Q):
            rows.append(jnp.concatenate(
                [blocks[:, off + kri - dr + C_WIN_ROWS - 1]
                 if first(dr) <= kri < first(dr) + C_WIN_ROWS else masked
                 for kri in range(C_ROWS_K)], axis=-1))
        tables.append(jnp.concatenate(rows, axis=1))
    return jnp.stack(tables)


def _natten(z, stats, rpb, bias, batch, seq):
    t = batch * seq
    qb = C_ROWS_Q * GRID_W
    nrb = seq // qb
    nkb = C_ROWS_K // C_ROWS_Q

    def kspec(i, col):
        return pl.BlockSpec(
            (qb, C_W), lambda rb, b: (b * nrb + jnp.clip(rb - 1, 0, nrb - nkb) + i, col))

    def variant(rb):
        return jnp.where(rb == 0, 0, jnp.where(rb == nrb - 1, 2, 1))

    def call(body, *extra):
        return pl.pallas_call(
            body,
            grid=(nrb, batch),
            in_specs=[pl.BlockSpec((qb, C_W), lambda rb, b: (b * nrb + rb, 0))]
            + [kspec(i, 1) for i in range(nkb)]
            + [kspec(i, 2) for i in range(nkb)]
            + [pl.BlockSpec((None, C_HEADS, qb, C_ROWS_K * GRID_W), lambda rb, b: (variant(rb), 0, 0, 0))]
            + [_const_spec(e.shape) for e in extra],
            out_specs=pl.BlockSpec((qb, C_W), lambda rb, b: (b * nrb + rb, 0)),
            out_shape=jax.ShapeDtypeStruct((t, C_W), BF16),
            compiler_params=_cparams(2),
            name="natten",
        )(z, z, z, z, z, z, z, bias, *extra)

    norms = jnp.sqrt(jnp.max(stats[:, 0, :2 * C_HEADS], axis=0)) * C_BOUND_SLACK
    qk_bound = norms[:C_HEADS] * norms[C_HEADS:]
    rpb_max = jnp.max(rpb, axis=(1, 2))
    rpb_range = rpb_max - jnp.min(rpb, axis=(1, 2))
    safe = jnp.max(2.0 * qk_bound + rpb_range) <= C_BOUND_LIMIT
    shift = jnp.broadcast_to((qk_bound + rpb_max - C_SHIFT_MARGIN)[:, None], (C_HEADS, C_ROWS_K * GRID_W))
    return lax.cond(safe, lambda: call(_natten_bounded_kernel, shift), lambda: call(_natten_kernel))


def _mem_kv_kernel(mem_ref, g_ref, w_ref, kv_ref):
    mb = _rms(mem_ref[...], g_ref[...]).astype(BF16)
    kv_ref[...] = _dot(mb, w_ref[...]).astype(BF16)


def _mem_kv(mem, g, w):
    batch, n_mem, _ = mem.shape
    return pl.pallas_call(
        _mem_kv_kernel,
        grid=(batch,),
        in_specs=[pl.BlockSpec((None, n_mem, D_MODEL), lambda b: (b, 0, 0)),
                  _const_spec((1, D_MODEL)),
                  _const_spec((D_MODEL, 2 * D_MODEL))],
        out_specs=pl.BlockSpec((None, n_mem, 2 * D_MODEL), lambda b: (b, 0, 0)),
        out_shape=jax.ShapeDtypeStruct((batch, n_mem, 2 * D_MODEL), BF16),
        compiler_params=_cparams(1),
        name="mem_kv",
    )(mem, g, w)


def _xattn_tail(x, g_ref, wq_ref, kv_ref, wo_ref, y_ref):
    q = _dot(_rms(x, g_ref[...]).astype(BF16), wq_ref[...]).astype(BF16)
    def score(h):
        cols = slice(h * X_HEAD_DIM, (h + 1) * X_HEAD_DIM)
        return _dot_nt(q[:, cols], kv_ref[:, cols])

    def finish(h, s):
        m = jnp.max(s, axis=-1, keepdims=True)
        p = jnp.exp(s - m)
        l = jnp.sum(p, axis=-1, keepdims=True)
        o = _dot(p.astype(BF16), kv_ref[:, D_MODEL + h * X_HEAD_DIM:D_MODEL + (h + 1) * X_HEAD_DIM])
        return (o / l).astype(BF16)

    outs = []
    s = score(0)
    for h in range(X_HEADS):
        nxt = score(h + 1) if h + 1 < X_HEADS else None
        outs.append(finish(h, s))
        s = nxt
    y_ref[...] = x + _dot(jnp.concatenate(outs, axis=-1), wo_ref[...])


def _mix_ab_xattn_kernel(x_ref, oa_ref, r0_ref, r1_ref, r2_ref, permT_ref, woa_ref, wob_ref,
                         g_ref, wq_ref, kv_ref, wo_ref, y_ref):
    ob = _merge_branches([r0_ref[...], r1_ref[...], r2_ref[...]], permT_ref)
    x = x_ref[...] + _dot(oa_ref[...], woa_ref[...]) + _dot(ob, wob_ref[...])
    _xattn_tail(x, g_ref, wq_ref, kv_ref, wo_ref, y_ref)


def _mix_c_xattn_kernel(x_ref, oc_ref, woc_ref, g_ref, wq_ref, kv_ref, wo_ref, y_ref):
    x = x_ref[...] + _dot(oc_ref[...], woc_ref[...])
    _xattn_tail(x, g_ref, wq_ref, kv_ref, wo_ref, y_ref)


def _mix_xattn(body, x, token_args, const_args, g, wq, kv, wo, seq):
    t = x.shape[0]
    tm = TOKEN_TILE
    n_seq_tiles = seq // tm
    n_mem = kv.shape[1]
    token_spec = lambda a: pl.BlockSpec((tm, a.shape[1]), lambda i: (i, 0))
    in_specs = ([token_spec(x)] + [token_spec(a) for a in token_args]
                + [_const_spec(a.shape) for a in const_args]
                + [_const_spec((1, D_MODEL)),
                   _const_spec((D_MODEL, D_MODEL)),
                   pl.BlockSpec((None, n_mem, 2 * D_MODEL), lambda i: (i // n_seq_tiles, 0, 0)),
                   _const_spec((D_MODEL, D_MODEL))])
    return pl.pallas_call(
        body,
        grid=(t // tm,),
        in_specs=in_specs,
        out_specs=pl.BlockSpec((tm, D_MODEL), lambda i: (i, 0)),
        out_shape=jax.ShapeDtypeStruct((t, D_MODEL), F32),
        compiler_params=_cparams(1),
        name="mix_xattn",
    )(x, *token_args, *const_args, g, wq, kv, wo)


def _swiglu_kernel(x_ref, g_ref, wg_ref, wu_ref, wd_ref, gf_ref, y_ref, *, final_norm):
    sub = x_ref.shape[0] // FFN_SPLIT
    for i in range(FFN_SPLIT):
        rows = slice(i * sub, (i + 1) * sub)
        x = x_ref[rows, :]
        hb = _rms(x, g_ref[...]).astype(BF16)
        gate = _dot(hb, wg_ref[...])
        up = _dot(hb, wu_ref[...])
        act = (gate / (1.0 + jnp.exp(-gate)) * up).astype(BF16)
        y = x + _dot(act, wd_ref[...])
        if final_norm:
            y = _rms(y, gf_ref[...])
        y_ref[rows, :] = y


def _swiglu(x, g, wg, wu, wd, g_final, final_norm):
    t = x.shape[0]
    tm = TOKEN_TILE
    return pl.pallas_call(
        functools.partial(_swiglu_kernel, final_norm=final_norm),
        grid=(t // tm,),
        in_specs=[pl.BlockSpec((tm, D_MODEL), lambda i: (i, 0)),
                  _const_spec((1, D_MODEL)),
                  _const_spec((D_MODEL, D_FF)),
                  _const_spec((D_MODEL, D_FF)),
                  _const_spec((D_FF, D_MODEL)),
                  _const_spec((1, D_MODEL))],
        out_specs=pl.BlockSpec((tm, D_MODEL), lambda i: (i, 0)),
        out_shape=jax.ShapeDtypeStruct((t, D_MODEL), F32),
        compiler_params=_cparams(1),
        name="swiglu",
    )(x, g, wg, wu, wd, g_final)


def _rope_tables(seq):
    tok = jnp.arange(seq, dtype=jnp.int32)
    row = (tok // GRID_W).astype(F32)
    col = (tok % GRID_W).astype(F32)
    axis_dim = HEAD_DIM // 2
    inv_freq = ROPE_THETA ** (-jnp.arange(0, axis_dim, 2, dtype=F32) / axis_dim)
    ang = jnp.concatenate([row[:, None] * inv_freq, col[:, None] * inv_freq], axis=-1)
    return jnp.cos(ang).T, jnp.sin(ang).T


def _prepare_ab(w_in, g_qn, g_kn, w_out):
    perm = np.concatenate([np.arange(0, HEAD_DIM, 2), np.arange(1, HEAD_DIM, 2)])
    cols = np.concatenate(
        [h * HEAD_DIM + perm for h in range(A_Q_HEADS + A_KV_HEADS)]
        + [np.arange(A_Q_W + A_KV_W, A_W)])
    waT = w_in[:, cols].T.astype(BF16)
    zbw = w_in[:, A_W:].reshape(D_MODEL, 3, B_BRANCHES, B_W)
    qscale = HEAD_DIM ** -0.5
    wb = jnp.stack([jnp.concatenate([zbw[:, 0, g] * qscale, zbw[:, 1, g], zbw[:, 2, g]], axis=1)
                    for g in range(B_BRANCHES)]).astype(BF16)
    gq = (g_qn[perm] * (qscale * LOG2E)).reshape(HEAD_DIM, 1).astype(F32)
    gk = g_kn[perm].reshape(HEAD_DIM, 1).astype(F32)
    perms = np.stack([_tile_perm(dil) for _, dil in B_PATTERNS[1:]])
    tile_perm = jnp.asarray(perms, BF16)
    tile_perm_t = jnp.asarray(perms.transpose(0, 2, 1), BF16)
    return waT, wb, tile_perm, tile_perm_t, gq, gk, w_out[:A_Q_W].astype(BF16), w_out[A_Q_W:].astype(BF16)


def _prepare_layers(p):
    depth = p["g_mix"].shape[0]
    row = lambda g: g.reshape(1, D_MODEL)
    layers = []
    for l in range(depth):
        lay = dict(g_mix=row(p["g_mix"][l]), g_xattn=row(p["g_xattn"][l]), g_mem=row(p["g_mem"][l]),
                   g_ffn=row(p["g_ffn"][l]),
                   wq=(p["wq_x"][l] * X_HEAD_DIM ** -0.5).astype(BF16), wkv=p["wkv_x"][l].astype(BF16),
                   wo=p["wo_x"][l].astype(BF16), wg=p["w_gu"][l][:, :D_FF].astype(BF16),
                   wu=p["w_gu"][l][:, D_FF:].astype(BF16), wd=p["w_down"][l].astype(BF16))
        i = l // 2
        if l % 2 == 0:
            lay["ab"] = _prepare_ab(p["w_in_ab"][i], p["g_qn"][i], p["g_kn"][i], p["w_out_ab"][i])
        else:
            w_in = p["w_in_c"][i]
            lay["w_in_c"] = jnp.concatenate(
                [w_in[:, :C_W] * HEAD_DIM ** -0.5, w_in[:, C_W:]], axis=1).astype(BF16)
            lay["rpb_c"] = p["rpb_c"][i]
            lay["bias_c"] = _natten_bias(p["rpb_c"][i])
            lay["w_out_c"] = p["w_out_c"][i].astype(BF16)
        layers.append(lay)
    return layers


def _trunk(x, mem, layers, g_final):
    batch, seq, d_model = x.shape
    assert d_model == D_MODEL and mem.shape[0] == batch and mem.shape[2] == D_MODEL
    assert seq % A_TQ == 0 and seq % TOKEN_TILE == 0 and seq >= 2 * A_TK
    assert seq % (C_ROWS_Q * GRID_W) == 0 and seq >= C_ROWS_K * GRID_W
    assert all(seq % (dil * B_SUB) == 0 for _, dil in B_PATTERNS)
    x = x.reshape(batch * seq, D_MODEL)
    for l, lay in enumerate(layers):
        if l % 2 == 0:
            waT, wb, perm, permT, gq, gk, wo_a, wo_b = lay["ab"]
            cosT, sinT = _rope_tables(seq)
            qT, k, vT, stats, zb = _inproj_ab(x, lay["g_mix"], waT, wb, perm, gq, gk, cosT, sinT, seq)
            oa = _gqa_attention(qT, k, vT, stats, batch, seq)
            rs = [_dilated_branch(zb, g, batch, seq) for g in range(B_BRANCHES)]
            body, token_args, const_args = _mix_ab_xattn_kernel, [oa] + rs, [permT, wo_a, wo_b]
        else:
            z, stats = _inproj(x, lay["g_mix"], lay["w_in_c"])
            oc = _natten(z, stats, lay["rpb_c"], lay["bias_c"], batch, seq)
            body, token_args, const_args = _mix_c_xattn_kernel, [oc], [lay["w_out_c"]]
        kv = _mem_kv(mem, lay["g_mem"], lay["wkv"])
        x = _mix_xattn(body, x, token_args, const_args, lay["g_xattn"], lay["wq"], kv, lay["wo"], seq)
        x = _swiglu(x, lay["g_ffn"], lay["wg"], lay["wu"], lay["wd"], g_final,
                    final_norm=(l == len(layers) - 1))
    return x.reshape(batch, seq, D_MODEL)


def kernel(x_prompt, x_sample, mem_prompt, mem_sample, g_mix, w_in_ab, g_qn, g_kn, w_out_ab, w_in_c,
           rpb_c, w_out_c, g_xattn, g_mem, wq_x, wkv_x, wo_x, g_ffn, w_gu, w_down, g_final):
    layers = _prepare_layers(dict(
        g_mix=g_mix, w_in_ab=w_in_ab, g_qn=g_qn, g_kn=g_kn, w_out_ab=w_out_ab, w_in_c=w_in_c,
        rpb_c=rpb_c, w_out_c=w_out_c, g_xattn=g_xattn, g_mem=g_mem, wq_x=wq_x, wkv_x=wkv_x,
        wo_x=wo_x, g_ffn=g_ffn, w_gu=w_gu, w_down=w_down))
    g_final = g_final.reshape(1, D_MODEL)
    return (_trunk(x_prompt, mem_prompt, layers, g_final), _trunk(x_sample, mem_sample, layers, g_final))
```

```python
import functools
import math

import numpy as np
import jax
import jax.numpy as jnp
from jax import lax
from jax.experimental import pallas as pl
from jax.experimental.pallas import tpu as pltpu

F32 = jnp.float32
BF16 = jnp.bfloat16

D_MODEL = 1024
GRID_W = 64
HEAD_DIM = 64
EPS = 1e-6
NEG = -1e30
LOG2E = math.log2(math.e)
TOKEN_TILE = 512

A_Q_HEADS = 8
A_KV_HEADS = 2
A_GROUP = A_Q_HEADS // A_KV_HEADS
ROPE_THETA = 10000.0
A_Q_W = A_Q_HEADS * HEAD_DIM
A_KV_W = A_KV_HEADS * HEAD_DIM
A_W = A_Q_W + 2 * A_KV_W
A_TQ = 1024
A_TK = 256
A_V_ROWS = HEAD_DIM + 16
A_UNROLL = 6
A_UNROLL_BOUNDED = 16
A_SHIFT_MARGIN = 60.0
A_BOUND_LIMIT = 90.0
A_BOUND_SLACK = 1.0 + 2.0 ** -6

B_PATTERNS = ((128, 1), (512, 4), (2048, 16))
B_BRANCHES = len(B_PATTERNS)
B_HEADS = 4
B_HALF = 64
B_W = B_HEADS * HEAD_DIM
B_SUB = 2 * B_HALF
B_KEYS = 4 * B_HALF
B_CHUNK = 1024
LSE_REP = 2 * HEAD_DIM // B_HEADS
B_R = B_W + 4 * HEAD_DIM

C_HEADS = 16
C_WIN_ROWS = 8
C_WIN_COLS = 16
C_ROWS_Q = 4
C_ROWS_K = 12
C_W = C_HEADS * HEAD_DIM
C_SHIFT_MARGIN = 40.0
C_BOUND_LIMIT = 120.0
C_BOUND_SLACK = 1.0 + 2.0 ** -6

X_HEADS = 4
X_HEAD_DIM = D_MODEL // X_HEADS
D_FF = 2816
FFN_SPLIT = 2

VMEM_LIMIT = 56 * 1024 * 1024


def _cparams(n_axes):
    return pltpu.CompilerParams(dimension_semantics=("arbitrary",) * n_axes,
                                vmem_limit_bytes=VMEM_LIMIT)


def _rms(x, g):
    ms = jnp.mean(x * x, axis=-1, keepdims=True)
    return x * lax.rsqrt(ms + EPS) * g


def _dot(a, b):
    return jnp.dot(a, b, preferred_element_type=F32)


def _dot_nt(a, b):
    return lax.dot_general(a, b, (((1,), (1,)), ((), ())), preferred_element_type=F32)


def _const_spec(shape):
    zeros = (0,) * len(shape)
    return pl.BlockSpec(shape, lambda *_: zeros)


def _inproj_ab_kernel(x_ref, g_ref, waT_ref, wb_ref, perm_ref, gq_ref, gk_ref, cos_ref, sin_ref,
                      qT_ref, k_ref, vT_ref, stats_ref, zb_ref):
    hb = _rms(x_ref[...], g_ref[...]).astype(BF16)
    zT = _dot_nt(waT_ref[...], hb)
    cos = cos_ref[...]
    sin = sin_ref[...]
    half = HEAD_DIM // 2

    def norm_rope(zh, gcol):
        ms = jnp.mean(zh * zh, axis=0, keepdims=True)
        y = zh * lax.rsqrt(ms + EPS) * gcol
        yr, yi = y[:half], y[half:]
        return jnp.concatenate([yr * cos - yi * sin, yr * sin + yi * cos], axis=0)

    def l2(v):
        return jnp.sqrt(jnp.sum(v * v, axis=0, keepdims=True))

    stats_ref[...] = jnp.zeros(stats_ref.shape, F32)
    for h in range(A_Q_HEADS):
        sl = slice(h * HEAD_DIM, (h + 1) * HEAD_DIM)
        q = norm_rope(zT[sl], gq_ref[...])
        qT_ref[sl, :] = q.astype(BF16)
        stats_ref[h // A_GROUP, h % A_GROUP:h % A_GROUP + 1, :] = l2(q)
    ks = [norm_rope(zT[A_Q_W + h * HEAD_DIM:A_Q_W + (h + 1) * HEAD_DIM], gk_ref[...])
          for h in range(A_KV_HEADS)]
    for h in range(A_KV_HEADS):
        stats_ref[h, A_GROUP:A_GROUP + 1, :] = l2(ks[h])
    kT = jnp.concatenate(ks, axis=0)
    k_nat = kT.T
    for h in range(A_KV_HEADS):
        k_ref[h] = k_nat[:, h * HEAD_DIM:(h + 1) * HEAD_DIM].astype(BF16)
    for h in range(A_KV_HEADS):
        for c in range(vT_ref.shape[1]):
            vT_ref[h, c] = zT[A_Q_W + A_KV_W + h * HEAD_DIM:A_Q_W + A_KV_W + (h + 1) * HEAD_DIM,
                              c * A_TK:(c + 1) * A_TK].astype(BF16)
    zb_ref[0] = _dot(hb, wb_ref[0]).astype(BF16)
    for g in range(1, B_BRANCHES):
        zb_ref[g] = _dot(perm_ref[g - 1], _dot(hb, wb_ref[g]).astype(BF16)).astype(BF16)


def _inproj_ab(x, g_mix, waT, wb, perm, gq, gk, cosT, sinT, seq):
    t = x.shape[0]
    tm = TOKEN_TILE
    n_seq_tiles = seq // tm
    return pl.pallas_call(
        _inproj_ab_kernel,
        grid=(t // tm,),
        in_specs=[
            pl.BlockSpec((tm, D_MODEL), lambda i: (i, 0)),
            _const_spec((1, D_MODEL)),
            _const_spec((A_W, D_MODEL)),
            _const_spec((B_BRANCHES, D_MODEL, 3 * B_W)),
            _const_spec((B_BRANCHES - 1, tm, tm)),
            _const_spec((HEAD_DIM, 1)),
            _const_spec((HEAD_DIM, 1)),
            pl.BlockSpec((HEAD_DIM // 2, tm), lambda i: (0, i % n_seq_tiles)),
            pl.BlockSpec((HEAD_DIM // 2, tm), lambda i: (0, i % n_seq_tiles)),
        ],
        out_specs=[
            pl.BlockSpec((A_Q_W, tm), lambda i: (0, i)),
            pl.BlockSpec((A_KV_HEADS, tm, HEAD_DIM), lambda i: (0, i, 0)),
            pl.BlockSpec((A_KV_HEADS, tm // A_TK, HEAD_DIM, A_TK), lambda i: (0, i, 0, 0)),
            pl.BlockSpec((A_KV_HEADS, 8, tm), lambda i: (0, 0, i)),
            pl.BlockSpec((B_BRANCHES, tm, 3 * B_W), lambda i: (0, i, 0)),
        ],
        out_shape=[
            jax.ShapeDtypeStruct((A_Q_W, t), BF16),
            jax.ShapeDtypeStruct((A_KV_HEADS, t, HEAD_DIM), BF16),
            jax.ShapeDtypeStruct((A_KV_HEADS, t // A_TK, HEAD_DIM, A_TK), BF16),
            jax.ShapeDtypeStruct((A_KV_HEADS, 8, t), F32),
            jax.ShapeDtypeStruct((B_BRANCHES, t, 3 * B_W), BF16),
        ],
        compiler_params=_cparams(1),
        name="inproj_ab",
    )(x, g_mix, waT, wb, perm, gq, gk, cosT, sinT)


def _gqa_kernel(qT_ref, k_ref, vT_ref, o_ref, s_scr, p_scr, cm_scr, alpha_scr, m_scr, acc_scr):
    tq = qT_ref.shape[1]
    tk = A_TK
    n_chunks = vT_ref.shape[0]
    ones = jnp.ones((A_V_ROWS - HEAD_DIM, tk), BF16)

    def scores(chunk, slot):
        k = k_ref[pl.ds(pl.multiple_of(chunk * tk, tk), tk), :]
        for h in range(A_GROUP):
            s = _dot(k, qT_ref[h * HEAD_DIM:(h + 1) * HEAD_DIM, :])
            s_scr[slot, h] = s
            cm_scr[slot, h] = jnp.max(s.reshape(tk // 8, 8, tq), axis=0)

    def probs(slot):
        for h in range(A_GROUP):
            m_old = m_scr[h]
            m_new = jnp.maximum(m_old, jnp.max(cm_scr[slot, h], axis=0, keepdims=True))
            alpha_scr[slot, h] = jnp.exp2(m_old - m_new)
            m_scr[h] = m_new
            p_scr[slot, h] = jnp.exp2(s_scr[slot, h] - m_new).astype(BF16)

    def accumulate(chunk, slot):
        vT_aug = jnp.concatenate([vT_ref[chunk], ones], axis=0)
        for h in range(A_GROUP):
            acc_scr[h] = alpha_scr[slot, h] * acc_scr[h] + _dot(vT_aug, p_scr[slot, h])

    def steady(c, count):
        for u in range(count):
            accumulate(c + u, u % 2)
            probs((u + 1) % 2)
            scores(c + u + 2, u % 2)

    m_scr[...] = jnp.full(m_scr.shape, NEG, F32)
    acc_scr[...] = jnp.zeros(acc_scr.shape, F32)
    scores(0, 0)
    scores(1, 1)
    probs(0)
    n_loop = (n_chunks - 2) // A_UNROLL

    def body(i, carry):
        steady(i * A_UNROLL, A_UNROLL)
        return carry

    if n_loop > 1:
        lax.fori_loop(0, n_loop, body, 0)
    else:
        steady(0, n_loop * A_UNROLL)
    steady(n_loop * A_UNROLL, (n_chunks - 2) % A_UNROLL)
    accumulate(n_chunks - 2, 0)
    probs(1)
    accumulate(n_chunks - 1, 1)
    oT = jnp.concatenate(
        [acc_scr[h, :HEAD_DIM, :] / acc_scr[h, HEAD_DIM:HEAD_DIM + 1, :] for h in range(A_GROUP)], axis=0)
    o_ref[...] = oT.T.astype(BF16)


def _gqa_bounded_kernel(qT_ref, k_ref, vT_ref, shift_ref, o_ref, p_scr, acc_scr):
    tq = qT_ref.shape[1]
    tk = A_TK
    n_chunks = vT_ref.shape[0]
    ones = jnp.ones((A_V_ROWS - HEAD_DIM, tk), BF16)

    def probs(chunk, slot):
        k = k_ref[pl.ds(pl.multiple_of(chunk * tk, tk), tk), :]
        for h in range(A_GROUP):
            s = _dot(k, qT_ref[h * HEAD_DIM:(h + 1) * HEAD_DIM, :])
            p_scr[slot, h] = jnp.exp2(s - shift_ref[h:h + 1, :]).astype(BF16)

    def accumulate(chunk, slot):
        vT_aug = jnp.concatenate([vT_ref[chunk], ones], axis=0)
        for h in range(A_GROUP):
            acc_scr[h] += _dot(vT_aug, p_scr[slot, h])

    def steady(c, count):
        for u in range(count):
            accumulate(c + u, u % 2)
            probs(c + u + 2, u % 2)

    acc_scr[...] = jnp.zeros(acc_scr.shape, F32)
    probs(0, 0)
    probs(1, 1)
    n_loop = (n_chunks - 2) // A_UNROLL_BOUNDED

    def body(i, carry):
        steady(i * A_UNROLL_BOUNDED, A_UNROLL_BOUNDED)
        return carry

    if n_loop > 1:
        lax.fori_loop(0, n_loop, body, 0)
    else:
        steady(0, n_loop * A_UNROLL_BOUNDED)
    steady(n_loop * A_UNROLL_BOUNDED, (n_chunks - 2) % A_UNROLL_BOUNDED)
    accumulate(n_chunks - 2, 0)
    accumulate(n_chunks - 1, 1)
    oT = jnp.concatenate(
        [acc_scr[h, :HEAD_DIM, :] / acc_scr[h, HEAD_DIM:HEAD_DIM + 1, :] for h in range(A_GROUP)], axis=0)
    o_ref[...] = oT.T.astype(BF16)


def _gqa_attention(qT, k, vT, stats, batch, seq, tq=A_TQ):
    t = batch * seq
    nq = seq // tq
    k4 = k.reshape(A_KV_HEADS, batch, seq, HEAD_DIM)
    v5 = vT.reshape(A_KV_HEADS, batch, seq // A_TK, HEAD_DIM, A_TK)
    kmax = jnp.max(stats[:, A_GROUP].reshape(A_KV_HEADS, batch, seq), axis=-1)
    bound = (stats.reshape(A_KV_HEADS, 8, batch, seq) * kmax[:, None, :, None]
             * A_BOUND_SLACK).reshape(A_KV_HEADS, 8, t)
    in_specs = [
        pl.BlockSpec((A_GROUP * HEAD_DIM, tq), lambda b, g, i: (g, b * nq + i)),
        pl.BlockSpec((None, None, seq, HEAD_DIM), lambda b, g, i: (g, b, 0, 0)),
        pl.BlockSpec((None, None, seq // A_TK, HEAD_DIM, A_TK), lambda b, g, i: (g, b, 0, 0, 0)),
    ]
    common = dict(
        grid=(batch, A_KV_HEADS, nq),
        out_specs=pl.BlockSpec((tq, A_GROUP * HEAD_DIM), lambda b, g, i: (b * nq + i, g)),
        out_shape=jax.ShapeDtypeStruct((t, A_Q_W), BF16),
        compiler_params=_cparams(3),
    )

    def bounded():
        return pl.pallas_call(
            _gqa_bounded_kernel,
            in_specs=in_specs + [pl.BlockSpec((None, 8, tq), lambda b, g, i: (g, 0, b * nq + i))],
            scratch_shapes=[pltpu.VMEM((2, A_GROUP, A_TK, tq), BF16),
                            pltpu.VMEM((A_GROUP, A_V_ROWS, tq), F32)],
            name="gqa_bounded", **common,
        )(qT, k4, v5, bound - A_SHIFT_MARGIN)

    def online():
        return pl.pallas_call(
            _gqa_kernel,
            in_specs=in_specs,
            scratch_shapes=[pltpu.VMEM((2, A_GROUP, A_TK, tq), F32),
                            pltpu.VMEM((2, A_GROUP, A_TK, tq), BF16),
                            pltpu.VMEM((2, A_GROUP, 8, tq), F32),
                            pltpu.VMEM((2, A_GROUP, 1, tq), F32),
                            pltpu.VMEM((A_GROUP, 1, tq), F32),
                            pltpu.VMEM((A_GROUP, A_V_ROWS, tq), F32)],
            name="gqa_attention", **common,
        )(qT, k4, v5)

    safe = jnp.max(bound[:, :A_GROUP]) <= A_BOUND_LIMIT
    return lax.cond(safe, bounded, online)


def _tile_perm(dil):
    c = TOKEN_TILE // dil
    dst = np.arange(TOKEN_TILE)
    src = (dst % c) * dil + dst // c
    p = np.zeros((TOKEN_TILE, TOKEN_TILE), np.float32)
    p[dst, src] = 1.0
    return p


def _seq_rows(ref, start, n, cols):
    c = ref.shape[1]
    if n <= c:
        return ref[start // c, start % c:start % c + n, cols]
    blocks = ref[start // c:(start + n) // c, :, cols]
    return blocks.reshape(n, blocks.shape[-1])


def _store_seq_rows(ref, start, cols, val):
    c = ref.shape[1]
    n = val.shape[0]
    if n <= c:
        ref[start // c, start % c:start % c + n, cols] = val
    else:
        ref[start // c:(start + n) // c, :, cols] = val.reshape(n // c, c, val.shape[-1])


def _dilated_kernel(main_ref, prev_ref, next_ref, bias_ref, r_ref, kv_scr, *, seq_len, cq):
    n = pl.program_id(2)
    kv_cols = slice(B_W, 3 * B_W)
    pair_w = 2 * HEAD_DIM
    lane = lax.broadcasted_iota(jnp.int32, (1, pair_w), 1)
    lo = lane < HEAD_DIM
    key_col = lax.broadcasted_iota(jnp.int32, (1, B_KEYS), 1)

    for rr in range(main_ref.shape[1]):
        main, res = main_ref.at[:, rr], r_ref.at[:, rr]
        kv_scr[0:B_HALF, :] = prev_ref[..., rr, :, kv_cols].reshape(B_HALF, 2 * B_W)
        for j in range(cq // B_SUB):
            kv_scr[B_HALF + j * B_SUB:B_HALF + (j + 1) * B_SUB, :] = _seq_rows(main, j * B_SUB, B_SUB, kv_cols)
        kv_scr[B_HALF + cq:, :] = next_ref[..., rr, :, kv_cols].reshape(B_HALF, 2 * B_W)

        for j in range(cq // B_SUB):
            key_pos = key_col + (n * cq + j * B_SUB - B_HALF)
            edge = jnp.where((key_pos >= 0) & (key_pos < seq_len), 0.0, NEG).astype(F32)
            krows = slice(j * B_SUB, j * B_SUB + B_KEYS)
            lses = []
            for pr in range(B_HEADS // 2):
                cols = slice(pr * pair_w, (pr + 1) * pair_w)
                q2 = _seq_rows(main, j * B_SUB, B_SUB, cols)
                k2 = kv_scr[krows, cols]
                v2 = kv_scr[krows, B_W + pr * pair_w:B_W + (pr + 1) * pair_w]
                o_pair = []
                for hh in range(2):
                    qm = jnp.where(lo if hh == 0 else ~lo, q2, jnp.zeros_like(q2))
                    s = _dot_nt(qm, k2) + bias_ref[pr * 2 + hh] + edge
                    m = jnp.max(s, axis=-1, keepdims=True)
                    p = jnp.exp(s - m)
                    l = jnp.sum(p, axis=-1, keepdims=True)
                    o_pair.append(_dot(p.astype(BF16), v2) / l)
                    lses.append(m + jnp.log(l))
                _store_seq_rows(res, j * B_SUB, cols, jnp.where(lo, o_pair[0], o_pair[1]).astype(BF16))
            tile = lses[B_HEADS - 1]
            for head in range(B_HEADS - 2, -1, -1):
                tile = jnp.where(lane < (head + 1) * LSE_REP, lses[head], tile)
            hi = tile.astype(BF16)
            _store_seq_rows(res, j * B_SUB, slice(B_W, B_W + pair_w), hi)
            _store_seq_rows(res, j * B_SUB, slice(B_W + pair_w, B_R),
                            (tile - hi.astype(F32)).astype(BF16))


def _dilated_bias(branch):
    _, dil = B_PATTERNS[branch]
    slopes = np.exp2(-8.0 * np.arange(1, B_BRANCHES * B_HEADS + 1, dtype=np.float64)
                     / (B_BRANCHES * B_HEADS)).reshape(B_BRANCHES, B_HEADS)[branch]
    a = np.arange(B_SUB)[:, None]
    c = np.arange(B_KEYS)[None, :]
    rel = (c - B_HALF) - a
    bias = -slopes[:, None, None] * (np.abs(rel) * dil).astype(np.float64)[None]
    bias = np.where((np.abs(rel) <= B_HALF)[None], bias, NEG)
    return jnp.asarray(bias, F32)


def _dilated_branch(zb, branch, batch, seq):
    _, dil = B_PATTERNS[branch]
    t = batch * seq
    seq_len = seq // dil
    c = TOKEN_TILE // dil
    n_tiles = seq // TOKEN_TILE
    cq = min(B_CHUNK, seq_len)
    nchunk = seq_len // cq
    n_halo = seq_len // B_HALF
    per = cq // B_HALF
    rb = min(dil, max(1, B_CHUNK // cq))
    zv = zb.reshape(B_BRANCHES, batch, n_tiles, dil, c, 3 * B_W)

    def halo_spec(pos):
        if c >= B_HALF:
            sub = c // B_HALF
            z6 = zb.reshape(B_BRANCHES, batch, n_tiles, dil, sub, B_HALF, 3 * B_W)
            return z6, pl.BlockSpec((None, None, None, rb, None, B_HALF, 3 * B_W),
                                    lambda b, r, n: (branch, b, pos(n) // sub, r, pos(n) % sub, 0, 0))
        return zv, pl.BlockSpec((None, None, B_HALF // c, rb, c, 3 * B_W),
                                lambda b, r, n: (branch, b, pos(n), r, 0, 0))

    prev_arr, prev_spec = halo_spec(lambda n: jnp.maximum(n * per - 1, 0))
    next_arr, next_spec = halo_spec(lambda n: jnp.minimum((n + 1) * per, n_halo - 1))
    res = pl.pallas_call(
        functools.partial(_dilated_kernel, seq_len=seq_len, cq=cq),
        grid=(batch, dil // rb, nchunk),
        in_specs=[pl.BlockSpec((None, None, cq // c, rb, c, 3 * B_W), lambda b, r, n: (branch, b, n, r, 0, 0)),
                  prev_spec, next_spec, _const_spec((B_HEADS, B_SUB, B_KEYS))],
        out_specs=pl.BlockSpec((None, cq // c, rb, c, B_R), lambda b, r, n: (b, n, r, 0, 0)),
        out_shape=jax.ShapeDtypeStruct((batch, n_tiles, dil, c, B_R), BF16),
        scratch_shapes=[pltpu.VMEM((cq + 2 * B_HALF, 2 * B_W), BF16)],
        compiler_params=_cparams(3),
        name=f"dilated_branch{branch}",
    )(zv, prev_arr, next_arr, _dilated_bias(branch))
    return res.reshape(t, B_R)


def _merge_branches(rs, permT_ref):
    pair_w = 2 * HEAD_DIM
    lo = lax.broadcasted_iota(jnp.int32, (1, pair_w), 1) < HEAD_DIM
    tiles = [rs[0].astype(F32)] + [_dot(permT_ref[g - 1], rs[g]) for g in range(1, B_BRANCHES)]
    lses = [tl[:, B_W:B_W + pair_w] + tl[:, B_W + pair_w:] for tl in tiles]
    mx = functools.reduce(jnp.maximum, lses)
    es = [jnp.exp(l - mx) for l in lses]
    inv = 1.0 / functools.reduce(lambda u, v: u + v, es)
    pairs = []
    for pr in range(B_HEADS // 2):
        acc = None
        for tl, e in zip(tiles, es):
            w = e * inv
            w2 = jnp.where(lo, w[:, 2 * pr * LSE_REP:2 * pr * LSE_REP + 1],
                           w[:, (2 * pr + 1) * LSE_REP:(2 * pr + 1) * LSE_REP + 1])
            term = w2 * tl[:, pr * pair_w:(pr + 1) * pair_w]
            acc = term if acc is None else acc + term
        pairs.append(acc)
    return jnp.concatenate(pairs, axis=-1).astype(BF16)


def _inproj_kernel(x_ref, g_ref, w_ref, z_ref, stats_ref):
    hb = _rms(x_ref[...], g_ref[...]).astype(BF16)
    z = _dot(hb, w_ref[...])
    z_ref[...] = z.astype(BF16)
    pair_w = 2 * HEAD_DIM
    lane = lax.broadcasted_iota(jnp.int32, (1, pair_w), 1)
    lo = lane < HEAD_DIM
    out = jnp.zeros((1, pair_w), F32)
    for b in range(2 * C_W // pair_w):
        sq = z[:, b * pair_w:(b + 1) * pair_w]
        sq = sq * sq
        tot = jnp.sum(sq, axis=-1, keepdims=True)
        first = jnp.sum(jnp.where(lo, sq, 0.0), axis=-1, keepdims=True)
        m_first = jnp.max(first, axis=0, keepdims=True)
        m_second = jnp.max(tot - first, axis=0, keepdims=True)
        out = jnp.where(lane == 2 * b, m_first, jnp.where(lane == 2 * b + 1, m_second, out))
    stats_ref[...] = jnp.broadcast_to(out, stats_ref.shape)


def _inproj(x, g, w):
    t = x.shape[0]
    tm = TOKEN_TILE
    n_out = w.shape[1]
    return pl.pallas_call(
        _inproj_kernel,
        grid=(t // tm,),
        in_specs=[pl.BlockSpec((tm, D_MODEL), lambda i: (i, 0)),
                  _const_spec((1, D_MODEL)),
                  _const_spec((D_MODEL, n_out))],
        out_specs=[pl.BlockSpec((tm, n_out), lambda i: (i, 0)),
                   pl.BlockSpec((None, 8, 2 * HEAD_DIM), lambda i: (i, 0, 0))],
        out_shape=[jax.ShapeDtypeStruct((t, n_out), BF16),
                   jax.ShapeDtypeStruct((t // tm, 8, 2 * HEAD_DIM), F32)],
        compiler_params=_cparams(1),
        name="inproj_c",
    )(x, g, w)


def _natten_kernel(q_ref, k0_ref, k1_ref, k2_ref, v0_ref, v1_ref, v2_ref, bias_ref, o_ref):
    k_refs = (k0_ref, k1_ref, k2_ref)
    v_refs = (v0_ref, v1_ref, v2_ref)
    lane = lax.broadcasted_iota(jnp.int32, (1, 2 * HEAD_DIM), 1)
    lo = lane < HEAD_DIM
    kb = k0_ref.shape[0]
    for pr in range(C_HEADS // 2):
        cols = slice(pr * 2 * HEAD_DIM, (pr + 1) * 2 * HEAD_DIM)
        q2 = q_ref[:, cols]
        ks = [r[:, cols] for r in k_refs]
        vs = [r[:, cols] for r in v_refs]
        o_pair = []
        for hh in range(2):
            qm = jnp.where(lo if hh == 0 else ~lo, q2, jnp.zeros_like(q2))
            s = jnp.concatenate([_dot_nt(qm, kk) for kk in ks], axis=-1) + bias_ref[pr * 2 + hh]
            m = jnp.max(s, axis=-1, keepdims=True)
            p = jnp.exp(s - m)
            l = jnp.sum(p, axis=-1, keepdims=True)
            pb = p.astype(BF16)
            o = _dot(pb[:, :kb], vs[0])
            for i in range(1, len(vs)):
                o = o + _dot(pb[:, i * kb:(i + 1) * kb], vs[i])
            o_pair.append(o / l)
        o_ref[:, cols] = jnp.where(lo, o_pair[0], o_pair[1]).astype(BF16)


def _natten_bounded_kernel(q_ref, k0_ref, k1_ref, k2_ref, v0_ref, v1_ref, v2_ref, bias_ref, shift_ref, o_ref):
    k_refs = (k0_ref, k1_ref, k2_ref)
    v_refs = (v0_ref, v1_ref, v2_ref)
    pair_w = 2 * HEAD_DIM
    lo = lax.broadcasted_iota(jnp.int32, (1, pair_w), 1) < HEAD_DIM
    kb = k0_ref.shape[0]
    one = jnp.ones((), BF16)

    def cols_of(head):
        return slice((head // 2) * pair_w, (head // 2 + 1) * pair_w)

    def own_lanes(head):
        return lo if head % 2 == 0 else ~lo

    def scores(head):
        q2 = q_ref[:, cols_of(head)]
        qm = jnp.where(own_lanes(head), q2, jnp.zeros_like(q2))
        return jnp.concatenate([_dot_nt(qm, r[:, cols_of(head)]) for r in k_refs], axis=-1)

    def probs(head, s):
        return jnp.exp(s + (bias_ref[head] - shift_ref[head:head + 1, :])).astype(BF16)

    def output(head, p):
        r = None
        for i, v_ref in enumerate(v_refs):
            t = _dot(p[:, i * kb:(i + 1) * kb], jnp.where(own_lanes(head), v_ref[:, cols_of(head)], one))
            r = t if r is None else r + t
        l = r[:, HEAD_DIM:HEAD_DIM + 1] if head % 2 == 0 else r[:, 0:1]
        return r / l

    s = {0: scores(0), 1: scores(1)}
    p = {0: probs(0, s[0])}
    o = {}
    for h in range(C_HEADS):
        if h + 2 < C_HEADS:
            s[h + 2] = scores(h + 2)
        if h + 1 < C_HEADS:
            p[h + 1] = probs(h + 1, s[h + 1])
        o[h] = output(h, p[h])
        if h % 2 == 1:
            o_ref[:, cols_of(h)] = jnp.where(lo, o[h - 1], o[h]).astype(BF16)


def _natten_bias(rpb):
    c = np.arange(GRID_W)[:, None]
    kc = np.arange(GRID_W)[None, :]
    cstart = np.clip(c - C_WIN_COLS // 2, 0, GRID_W - C_WIN_COLS)
    col_valid = (kc >= cstart) & (kc < cstart + C_WIN_COLS)
    padded = jnp.pad(rpb.astype(F32), ((0, 0), (0, 0), (GRID_W, GRID_W)))
    base = GRID_W + C_WIN_COLS - 1
    blocks = jnp.stack([padded[:, :, base - q:base - q + GRID_W] for q in range(GRID_W)], axis=2)
    blocks = jnp.where(col_valid[None, None], blocks, NEG)
    masked = jnp.full((rpb.shape[0], GRID_W, GRID_W), NEG, F32)
    tables = []
    for off, first in ((0, lambda dr: 0), (-4, lambda dr: dr), (-8, lambda dr: 4)):
        rows = []
        for dr in range(C_ROWS_Q):
            rows.append(jnp.concatenate(
                [blocks[:, off + kri - dr + C_WIN_ROWS - 1]
                 if first(dr) <= kri < first(dr) + C_WIN_ROWS else masked
                 for kri in range(C_ROWS_K)], axis=-1))
        tables.append(jnp.concatenate(rows, axis=1))
    return jnp.stack(tables)


def _natten(z, stats, rpb, bias, batch, seq):
    t = batch * seq
    qb = C_ROWS_Q * GRID_W
    nrb = seq // qb
    nkb = C_ROWS_K // C_ROWS_Q

    def kspec(i, col):
        return pl.BlockSpec(
            (qb, C_W), lambda rb, b: (b * nrb + jnp.clip(rb - 1, 0, nrb - nkb) + i, col))

    def variant(rb):
        return jnp.where(rb == 0, 0, jnp.where(rb == nrb - 1, 2, 1))

    def call(body, *extra):
        return pl.pallas_call(
            body,
            grid=(nrb, batch),
            in_specs=[pl.BlockSpec((qb, C_W), lambda rb, b: (b * nrb + rb, 0))]
            + [kspec(i, 1) for i in range(nkb)]
            + [kspec(i, 2) for i in range(nkb)]
            + [pl.BlockSpec((None, C_HEADS, qb, C_ROWS_K * GRID_W), lambda rb, b: (variant(rb), 0, 0, 0))]
            + [_const_spec(e.shape) for e in extra],
            out_specs=pl.BlockSpec((qb, C_W), lambda rb, b: (b * nrb + rb, 0)),
            out_shape=jax.ShapeDtypeStruct((t, C_W), BF16),
            compiler_params=_cparams(2),
            name="natten",
        )(z, z, z, z, z, z, z, bias, *extra)

    norms = jnp.sqrt(jnp.max(stats[:, 0, :2 * C_HEADS], axis=0)) * C_BOUND_SLACK
    qk_bound = norms[:C_HEADS] * norms[C_HEADS:]
    rpb_max = jnp.max(rpb, axis=(1, 2))
    rpb_range = rpb_max - jnp.min(rpb, axis=(1, 2))
    safe = jnp.max(2.0 * qk_bound + rpb_range) <= C_BOUND_LIMIT
    shift = jnp.broadcast_to((qk_bound + rpb_max - C_SHIFT_MARGIN)[:, None], (C_HEADS, C_ROWS_K * GRID_W))
    return lax.cond(safe, lambda: call(_natten_bounded_kernel, shift), lambda: call(_natten_kernel))


def _mem_kv_kernel(mem_ref, g_ref, w_ref, kv_ref):
    mb = _rms(mem_ref[...], g_ref[...]).astype(BF16)
    kv_ref[...] = _dot(mb, w_ref[...]).astype(BF16)


def _mem_kv(mem, g, w):
    batch, n_mem, _ = mem.shape
    return pl.pallas_call(
        _mem_kv_kernel,
        grid=(batch,),
        in_specs=[pl.BlockSpec((None, n_mem, D_MODEL), lambda b: (b, 0, 0)),
                  _const_spec((1, D_MODEL)),
                  _const_spec((D_MODEL, 2 * D_MODEL))],
        out_specs=pl.BlockSpec((None, n_mem, 2 * D_MODEL), lambda b: (b, 0, 0)),
        out_shape=jax.ShapeDtypeStruct((batch, n_mem, 2 * D_MODEL), BF16),
        compiler_params=_cparams(1),
        name="mem_kv",
    )(mem, g, w)


def _xattn_tail(x, g_ref, wq_ref, kv_ref, wo_ref, y_ref):
    q = _dot(_rms(x, g_ref[...]).astype(BF16), wq_ref[...]).astype(BF16)
    def score(h):
        cols = slice(h * X_HEAD_DIM, (h + 1) * X_HEAD_DIM)
        return _dot_nt(q[:, cols], kv_ref[:, cols])

    def finish(h, s):
        m = jnp.max(s, axis=-1, keepdims=True)
        p = jnp.exp(s - m)
        l = jnp.sum(p, axis=-1, keepdims=True)
        o = _dot(p.astype(BF16), kv_ref[:, D_MODEL + h * X_HEAD_DIM:D_MODEL + (h + 1) * X_HEAD_DIM])
        return (o / l).astype(BF16)

    outs = []
    s = score(0)
    for h in range(X_HEADS):
        nxt = score(h + 1) if h + 1 < X_HEADS else None
        outs.append(finish(h, s))
        s = nxt
    y_ref[...] = x + _dot(jnp.concatenate(outs, axis=-1), wo_ref[...])


def _mix_ab_xattn_kernel(x_ref, oa_ref, r0_ref, r1_ref, r2_ref, permT_ref, woa_ref, wob_ref,
                         g_ref, wq_ref, kv_ref, wo_ref, y_ref):
    ob = _merge_branches([r0_ref[...], r1_ref[...], r2_ref[...]], permT_ref)
    x = x_ref[...] + _dot(oa_ref[...], woa_ref[...]) + _dot(ob, wob_ref[...])
    _xattn_tail(x, g_ref, wq_ref, kv_ref, wo_ref, y_ref)


def _mix_c_xattn_kernel(x_ref, oc_ref, woc_ref, g_ref, wq_ref, kv_ref, wo_ref, y_ref):
    x = x_ref[...] + _dot(oc_ref[...], woc_ref[...])
    _xattn_tail(x, g_ref, wq_ref, kv_ref, wo_ref, y_ref)


def _mix_xattn(body, x, token_args, const_args, g, wq, kv, wo, seq):
    t = x.shape[0]
    tm = TOKEN_TILE
    n_seq_tiles = seq // tm
    n_mem = kv.shape[1]
    token_spec = lambda a: pl.BlockSpec((tm, a.shape[1]), lambda i: (i, 0))
    in_specs = ([token_spec(x)] + [token_spec(a) for a in token_args]
                + [_const_spec(a.shape) for a in const_args]
                + [_const_spec((1, D_MODEL)),
                   _const_spec((D_MODEL, D_MODEL)),
                   pl.BlockSpec((None, n_mem, 2 * D_MODEL), lambda i: (i // n_seq_tiles, 0, 0)),
                   _const_spec((D_MODEL, D_MODEL))])
    return pl.pallas_call(
        body,
        grid=(t // tm,),
        in_specs=in_specs,
        out_specs=pl.BlockSpec((tm, D_MODEL), lambda i: (i, 0)),
        out_shape=jax.ShapeDtypeStruct((t, D_MODEL), F32),
        compiler_params=_cparams(1),
        name="mix_xattn",
    )(x, *token_args, *const_args, g, wq, kv, wo)


def _swiglu_kernel(x_ref, g_ref, wg_ref, wu_ref, wd_ref, gf_ref, y_ref, *, final_norm):
    sub = x_ref.shape[0] // FFN_SPLIT
    for i in range(FFN_SPLIT):
        rows = slice(i * sub, (i + 1) * sub)
        x = x_ref[rows, :]
        hb = _rms(x, g_ref[...]).astype(BF16)
        gate = _dot(hb, wg_ref[...])
        up = _dot(hb, wu_ref[...])
        act = (gate / (1.0 + jnp.exp(-gate)) * up).astype(BF16)
        y = x + _dot(act, wd_ref[...])
        if final_norm:
            y = _rms(y, gf_ref[...])
        y_ref[rows, :] = y


def _swiglu(x, g, wg, wu, wd, g_final, final_norm):
    t = x.shape[0]
    tm = TOKEN_TILE
    return pl.pallas_call(
        functools.partial(_swiglu_kernel, final_norm=final_norm),
        grid=(t // tm,),
        in_specs=[pl.BlockSpec((tm, D_MODEL), lambda i: (i, 0)),
                  _const_spec((1, D_MODEL)),
                  _const_spec((D_MODEL, D_FF)),
                  _const_spec((D_MODEL, D_FF)),
                  _const_spec((D_FF, D_MODEL)),
                  _const_spec((1, D_MODEL))],
        out_specs=pl.BlockSpec((tm, D_MODEL), lambda i: (i, 0)),
        out_shape=jax.ShapeDtypeStruct((t, D_MODEL), F32),
        compiler_params=_cparams(1),
        name="swiglu",
    )(x, g, wg, wu, wd, g_final)


def _rope_tables(seq):
    tok = jnp.arange(seq, dtype=jnp.int32)
    row = (tok // GRID_W).astype(F32)
    col = (tok % GRID_W).astype(F32)
    axis_dim = HEAD_DIM // 2
    inv_freq = ROPE_THETA ** (-jnp.arange(0, axis_dim, 2, dtype=F32) / axis_dim)
    ang = jnp.concatenate([row[:, None] * inv_freq, col[:, None] * inv_freq], axis=-1)
    return jnp.cos(ang).T, jnp.sin(ang).T


def _prepare_ab(w_in, g_qn, g_kn, w_out):
    perm = np.concatenate([np.arange(0, HEAD_DIM, 2), np.arange(1, HEAD_DIM, 2)])
    cols = np.concatenate(
        [h * HEAD_DIM + perm for h in range(A_Q_HEADS + A_KV_HEADS)]
        + [np.arange(A_Q_W + A_KV_W, A_W)])
    waT = w_in[:, cols].T.astype(BF16)
    zbw = w_in[:, A_W:].reshape(D_MODEL, 3, B_BRANCHES, B_W)
    qscale = HEAD_DIM ** -0.5
    wb = jnp.stack([jnp.concatenate([zbw[:, 0, g] * qscale, zbw[:, 1, g], zbw[:, 2, g]], axis=1)
                    for g in range(B_BRANCHES)]).astype(BF16)
    gq = (g_qn[perm] * (qscale * LOG2E)).reshape(HEAD_DIM, 1).astype(F32)
    gk = g_kn[perm].reshape(HEAD_DIM, 1).astype(F32)
    perms = np.stack([_tile_perm(dil) for _, dil in B_PATTERNS[1:]])
    tile_perm = jnp.asarray(perms, BF16)
    tile_perm_t = jnp.asarray(perms.transpose(0, 2, 1), BF16)
    return waT, wb, tile_perm, tile_perm_t, gq, gk, w_out[:A_Q_W].astype(BF16), w_out[A_Q_W:].astype(BF16)


def _prepare_layers(p):
    depth = p["g_mix"].shape[0]
    row = lambda g: g.reshape(1, D_MODEL)
    layers = []
    for l in range(depth):
        lay = dict(g_mix=row(p["g_mix"][l]), g_xattn=row(p["g_xattn"][l]), g_mem=row(p["g_mem"][l]),
                   g_ffn=row(p["g_ffn"][l]),
                   wq=(p["wq_x"][l] * X_HEAD_DIM ** -0.5).astype(BF16), wkv=p["wkv_x"][l].astype(BF16),
                   wo=p["wo_x"][l].astype(BF16), wg=p["w_gu"][l][:, :D_FF].astype(BF16),
                   wu=p["w_gu"][l][:, D_FF:].astype(BF16), wd=p["w_down"][l].astype(BF16))
        i = l // 2
        if l % 2 == 0:
            lay["ab"] = _prepare_ab(p["w_in_ab"][i], p["g_qn"][i], p["g_kn"][i], p["w_out_ab"][i])
        else:
            w_in = p["w_in_c"][i]
            lay["w_in_c"] = jnp.concatenate(
                [w_in[:, :C_W] * HEAD_DIM ** -0.5, w_in[:, C_W:]], axis=1).astype(BF16)
            lay["rpb_c"] = p["rpb_c"][i]
            lay["bias_c"] = _natten_bias(p["rpb_c"][i])
            lay["w_out_c"] = p["w_out_c"][i].astype(BF16)
        layers.append(lay)
    return layers


def _trunk(x, mem, layers, g_final):
    batch, seq, d_model = x.shape
    assert d_model == D_MODEL and mem.shape[0] == batch and mem.shape[2] == D_MODEL
    assert seq % A_TQ == 0 and seq % TOKEN_TILE == 0 and seq >= 2 * A_TK
    assert seq % (C_ROWS_Q * GRID_W) == 0 and seq >= C_ROWS_K * GRID_W
    assert all(seq % (dil * B_SUB) == 0 for _, dil in B_PATTERNS)
    x = x.reshape(batch * seq, D_MODEL)
    for l, lay in enumerate(layers):
        if l % 2 == 0:
            waT, wb, perm, permT, gq, gk, wo_a, wo_b = lay["ab"]
            cosT, sinT = _rope_tables(seq)
            qT, k, vT, stats, zb = _inproj_ab(x, lay["g_mix"], waT, wb, perm, gq, gk, cosT, sinT, seq)
            oa = _gqa_attention(qT, k, vT, stats, batch, seq)
            rs = [_dilated_branch(zb, g, batch, seq) for g in range(B_BRANCHES)]
            body, token_args, const_args = _mix_ab_xattn_kernel, [oa] + rs, [permT, wo_a, wo_b]
        else:
            z, stats = _inproj(x, lay["g_mix"], lay["w_in_c"])
            oc = _natten(z, stats, lay["rpb_c"], lay["bias_c"], batch, seq)
            body, token_args, const_args = _mix_c_xattn_kernel, [oc], [lay["w_out_c"]]
        kv = _mem_kv(mem, lay["g_mem"], lay["wkv"])
        x = _mix_xattn(body, x, token_args, const_args, lay["g_xattn"], lay["wq"], kv, lay["wo"], seq)
        x = _swiglu(x, lay["g_ffn"], lay["wg"], lay["wu"], lay["wd"], g_final,
                    final_norm=(l == len(layers) - 1))
    return x.reshape(batch, seq, D_MODEL)


def kernel(x_prompt, x_sample, mem_prompt, mem_sample, g_mix, w_in_ab, g_qn, g_kn, w_out_ab, w_in_c,
           rpb_c, w_out_c, g_xattn, g_mem, wq_x, wkv_x, wo_x, g_ffn, w_gu, w_down, g_final):
    layers = _prepare_layers(dict(
        g_mix=g_mix, w_in_ab=w_in_ab, g_qn=g_qn, g_kn=g_kn, w_out_ab=w_out_ab, w_in_c=w_in_c,
        rpb_c=rpb_c, w_out_c=w_out_c, g_xattn=g_xattn, g_mem=g_mem, wq_x=wq_x, wkv_x=wkv_x,
        wo_x=wo_x, g_ffn=g_ffn, w_gu=w_gu, w_down=w_down))
    g_final = g_final.reshape(1, D_MODEL)
    return (_trunk(x_prompt, mem_prompt, layers, g_final), _trunk(x_sample, mem_sample, layers, g_final))
```

```python
import functools
import math

import numpy as np
import jax
import jax.numpy as jnp
from jax import lax
from jax.experimental import pallas as pl
from jax.experimental.pallas import tpu as pltpu

F32 = jnp.float32
BF16 = jnp.bfloat16

D_MODEL = 1024
GRID_W = 64
HEAD_DIM = 64
EPS = 1e-6
NEG = -1e30
LOG2E = math.log2(math.e)
TOKEN_TILE = 512

A_Q_HEADS = 8
A_KV_HEADS = 2
A_GROUP = A_Q_HEADS // A_KV_HEADS
ROPE_THETA = 10000.0
A_Q_W = A_Q_HEADS * HEAD_DIM
A_KV_W = A_KV_HEADS * HEAD_DIM
A_W = A_Q_W + 2 * A_KV_W
A_TQ = 1024
A_TK = 256
A_V_ROWS = HEAD_DIM + 16
A_UNROLL = 6
A_UNROLL_BOUNDED = 16
A_SHIFT_MARGIN = 60.0
A_BOUND_LIMIT = 90.0
A_BOUND_SLACK = 1.0 + 2.0 ** -6

B_PATTERNS = ((128, 1), (512, 4), (2048, 16))
B_BRANCHES = len(B_PATTERNS)
B_HEADS = 4
B_HALF = 64
B_W = B_HEADS * HEAD_DIM
B_SUB = 2 * B_HALF
B_KEYS = 4 * B_HALF
B_CHUNK = 1024
LSE_REP = 2 * HEAD_DIM // B_HEADS
B_R = B_W + 4 * HEAD_DIM

C_HEADS = 16
C_WIN_ROWS = 8
C_WIN_COLS = 16
C_ROWS_Q = 4
C_ROWS_K = 12
C_W = C_HEADS * HEAD_DIM
C_SHIFT_MARGIN = 40.0
C_BOUND_LIMIT = 120.0
C_BOUND_SLACK = 1.0 + 2.0 ** -6

X_HEADS = 4
X_HEAD_DIM = D_MODEL // X_HEADS
D_FF = 2816
FFN_TILE = 1024
FFN_SPLIT = 4

VMEM_LIMIT = 56 * 1024 * 1024


def _cparams(n_axes):
    return pltpu.CompilerParams(dimension_semantics=("arbitrary",) * n_axes,
                                vmem_limit_bytes=VMEM_LIMIT)


def _rms(x, g):
    ms = jnp.mean(x * x, axis=-1, keepdims=True)
    return x * lax.rsqrt(ms + EPS) * g


def _dot(a, b):
    return jnp.dot(a, b, preferred_element_type=F32)


def _dot_nt(a, b):
    return lax.dot_general(a, b, (((1,), (1,)), ((), ())), preferred_element_type=F32)


def _const_spec(shape):
    zeros = (0,) * len(shape)
    return pl.BlockSpec(shape, lambda *_: zeros)


def _resident_spec(shape):
    zeros = (0,) * len(shape)
    return pl.BlockSpec(shape, lambda *_: zeros, pipeline_mode=pl.Buffered(1))


def _inproj_ab_kernel(x_ref, g_ref, waT_ref, wb_ref, perm_ref, gq_ref, gk_ref, cos_ref, sin_ref,
                      qT_ref, k_ref, vT_ref, stats_ref, zb_ref):
    hb = _rms(x_ref[...], g_ref[...]).astype(BF16)
    zT = _dot_nt(waT_ref[...], hb)
    cos = cos_ref[...]
    sin = sin_ref[...]
    half = HEAD_DIM // 2

    def norm_rope(zh, gcol):
        ms = jnp.mean(zh * zh, axis=0, keepdims=True)
        y = zh * lax.rsqrt(ms + EPS) * gcol
        yr, yi = y[:half], y[half:]
        return jnp.concatenate([yr * cos - yi * sin, yr * sin + yi * cos], axis=0)

    def l2(v):
        return jnp.sqrt(jnp.sum(v * v, axis=0, keepdims=True))

    stats_ref[...] = jnp.zeros(stats_ref.shape, F32)
    for h in range(A_Q_HEADS):
        sl = slice(h * HEAD_DIM, (h + 1) * HEAD_DIM)
        q = norm_rope(zT[sl], gq_ref[...])
        qT_ref[sl, :] = q.astype(BF16)
        stats_ref[h // A_GROUP, h % A_GROUP:h % A_GROUP + 1, :] = l2(q)
    ks = [norm_rope(zT[A_Q_W + h * HEAD_DIM:A_Q_W + (h + 1) * HEAD_DIM], gk_ref[...])
          for h in range(A_KV_HEADS)]
    for h in range(A_KV_HEADS):
        stats_ref[h, A_GROUP:A_GROUP + 1, :] = l2(ks[h])
    kT = jnp.concatenate(ks, axis=0)
    k_nat = kT.T
    for h in range(A_KV_HEADS):
        k_ref[h] = k_nat[:, h * HEAD_DIM:(h + 1) * HEAD_DIM].astype(BF16)
    for h in range(A_KV_HEADS):
        for c in range(vT_ref.shape[1]):
            vT_ref[h, c] = zT[A_Q_W + A_KV_W + h * HEAD_DIM:A_Q_W + A_KV_W + (h + 1) * HEAD_DIM,
                              c * A_TK:(c + 1) * A_TK].astype(BF16)
    zb_ref[0] = _dot(hb, wb_ref[0]).astype(BF16)
    for g in range(1, B_BRANCHES):
        zb_ref[g] = _dot(perm_ref[g - 1], _dot(hb, wb_ref[g]).astype(BF16)).astype(BF16)


def _inproj_ab(x, g_mix, waT, wb, perm, gq, gk, cosT, sinT, seq):
    t = x.shape[0]
    tm = TOKEN_TILE
    n_seq_tiles = seq // tm
    return pl.pallas_call(
        _inproj_ab_kernel,
        grid=(t // tm,),
        in_specs=[
            pl.BlockSpec((tm, D_MODEL), lambda i: (i, 0)),
            _const_spec((1, D_MODEL)),
            _const_spec((A_W, D_MODEL)),
            _const_spec((B_BRANCHES, D_MODEL, 3 * B_W)),
            _const_spec((B_BRANCHES - 1, tm, tm)),
            _const_spec((HEAD_DIM, 1)),
            _const_spec((HEAD_DIM, 1)),
            pl.BlockSpec((HEAD_DIM // 2, tm), lambda i: (0, i % n_seq_tiles)),
            pl.BlockSpec((HEAD_DIM // 2, tm), lambda i: (0, i % n_seq_tiles)),
        ],
        out_specs=[
            pl.BlockSpec((A_Q_W, tm), lambda i: (0, i)),
            pl.BlockSpec((A_KV_HEADS, tm, HEAD_DIM), lambda i: (0, i, 0)),
            pl.BlockSpec((A_KV_HEADS, tm // A_TK, HEAD_DIM, A_TK), lambda i: (0, i, 0, 0)),
            pl.BlockSpec((A_KV_HEADS, 8, tm), lambda i: (0, 0, i)),
            pl.BlockSpec((B_BRANCHES, tm, 3 * B_W), lambda i: (0, i, 0)),
        ],
        out_shape=[
            jax.ShapeDtypeStruct((A_Q_W, t), BF16),
            jax.ShapeDtypeStruct((A_KV_HEADS, t, HEAD_DIM), BF16),
            jax.ShapeDtypeStruct((A_KV_HEADS, t // A_TK, HEAD_DIM, A_TK), BF16),
            jax.ShapeDtypeStruct((A_KV_HEADS, 8, t), F32),
            jax.ShapeDtypeStruct((B_BRANCHES, t, 3 * B_W), BF16),
        ],
        compiler_params=_cparams(1),
        name="inproj_ab",
    )(x, g_mix, waT, wb, perm, gq, gk, cosT, sinT)


def _gqa_kernel(qT_ref, k_ref, vT_ref, o_ref, s_scr, p_scr, cm_scr, alpha_scr, m_scr, acc_scr):
    tq = qT_ref.shape[1]
    tk = A_TK
    n_chunks = vT_ref.shape[0]
    ones = jnp.ones((A_V_ROWS - HEAD_DIM, tk), BF16)

    def scores(chunk, slot):
        k = k_ref[pl.ds(pl.multiple_of(chunk * tk, tk), tk), :]
        for h in range(A_GROUP):
            s = _dot(k, qT_ref[h * HEAD_DIM:(h + 1) * HEAD_DIM, :])
            s_scr[slot, h] = s
            cm_scr[slot, h] = jnp.max(s.reshape(tk // 8, 8, tq), axis=0)

    def probs(slot):
        for h in range(A_GROUP):
            m_old = m_scr[h]
            m_new = jnp.maximum(m_old, jnp.max(cm_scr[slot, h], axis=0, keepdims=True))
            alpha_scr[slot, h] = jnp.exp2(m_old - m_new)
            m_scr[h] = m_new
            p_scr[slot, h] = jnp.exp2(s_scr[slot, h] - m_new).astype(BF16)

    def accumulate(chunk, slot):
        vT_aug = jnp.concatenate([vT_ref[chunk], ones], axis=0)
        for h in range(A_GROUP):
            acc_scr[h] = alpha_scr[slot, h] * acc_scr[h] + _dot(vT_aug, p_scr[slot, h])

    def steady(c, count):
        for u in range(count):
            accumulate(c + u, u % 2)
            probs((u + 1) % 2)
            scores(c + u + 2, u % 2)

    m_scr[...] = jnp.full(m_scr.shape, NEG, F32)
    acc_scr[...] = jnp.zeros(acc_scr.shape, F32)
    scores(0, 0)
    scores(1, 1)
    probs(0)
    n_loop = (n_chunks - 2) // A_UNROLL

    def body(i, carry):
        steady(i * A_UNROLL, A_UNROLL)
        return carry

    if n_loop > 1:
        lax.fori_loop(0, n_loop, body, 0)
    else:
        steady(0, n_loop * A_UNROLL)
    steady(n_loop * A_UNROLL, (n_chunks - 2) % A_UNROLL)
    accumulate(n_chunks - 2, 0)
    probs(1)
    accumulate(n_chunks - 1, 1)
    oT = jnp.concatenate(
        [acc_scr[h, :HEAD_DIM, :] / acc_scr[h, HEAD_DIM:HEAD_DIM + 1, :] for h in range(A_GROUP)], axis=0)
    o_ref[...] = oT.T.astype(BF16)


def _gqa_bounded_kernel(qT_ref, k_ref, vT_ref, shift_ref, o_ref, p_scr, acc_scr):
    tq = qT_ref.shape[1]
    tk = A_TK
    n_chunks = vT_ref.shape[0]
    ones = jnp.ones((A_V_ROWS - HEAD_DIM, tk), BF16)

    def probs(chunk, slot):
        k = k_ref[pl.ds(pl.multiple_of(chunk * tk, tk), tk), :]
        for h in range(A_GROUP):
            s = _dot(k, qT_ref[h * HEAD_DIM:(h + 1) * HEAD_DIM, :])
            p_scr[slot, h] = jnp.exp2(s - shift_ref[h:h + 1, :]).astype(BF16)

    def accumulate(chunk, slot):
        vT_aug = jnp.concatenate([vT_ref[chunk], ones], axis=0)
        for h in range(A_GROUP):
            acc_scr[h] += _dot(vT_aug, p_scr[slot, h])

    def steady(c, count):
        for u in range(count):
            accumulate(c + u, u % 2)
            probs(c + u + 2, u % 2)

    acc_scr[...] = jnp.zeros(acc_scr.shape, F32)
    probs(0, 0)
    probs(1, 1)
    n_loop = (n_chunks - 2) // A_UNROLL_BOUNDED

    def body(i, carry):
        steady(i * A_UNROLL_BOUNDED, A_UNROLL_BOUNDED)
        return carry

    if n_loop > 1:
        lax.fori_loop(0, n_loop, body, 0)
    else:
        steady(0, n_loop * A_UNROLL_BOUNDED)
    steady(n_loop * A_UNROLL_BOUNDED, (n_chunks - 2) % A_UNROLL_BOUNDED)
    accumulate(n_chunks - 2, 0)
    accumulate(n_chunks - 1, 1)
    oT = jnp.concatenate(
        [acc_scr[h, :HEAD_DIM, :] / acc_scr[h, HEAD_DIM:HEAD_DIM + 1, :] for h in range(A_GROUP)], axis=0)
    o_ref[...] = oT.T.astype(BF16)


def _gqa_attention(qT, k, vT, stats, batch, seq, tq=A_TQ):
    t = batch * seq
    nq = seq // tq
    k4 = k.reshape(A_KV_HEADS, batch, seq, HEAD_DIM)
    v5 = vT.reshape(A_KV_HEADS, batch, seq // A_TK, HEAD_DIM, A_TK)
    kmax = jnp.max(stats[:, A_GROUP].reshape(A_KV_HEADS, batch, seq), axis=-1)
    bound = (stats.reshape(A_KV_HEADS, 8, batch, seq) * kmax[:, None, :, None]
             * A_BOUND_SLACK).reshape(A_KV_HEADS, 8, t)
    in_specs = [
        pl.BlockSpec((A_GROUP * HEAD_DIM, tq), lambda b, g, i: (g, b * nq + i)),
        pl.BlockSpec((None, None, seq, HEAD_DIM), lambda b, g, i: (g, b, 0, 0)),
        pl.BlockSpec((None, None, seq // A_TK, HEAD_DIM, A_TK), lambda b, g, i: (g, b, 0, 0, 0)),
    ]
    common = dict(
        grid=(batch, A_KV_HEADS, nq),
        out_specs=pl.BlockSpec((tq, A_GROUP * HEAD_DIM), lambda b, g, i: (b * nq + i, g)),
        out_shape=jax.ShapeDtypeStruct((t, A_Q_W), BF16),
        compiler_params=_cparams(3),
    )

    def bounded():
        return pl.pallas_call(
            _gqa_bounded_kernel,
            in_specs=in_specs + [pl.BlockSpec((None, 8, tq), lambda b, g, i: (g, 0, b * nq + i))],
            scratch_shapes=[pltpu.VMEM((2, A_GROUP, A_TK, tq), BF16),
                            pltpu.VMEM((A_GROUP, A_V_ROWS, tq), F32)],
            name="gqa_bounded", **common,
        )(qT, k4, v5, bound - A_SHIFT_MARGIN)

    def online():
        return pl.pallas_call(
            _gqa_kernel,
            in_specs=in_specs,
            scratch_shapes=[pltpu.VMEM((2, A_GROUP, A_TK, tq), F32),
                            pltpu.VMEM((2, A_GROUP, A_TK, tq), BF16),
                            pltpu.VMEM((2, A_GROUP, 8, tq), F32),
                            pltpu.VMEM((2, A_GROUP, 1, tq), F32),
                            pltpu.VMEM((A_GROUP, 1, tq), F32),
                            pltpu.VMEM((A_GROUP, A_V_ROWS, tq), F32)],
            name="gqa_attention", **common,
        )(qT, k4, v5)

    safe = jnp.max(bound[:, :A_GROUP]) <= A_BOUND_LIMIT
    return lax.cond(safe, bounded, online)


def _tile_perm(dil):
    c = TOKEN_TILE // dil
    dst = np.arange(TOKEN_TILE)
    src = (dst % c) * dil + dst // c
    p = np.zeros((TOKEN_TILE, TOKEN_TILE), np.float32)
    p[dst, src] = 1.0
    return p


def _seq_rows(ref, start, n, cols):
    c = ref.shape[1]
    if n <= c:
        return ref[start // c, start % c:start % c + n, cols]
    blocks = ref[start // c:(start + n) // c, :, cols]
    return blocks.reshape(n, blocks.shape[-1])


def _store_seq_rows(ref, start, cols, val):
    c = ref.shape[1]
    n = val.shape[0]
    if n <= c:
        ref[start // c, start % c:start % c + n, cols] = val
    else:
        ref[start // c:(start + n) // c, :, cols] = val.reshape(n // c, c, val.shape[-1])


def _dilated_kernel(main_ref, prev_ref, next_ref, bias_ref, r_ref, kv_scr, *, seq_len, cq):
    n = pl.program_id(2)
    kv_cols = slice(B_W, 3 * B_W)
    pair_w = 2 * HEAD_DIM
    lane = lax.broadcasted_iota(jnp.int32, (1, pair_w), 1)
    lo = lane < HEAD_DIM
    key_col = lax.broadcasted_iota(jnp.int32, (1, B_KEYS), 1)

    for rr in range(main_ref.shape[1]):
        main, res = main_ref.at[:, rr], r_ref.at[:, rr]
        kv_scr[0:B_HALF, :] = prev_ref[..., rr, :, kv_cols].reshape(B_HALF, 2 * B_W)
        for j in range(cq // B_SUB):
            kv_scr[B_HALF + j * B_SUB:B_HALF + (j + 1) * B_SUB, :] = _seq_rows(main, j * B_SUB, B_SUB, kv_cols)
        kv_scr[B_HALF + cq:, :] = next_ref[..., rr, :, kv_cols].reshape(B_HALF, 2 * B_W)

        for j in range(cq // B_SUB):
            key_pos = key_col + (n * cq + j * B_SUB - B_HALF)
            edge = jnp.where((key_pos >= 0) & (key_pos < seq_len), 0.0, NEG).astype(F32)
            krows = slice(j * B_SUB, j * B_SUB + B_KEYS)
            lses = []
            for pr in range(B_HEADS // 2):
                cols = slice(pr * pair_w, (pr + 1) * pair_w)
                q2 = _seq_rows(main, j * B_SUB, B_SUB, cols)
                k2 = kv_scr[krows, cols]
                v2 = kv_scr[krows, B_W + pr * pair_w:B_W + (pr + 1) * pair_w]
                o_pair = []
                for hh in range(2):
                    qm = jnp.where(lo if hh == 0 else ~lo, q2, jnp.zeros_like(q2))
                    s = _dot_nt(qm, k2) + bias_ref[pr * 2 + hh] + edge
                    m = jnp.max(s, axis=-1, keepdims=True)
                    p = jnp.exp(s - m)
                    l = jnp.sum(p, axis=-1, keepdims=True)
                    o_pair.append(_dot(p.astype(BF16), v2) / l)
                    lses.append(m + jnp.log(l))
                _store_seq_rows(res, j * B_SUB, cols, jnp.where(lo, o_pair[0], o_pair[1]).astype(BF16))
            tile = lses[B_HEADS - 1]
            for head in range(B_HEADS - 2, -1, -1):
                tile = jnp.where(lane < (head + 1) * LSE_REP, lses[head], tile)
            hi = tile.astype(BF16)
            _store_seq_rows(res, j * B_SUB, slice(B_W, B_W + pair_w), hi)
            _store_seq_rows(res, j * B_SUB, slice(B_W + pair_w, B_R),
                            (tile - hi.astype(F32)).astype(BF16))


def _dilated_bias(branch):
    _, dil = B_PATTERNS[branch]
    slopes = np.exp2(-8.0 * np.arange(1, B_BRANCHES * B_HEADS + 1, dtype=np.float64)
                     / (B_BRANCHES * B_HEADS)).reshape(B_BRANCHES, B_HEADS)[branch]
    a = np.arange(B_SUB)[:, None]
    c = np.arange(B_KEYS)[None, :]
    rel = (c - B_HALF) - a
    bias = -slopes[:, None, None] * (np.abs(rel) * dil).astype(np.float64)[None]
    bias = np.where((np.abs(rel) <= B_HALF)[None], bias, NEG)
    return jnp.asarray(bias, F32)


def _dilated_branch(zb, branch, batch, seq):
    _, dil = B_PATTERNS[branch]
    t = batch * seq
    seq_len = seq // dil
    c = TOKEN_TILE // dil
    n_tiles = seq // TOKEN_TILE
    cq = min(B_CHUNK, seq_len)
    nchunk = seq_len // cq
    n_halo = seq_len // B_HALF
    per = cq // B_HALF
    rb = min(dil, max(1, B_CHUNK // cq))
    zv = zb.reshape(B_BRANCHES, batch, n_tiles, dil, c, 3 * B_W)

    def halo_spec(pos):
        if c >= B_HALF:
            sub = c // B_HALF
            z6 = zb.reshape(B_BRANCHES, batch, n_tiles, dil, sub, B_HALF, 3 * B_W)
            return z6, pl.BlockSpec((None, None, None, rb, None, B_HALF, 3 * B_W),
                                    lambda b, r, n: (branch, b, pos(n) // sub, r, pos(n) % sub, 0, 0))
        return zv, pl.BlockSpec((None, None, B_HALF // c, rb, c, 3 * B_W),
                                lambda b, r, n: (branch, b, pos(n), r, 0, 0))

    prev_arr, prev_spec = halo_spec(lambda n: jnp.maximum(n * per - 1, 0))
    next_arr, next_spec = halo_spec(lambda n: jnp.minimum((n + 1) * per, n_halo - 1))
    res = pl.pallas_call(
        functools.partial(_dilated_kernel, seq_len=seq_len, cq=cq),
        grid=(batch, dil // rb, nchunk),
        in_specs=[pl.BlockSpec((None, None, cq // c, rb, c, 3 * B_W), lambda b, r, n: (branch, b, n, r, 0, 0)),
                  prev_spec, next_spec, _const_spec((B_HEADS, B_SUB, B_KEYS))],
        out_specs=pl.BlockSpec((None, cq // c, rb, c, B_R), lambda b, r, n: (b, n, r, 0, 0)),
        out_shape=jax.ShapeDtypeStruct((batch, n_tiles, dil, c, B_R), BF16),
        scratch_shapes=[pltpu.VMEM((cq + 2 * B_HALF, 2 * B_W), BF16)],
        compiler_params=_cparams(3),
        name=f"dilated_branch{branch}",
    )(zv, prev_arr, next_arr, _dilated_bias(branch))
    return res.reshape(t, B_R)


def _merge_branches(rs, permT_ref):
    pair_w = 2 * HEAD_DIM
    lo = lax.broadcasted_iota(jnp.int32, (1, pair_w), 1) < HEAD_DIM
    tiles = [rs[0].astype(F32)] + [_dot(permT_ref[g - 1], rs[g]) for g in range(1, B_BRANCHES)]
    lses = [tl[:, B_W:B_W + pair_w] + tl[:, B_W + pair_w:] for tl in tiles]
    mx = functools.reduce(jnp.maximum, lses)
    es = [jnp.exp(l - mx) for l in lses]
    inv = 1.0 / functools.reduce(lambda u, v: u + v, es)
    pairs = []
    for pr in range(B_HEADS // 2):
        acc = None
        for tl, e in zip(tiles, es):
            w = e * inv
            w2 = jnp.where(lo, w[:, 2 * pr * LSE_REP:2 * pr * LSE_REP + 1],
                           w[:, (2 * pr + 1) * LSE_REP:(2 * pr + 1) * LSE_REP + 1])
            term = w2 * tl[:, pr * pair_w:(pr + 1) * pair_w]
            acc = term if acc is None else acc + term
        pairs.append(acc)
    return jnp.concatenate(pairs, axis=-1).astype(BF16)


def _inproj_kernel(x_ref, g_ref, w_ref, z_ref, stats_ref):
    hb = _rms(x_ref[...], g_ref[...]).astype(BF16)
    z = _dot(hb, w_ref[...])
    z_ref[...] = z.astype(BF16)
    pair_w = 2 * HEAD_DIM
    lane = lax.broadcasted_iota(jnp.int32, (1, pair_w), 1)
    lo = lane < HEAD_DIM
    out = jnp.zeros((1, pair_w), F32)
    for b in range(2 * C_W // pair_w):
        sq = z[:, b * pair_w:(b + 1) * pair_w]
        sq = sq * sq
        tot = jnp.sum(sq, axis=-1, keepdims=True)
        first = jnp.sum(jnp.where(lo, sq, 0.0), axis=-1, keepdims=True)
        m_first = jnp.max(first, axis=0, keepdims=True)
        m_second = jnp.max(tot - first, axis=0, keepdims=True)
        out = jnp.where(lane == 2 * b, m_first, jnp.where(lane == 2 * b + 1, m_second, out))
    stats_ref[...] = jnp.broadcast_to(out, stats_ref.shape)


def _inproj(x, g, w):
    t = x.shape[0]
    tm = TOKEN_TILE
    n_out = w.shape[1]
    return pl.pallas_call(
        _inproj_kernel,
        grid=(t // tm,),
        in_specs=[pl.BlockSpec((tm, D_MODEL), lambda i: (i, 0)),
                  _const_spec((1, D_MODEL)),
                  _const_spec((D_MODEL, n_out))],
        out_specs=[pl.BlockSpec((tm, n_out), lambda i: (i, 0)),
                   pl.BlockSpec((None, 8, 2 * HEAD_DIM), lambda i: (i, 0, 0))],
        out_shape=[jax.ShapeDtypeStruct((t, n_out), BF16),
                   jax.ShapeDtypeStruct((t // tm, 8, 2 * HEAD_DIM), F32)],
        compiler_params=_cparams(1),
        name="inproj_c",
    )(x, g, w)


def _natten_kernel(q_ref, k0_ref, k1_ref, k2_ref, v0_ref, v1_ref, v2_ref, bias_ref, o_ref):
    k_refs = (k0_ref, k1_ref, k2_ref)
    v_refs = (v0_ref, v1_ref, v2_ref)
    lane = lax.broadcasted_iota(jnp.int32, (1, 2 * HEAD_DIM), 1)
    lo = lane < HEAD_DIM
    kb = k0_ref.shape[0]
    for pr in range(C_HEADS // 2):
        cols = slice(pr * 2 * HEAD_DIM, (pr + 1) * 2 * HEAD_DIM)
        q2 = q_ref[:, cols]
        ks = [r[:, cols] for r in k_refs]
        vs = [r[:, cols] for r in v_refs]
        o_pair = []
        for hh in range(2):
            qm = jnp.where(lo if hh == 0 else ~lo, q2, jnp.zeros_like(q2))
            s = jnp.concatenate([_dot_nt(qm, kk) for kk in ks], axis=-1) + bias_ref[pr * 2 + hh]
            m = jnp.max(s, axis=-1, keepdims=True)
            p = jnp.exp(s - m)
            l = jnp.sum(p, axis=-1, keepdims=True)
            pb = p.astype(BF16)
            o = _dot(pb[:, :kb], vs[0])
            for i in range(1, len(vs)):
                o = o + _dot(pb[:, i * kb:(i + 1) * kb], vs[i])
            o_pair.append(o / l)
        o_ref[:, cols] = jnp.where(lo, o_pair[0], o_pair[1]).astype(BF16)


def _natten_bounded_kernel(q_ref, k0_ref, k1_ref, k2_ref, v0_ref, v1_ref, v2_ref, bias_ref, shift_ref, o_ref):
    k_refs = (k0_ref, k1_ref, k2_ref)
    v_refs = (v0_ref, v1_ref, v2_ref)
    pair_w = 2 * HEAD_DIM
    lo = lax.broadcasted_iota(jnp.int32, (1, pair_w), 1) < HEAD_DIM
    kb = k0_ref.shape[0]
    one = jnp.ones((), BF16)

    def cols_of(head):
        return slice((head // 2) * pair_w, (head // 2 + 1) * pair_w)

    def own_lanes(head):
        return lo if head % 2 == 0 else ~lo

    def scores(head):
        q2 = q_ref[:, cols_of(head)]
        qm = jnp.where(own_lanes(head), q2, jnp.zeros_like(q2))
        return jnp.concatenate([_dot_nt(qm, r[:, cols_of(head)]) for r in k_refs], axis=-1)

    def probs(head, s):
        return jnp.exp(s + (bias_ref[head] - shift_ref[head:head + 1, :])).astype(BF16)

    def output(head, p):
        r = None
        for i, v_ref in enumerate(v_refs):
            t = _dot(p[:, i * kb:(i + 1) * kb], jnp.where(own_lanes(head), v_ref[:, cols_of(head)], one))
            r = t if r is None else r + t
        l = r[:, HEAD_DIM:HEAD_DIM + 1] if head % 2 == 0 else r[:, 0:1]
        return r / l

    s = {0: scores(0), 1: scores(1)}
    p = {0: probs(0, s[0])}
    o = {}
    for h in range(C_HEADS):
        if h + 2 < C_HEADS:
            s[h + 2] = scores(h + 2)
        if h + 1 < C_HEADS:
            p[h + 1] = probs(h + 1, s[h + 1])
        o[h] = output(h, p[h])
        if h % 2 == 1:
            o_ref[:, cols_of(h)] = jnp.where(lo, o[h - 1], o[h]).astype(BF16)


def _natten_bias(rpb):
    c = np.arange(GRID_W)[:, None]
    kc = np.arange(GRID_W)[None, :]
    cstart = np.clip(c - C_WIN_COLS // 2, 0, GRID_W - C_WIN_COLS)
    col_valid = (kc >= cstart) & (kc < cstart + C_WIN_COLS)
    padded = jnp.pad(rpb.astype(F32), ((0, 0), (0, 0), (GRID_W, GRID_W)))
    base = GRID_W + C_WIN_COLS - 1
    blocks = jnp.stack([padded[:, :, base - q:base - q + GRID_W] for q in range(GRID_W)], axis=2)
    blocks = jnp.where(col_valid[None, None], blocks, NEG)
    masked = jnp.full((rpb.shape[0], GRID_W, GRID_W), NEG, F32)
    tables = []
    for off, first in ((0, lambda dr: 0), (-4, lambda dr: dr), (-8, lambda dr: 4)):
        rows = []
        for dr in range(C_ROWS_Q):
            rows.append(jnp.concatenate(
                [blocks[:, off + kri - dr + C_WIN_ROWS - 1]
                 if first(dr) <= kri < first(dr) + C_WIN_ROWS else masked
                 for kri in range(C_ROWS_K)], axis=-1))
        tables.append(jnp.concatenate(rows, axis=1))
    return jnp.stack(tables)


def _natten(z, stats, rpb, bias, batch, seq):
    t = batch * seq
    qb = C_ROWS_Q * GRID_W
    nrb = seq // qb
    nkb = C_ROWS_K // C_ROWS_Q

    def kspec(i, col):
        return pl.BlockSpec(
            (qb, C_W), lambda rb, b: (b * nrb + jnp.clip(rb - 1, 0, nrb - nkb) + i, col))

    def variant(rb):
        return jnp.where(rb == 0, 0, jnp.where(rb == nrb - 1, 2, 1))

    def call(body, *extra):
        return pl.pallas_call(
            body,
            grid=(nrb, batch),
            in_specs=[pl.BlockSpec((qb, C_W), lambda rb, b: (b * nrb + rb, 0))]
            + [kspec(i, 1) for i in range(nkb)]
            + [kspec(i, 2) for i in range(nkb)]
            + [pl.BlockSpec((None, C_HEADS, qb, C_ROWS_K * GRID_W), lambda rb, b: (variant(rb), 0, 0, 0))]
            + [_const_spec(e.shape) for e in extra],
            out_specs=pl.BlockSpec((qb, C_W), lambda rb, b: (b * nrb + rb, 0)),
            out_shape=jax.ShapeDtypeStruct((t, C_W), BF16),
            compiler_params=_cparams(2),
            name="natten",
        )(z, z, z, z, z, z, z, bias, *extra)

    norms = jnp.sqrt(jnp.max(stats[:, 0, :2 * C_HEADS], axis=0)) * C_BOUND_SLACK
    qk_bound = norms[:C_HEADS] * norms[C_HEADS:]
    rpb_max = jnp.max(rpb, axis=(1, 2))
    rpb_range = rpb_max - jnp.min(rpb, axis=(1, 2))
    safe = jnp.max(2.0 * qk_bound + rpb_range) <= C_BOUND_LIMIT
    shift = jnp.broadcast_to((qk_bound + rpb_max - C_SHIFT_MARGIN)[:, None], (C_HEADS, C_ROWS_K * GRID_W))
    return lax.cond(safe, lambda: call(_natten_bounded_kernel, shift), lambda: call(_natten_kernel))


def _mem_kv_kernel(mem_ref, g_ref, w_ref, kv_ref):
    mb = _rms(mem_ref[...], g_ref[...]).astype(BF16)
    kv_ref[...] = _dot(mb, w_ref[...]).astype(BF16)


def _mem_kv(mem, g, w):
    batch, n_mem, _ = mem.shape
    return pl.pallas_call(
        _mem_kv_kernel,
        grid=(batch,),
        in_specs=[pl.BlockSpec((None, n_mem, D_MODEL), lambda b: (b, 0, 0)),
                  _const_spec((1, D_MODEL)),
                  _const_spec((D_MODEL, 2 * D_MODEL))],
        out_specs=pl.BlockSpec((None, n_mem, 2 * D_MODEL), lambda b: (b, 0, 0)),
        out_shape=jax.ShapeDtypeStruct((batch, n_mem, 2 * D_MODEL), BF16),
        compiler_params=_cparams(1),
        name="mem_kv",
    )(mem, g, w)


def _xattn_tail(x, g_ref, wq_ref, kv_ref, wo_ref, y_ref):
    q = _dot(_rms(x, g_ref[...]).astype(BF16), wq_ref[...]).astype(BF16)
    def score(h):
        cols = slice(h * X_HEAD_DIM, (h + 1) * X_HEAD_DIM)
        return _dot_nt(q[:, cols], kv_ref[:, cols])

    def finish(h, s):
        m = jnp.max(s, axis=-1, keepdims=True)
        p = jnp.exp(s - m)
        l = jnp.sum(p, axis=-1, keepdims=True)
        o = _dot(p.astype(BF16), kv_ref[:, D_MODEL + h * X_HEAD_DIM:D_MODEL + (h + 1) * X_HEAD_DIM])
        return (o / l).astype(BF16)

    outs = []
    s = score(0)
    for h in range(X_HEADS):
        nxt = score(h + 1) if h + 1 < X_HEADS else None
        outs.append(finish(h, s))
        s = nxt
    y_ref[...] = x + _dot(jnp.concatenate(outs, axis=-1), wo_ref[...])


def _mix_ab_xattn_kernel(x_ref, oa_ref, r0_ref, r1_ref, r2_ref, permT_ref, woa_ref, wob_ref,
                         g_ref, wq_ref, kv_ref, wo_ref, y_ref):
    ob = _merge_branches([r0_ref[...], r1_ref[...], r2_ref[...]], permT_ref)
    x = x_ref[...] + _dot(oa_ref[...], woa_ref[...]) + _dot(ob, wob_ref[...])
    _xattn_tail(x, g_ref, wq_ref, kv_ref, wo_ref, y_ref)


def _mix_c_xattn_kernel(x_ref, oc_ref, woc_ref, g_ref, wq_ref, kv_ref, wo_ref, y_ref):
    x = x_ref[...] + _dot(oc_ref[...], woc_ref[...])
    _xattn_tail(x, g_ref, wq_ref, kv_ref, wo_ref, y_ref)


def _mix_xattn(body, x, token_args, const_args, g, wq, kv, wo, seq):
    t = x.shape[0]
    tm = TOKEN_TILE
    n_seq_tiles = seq // tm
    n_mem = kv.shape[1]
    token_spec = lambda a: pl.BlockSpec((tm, a.shape[1]), lambda i: (i, 0))
    in_specs = ([token_spec(x)] + [token_spec(a) for a in token_args]
                + [_const_spec(a.shape) for a in const_args]
                + [_const_spec((1, D_MODEL)),
                   _const_spec((D_MODEL, D_MODEL)),
                   pl.BlockSpec((None, n_mem, 2 * D_MODEL), lambda i: (i // n_seq_tiles, 0, 0)),
                   _const_spec((D_MODEL, D_MODEL))])
    return pl.pallas_call(
        body,
        grid=(t // tm,),
        in_specs=in_specs,
        out_specs=pl.BlockSpec((tm, D_MODEL), lambda i: (i, 0)),
        out_shape=jax.ShapeDtypeStruct((t, D_MODEL), F32),
        compiler_params=_cparams(1),
        name="mix_xattn",
    )(x, *token_args, *const_args, g, wq, kv, wo)


def _swiglu_kernel(x_ref, g_ref, wg_ref, wu_ref, wd_ref, gf_ref, y_ref, *, final_norm):
    sub = x_ref.shape[0] // FFN_SPLIT
    for i in range(FFN_SPLIT):
        rows = slice(i * sub, (i + 1) * sub)
        x = x_ref[rows, :]
        hb = _rms(x, g_ref[...]).astype(BF16)
        gate = _dot(hb, wg_ref[...])
        up = _dot(hb, wu_ref[...])
        act = (gate / (1.0 + jnp.exp(-gate)) * up).astype(BF16)
        y = x + _dot(act, wd_ref[...])
        if final_norm:
            y = _rms(y, gf_ref[...])
        y_ref[rows, :] = y


def _swiglu(x, g, wg, wu, wd, g_final, final_norm):
    t = x.shape[0]
    tm = FFN_TILE
    assert t % tm == 0
    return pl.pallas_call(
        functools.partial(_swiglu_kernel, final_norm=final_norm),
        grid=(t // tm,),
        in_specs=[pl.BlockSpec((tm, D_MODEL), lambda i: (i, 0)),
                  _const_spec((1, D_MODEL)),
                  _resident_spec((D_MODEL, D_FF)),
                  _resident_spec((D_MODEL, D_FF)),
                  _resident_spec((D_FF, D_MODEL)),
                  _const_spec((1, D_MODEL))],
        out_specs=pl.BlockSpec((tm, D_MODEL), lambda i: (i, 0)),
        out_shape=jax.ShapeDtypeStruct((t, D_MODEL), F32),
        compiler_params=_cparams(1),
        name="swiglu",
    )(x, g, wg, wu, wd, g_final)


def _rope_tables(seq):
    tok = jnp.arange(seq, dtype=jnp.int32)
    row = (tok // GRID_W).astype(F32)
    col = (tok % GRID_W).astype(F32)
    axis_dim = HEAD_DIM // 2
    inv_freq = ROPE_THETA ** (-jnp.arange(0, axis_dim, 2, dtype=F32) / axis_dim)
    ang = jnp.concatenate([row[:, None] * inv_freq, col[:, None] * inv_freq], axis=-1)
    return jnp.cos(ang).T, jnp.sin(ang).T


def _prepare_ab(w_in, g_qn, g_kn, w_out):
    perm = np.concatenate([np.arange(0, HEAD_DIM, 2), np.arange(1, HEAD_DIM, 2)])
    cols = np.concatenate(
        [h * HEAD_DIM + perm for h in range(A_Q_HEADS + A_KV_HEADS)]
        + [np.arange(A_Q_W + A_KV_W, A_W)])
    waT = w_in[:, cols].T.astype(BF16)
    zbw = w_in[:, A_W:].reshape(D_MODEL, 3, B_BRANCHES, B_W)
    qscale = HEAD_DIM ** -0.5
    wb = jnp.stack([jnp.concatenate([zbw[:, 0, g] * qscale, zbw[:, 1, g], zbw[:, 2, g]], axis=1)
                    for g in range(B_BRANCHES)]).astype(BF16)
    gq = (g_qn[perm] * (qscale * LOG2E)).reshape(HEAD_DIM, 1).astype(F32)
    gk = g_kn[perm].reshape(HEAD_DIM, 1).astype(F32)
    perms = np.stack([_tile_perm(dil) for _, dil in B_PATTERNS[1:]])
    tile_perm = jnp.asarray(perms, BF16)
    tile_perm_t = jnp.asarray(perms.transpose(0, 2, 1), BF16)
    return waT, wb, tile_perm, tile_perm_t, gq, gk, w_out[:A_Q_W].astype(BF16), w_out[A_Q_W:].astype(BF16)


def _prepare_layers(p):
    depth = p["g_mix"].shape[0]
    row = lambda g: g.reshape(1, D_MODEL)
    layers = []
    for l in range(depth):
        lay = dict(g_mix=row(p["g_mix"][l]), g_xattn=row(p["g_xattn"][l]), g_mem=row(p["g_mem"][l]),
                   g_ffn=row(p["g_ffn"][l]),
                   wq=(p["wq_x"][l] * X_HEAD_DIM ** -0.5).astype(BF16), wkv=p["wkv_x"][l].astype(BF16),
                   wo=p["wo_x"][l].astype(BF16), wg=p["w_gu"][l][:, :D_FF].astype(BF16),
                   wu=p["w_gu"][l][:, D_FF:].astype(BF16), wd=p["w_down"][l].astype(BF16))
        i = l // 2
        if l % 2 == 0:
            lay["ab"] = _prepare_ab(p["w_in_ab"][i], p["g_qn"][i], p["g_kn"][i], p["w_out_ab"][i])
        else:
            w_in = p["w_in_c"][i]
            lay["w_in_c"] = jnp.concatenate(
                [w_in[:, :C_W] * HEAD_DIM ** -0.5, w_in[:, C_W:]], axis=1).astype(BF16)
            lay["rpb_c"] = p["rpb_c"][i]
            lay["bias_c"] = _natten_bias(p["rpb_c"][i])
            lay["w_out_c"] = p["w_out_c"][i].astype(BF16)
        layers.append(lay)
    return layers


def _trunk(x, mem, layers, g_final):
    batch, seq, d_model = x.shape
    assert d_model == D_MODEL and mem.shape[0] == batch and mem.shape[2] == D_MODEL
    assert seq % A_TQ == 0 and seq % TOKEN_TILE == 0 and seq >= 2 * A_TK
    assert seq % (C_ROWS_Q * GRID_W) == 0 and seq >= C_ROWS_K * GRID_W
    assert all(seq % (dil * B_SUB) == 0 for _, dil in B_PATTERNS)
    x = x.reshape(batch * seq, D_MODEL)
    for l, lay in enumerate(layers):
        if l % 2 == 0:
            waT, wb, perm, permT, gq, gk, wo_a, wo_b = lay["ab"]
            cosT, sinT = _rope_tables(seq)
            qT, k, vT, stats, zb = _inproj_ab(x, lay["g_mix"], waT, wb, perm, gq, gk, cosT, sinT, seq)
            oa = _gqa_attention(qT, k, vT, stats, batch, seq)
            rs = [_dilated_branch(zb, g, batch, seq) for g in range(B_BRANCHES)]
            body, token_args, const_args = _mix_ab_xattn_kernel, [oa] + rs, [permT, wo_a, wo_b]
        else:
            z, stats = _inproj(x, lay["g_mix"], lay["w_in_c"])
            oc = _natten(z, stats, lay["rpb_c"], lay["bias_c"], batch, seq)
            body, token_args, const_args = _mix_c_xattn_kernel, [oc], [lay["w_out_c"]]
        kv = _mem_kv(mem, lay["g_mem"], lay["wkv"])
        x = _mix_xattn(body, x, token_args, const_args, lay["g_xattn"], lay["wq"], kv, lay["wo"], seq)
        x = _swiglu(x, lay["g_ffn"], lay["wg"], lay["wu"], lay["wd"], g_final,
                    final_norm=(l == len(layers) - 1))
    return x.reshape(batch, seq, D_MODEL)


def kernel(x_prompt, x_sample, mem_prompt, mem_sample, g_mix, w_in_ab, g_qn, g_kn, w_out_ab, w_in_c,
           rpb_c, w_out_c, g_xattn, g_mem, wq_x, wkv_x, wo_x, g_ffn, w_gu, w_down, g_final):
    layers = _prepare_layers(dict(
        g_mix=g_mix, w_in_ab=w_in_ab, g_qn=g_qn, g_kn=g_kn, w_out_ab=w_out_ab, w_in_c=w_in_c,
        rpb_c=rpb_c, w_out_c=w_out_c, g_xattn=g_xattn, g_mem=g_mem, wq_x=wq_x, wkv_x=wkv_x,
        wo_x=wo_x, g_ffn=g_ffn, w_gu=w_gu, w_down=w_down))
    g_final = g_final.reshape(1, D_MODEL)
    return (_trunk(x_prompt, mem_prompt, layers, g_final), _trunk(x_sample, mem_sample, layers, g_final))
```

```python
import functools
import math

import numpy as np
import jax
import jax.numpy as jnp
from jax import lax
from jax.experimental import pallas as pl
from jax.experimental.pallas import tpu as pltpu

F32 = jnp.float32
BF16 = jnp.bfloat16

D_MODEL = 1024
GRID_W = 64
HEAD_DIM = 64
EPS = 1e-6
NEG = -1e30
LOG2E = math.log2(math.e)
TOKEN_TILE = 512

A_Q_HEADS = 8
A_KV_HEADS = 2
A_GROUP = A_Q_HEADS // A_KV_HEADS
ROPE_THETA = 10000.0
A_Q_W = A_Q_HEADS * HEAD_DIM
A_KV_W = A_KV_HEADS * HEAD_DIM
A_W = A_Q_W + 2 * A_KV_W
A_TQ = 1024
A_TK = 256
A_V_ROWS = HEAD_DIM + 16
A_UNROLL = 6
A_UNROLL_BOUNDED = 16
A_SHIFT_MARGIN = 60.0
A_BOUND_LIMIT = 90.0
A_BOUND_SLACK = 1.0 + 2.0 ** -6

B_PATTERNS = ((128, 1), (512, 4), (2048, 16))
B_BRANCHES = len(B_PATTERNS)
B_HEADS = 4
B_HALF = 64
B_W = B_HEADS * HEAD_DIM
B_SUB = 2 * B_HALF
B_KEYS = 4 * B_HALF
B_CHUNK = 1024
LSE_REP = 2 * HEAD_DIM // B_HEADS
B_R = B_W + 4 * HEAD_DIM

C_HEADS = 16
C_WIN_ROWS = 8
C_WIN_COLS = 16
C_ROWS_Q = 4
C_ROWS_K = 12
C_W = C_HEADS * HEAD_DIM
C_SHIFT_MARGIN = 40.0
C_BOUND_LIMIT = 120.0
C_BOUND_SLACK = 1.0 + 2.0 ** -6

X_HEADS = 4
X_HEAD_DIM = D_MODEL // X_HEADS
D_FF = 2816
FFN_TILE = 1024
FFN_SPLIT = 4

VMEM_LIMIT = 56 * 1024 * 1024


def _cparams(n_axes):
    return pltpu.CompilerParams(dimension_semantics=("arbitrary",) * n_axes,
                                vmem_limit_bytes=VMEM_LIMIT)


def _rms(x, g):
    ms = jnp.mean(x * x, axis=-1, keepdims=True)
    return x * lax.rsqrt(ms + EPS) * g


def _dot(a, b):
    return jnp.dot(a, b, preferred_element_type=F32)


def _dot_nt(a, b):
    return lax.dot_general(a, b, (((1,), (1,)), ((), ())), preferred_element_type=F32)


def _const_spec(shape):
    zeros = (0,) * len(shape)
    return pl.BlockSpec(shape, lambda *_: zeros)


def _resident_spec(shape):
    zeros = (0,) * len(shape)
    return pl.BlockSpec(shape, lambda *_: zeros, pipeline_mode=pl.Buffered(1))


def _inproj_ab_kernel(x_ref, g_ref, waT_ref, wb_ref, perm_ref, gq_ref, gk_ref, cos_ref, sin_ref,
                      qT_ref, k_ref, vT_ref, stats_ref, zb_ref):
    hb = _rms(x_ref[...], g_ref[...]).astype(BF16)
    zT = _dot_nt(waT_ref[...], hb)
    cos = cos_ref[...]
    sin = sin_ref[...]
    half = HEAD_DIM // 2

    def norm_rope(zh, gcol):
        ms = jnp.mean(zh * zh, axis=0, keepdims=True)
        y = zh * lax.rsqrt(ms + EPS) * gcol
        yr, yi = y[:half], y[half:]
        return jnp.concatenate([yr * cos - yi * sin, yr * sin + yi * cos], axis=0)

    def l2(v):
        return jnp.sqrt(jnp.sum(v * v, axis=0, keepdims=True))

    stats_ref[...] = jnp.zeros(stats_ref.shape, F32)
    for h in range(A_Q_HEADS):
        sl = slice(h * HEAD_DIM, (h + 1) * HEAD_DIM)
        q = norm_rope(zT[sl], gq_ref[...])
        qT_ref[sl, :] = q.astype(BF16)
        stats_ref[h // A_GROUP, h % A_GROUP:h % A_GROUP + 1, :] = l2(q)
    ks = [norm_rope(zT[A_Q_W + h * HEAD_DIM:A_Q_W + (h + 1) * HEAD_DIM], gk_ref[...])
          for h in range(A_KV_HEADS)]
    for h in range(A_KV_HEADS):
        stats_ref[h, A_GROUP:A_GROUP + 1, :] = l2(ks[h])
    kT = jnp.concatenate(ks, axis=0)
    k_nat = kT.T
    for h in range(A_KV_HEADS):
        k_ref[h] = k_nat[:, h * HEAD_DIM:(h + 1) * HEAD_DIM].astype(BF16)
    for h in range(A_KV_HEADS):
        for c in range(vT_ref.shape[1]):
            vT_ref[h, c] = zT[A_Q_W + A_KV_W + h * HEAD_DIM:A_Q_W + A_KV_W + (h + 1) * HEAD_DIM,
                              c * A_TK:(c + 1) * A_TK].astype(BF16)
    zb_ref[0] = _dot(hb, wb_ref[0]).astype(BF16)
    for g in range(1, B_BRANCHES):
        zb_ref[g] = _dot(perm_ref[g - 1], _dot(hb, wb_ref[g]).astype(BF16)).astype(BF16)


def _inproj_ab(x, g_mix, waT, wb, perm, gq, gk, cosT, sinT, seq):
    t = x.shape[0]
    tm = TOKEN_TILE
    n_seq_tiles = seq // tm
    return pl.pallas_call(
        _inproj_ab_kernel,
        grid=(t // tm,),
        in_specs=[
            pl.BlockSpec((tm, D_MODEL), lambda i: (i, 0)),
            _const_spec((1, D_MODEL)),
            _const_spec((A_W, D_MODEL)),
            _const_spec((B_BRANCHES, D_MODEL, 3 * B_W)),
            _const_spec((B_BRANCHES - 1, tm, tm)),
            _const_spec((HEAD_DIM, 1)),
            _const_spec((HEAD_DIM, 1)),
            pl.BlockSpec((HEAD_DIM // 2, tm), lambda i: (0, i % n_seq_tiles)),
            pl.BlockSpec((HEAD_DIM // 2, tm), lambda i: (0, i % n_seq_tiles)),
        ],
        out_specs=[
            pl.BlockSpec((A_Q_W, tm), lambda i: (0, i)),
            pl.BlockSpec((A_KV_HEADS, tm, HEAD_DIM), lambda i: (0, i, 0)),
            pl.BlockSpec((A_KV_HEADS, tm // A_TK, HEAD_DIM, A_TK), lambda i: (0, i, 0, 0)),
            pl.BlockSpec((A_KV_HEADS, 8, tm), lambda i: (0, 0, i)),
            pl.BlockSpec((B_BRANCHES, tm, 3 * B_W), lambda i: (0, i, 0)),
        ],
        out_shape=[
            jax.ShapeDtypeStruct((A_Q_W, t), BF16),
            jax.ShapeDtypeStruct((A_KV_HEADS, t, HEAD_DIM), BF16),
            jax.ShapeDtypeStruct((A_KV_HEADS, t // A_TK, HEAD_DIM, A_TK), BF16),
            jax.ShapeDtypeStruct((A_KV_HEADS, 8, t), F32),
            jax.ShapeDtypeStruct((B_BRANCHES, t, 3 * B_W), BF16),
        ],
        compiler_params=_cparams(1),
        name="inproj_ab",
    )(x, g_mix, waT, wb, perm, gq, gk, cosT, sinT)


def _gqa_kernel(qT_ref, k_ref, vT_ref, o_ref, s_scr, p_scr, cm_scr, alpha_scr, m_scr, acc_scr):
    tq = qT_ref.shape[1]
    tk = A_TK
    n_chunks = vT_ref.shape[0]
    ones = jnp.ones((A_V_ROWS - HEAD_DIM, tk), BF16)

    def scores(chunk, slot):
        k = k_ref[pl.ds(pl.multiple_of(chunk * tk, tk), tk), :]
        for h in range(A_GROUP):
            s = _dot(k, qT_ref[h * HEAD_DIM:(h + 1) * HEAD_DIM, :])
            s_scr[slot, h] = s
            cm_scr[slot, h] = jnp.max(s.reshape(tk // 8, 8, tq), axis=0)

    def probs(slot):
        for h in range(A_GROUP):
            m_old = m_scr[h]
            m_new = jnp.maximum(m_old, jnp.max(cm_scr[slot, h], axis=0, keepdims=True))
            alpha_scr[slot, h] = jnp.exp2(m_old - m_new)
            m_scr[h] = m_new
            p_scr[slot, h] = jnp.exp2(s_scr[slot, h] - m_new).astype(BF16)

    def accumulate(chunk, slot):
        vT_aug = jnp.concatenate([vT_ref[chunk], ones], axis=0)
        for h in range(A_GROUP):
            acc_scr[h] = alpha_scr[slot, h] * acc_scr[h] + _dot(vT_aug, p_scr[slot, h])

    def steady(c, count):
        for u in range(count):
            accumulate(c + u, u % 2)
            probs((u + 1) % 2)
            scores(c + u + 2, u % 2)

    m_scr[...] = jnp.full(m_scr.shape, NEG, F32)
    acc_scr[...] = jnp.zeros(acc_scr.shape, F32)
    scores(0, 0)
    scores(1, 1)
    probs(0)
    n_loop = (n_chunks - 2) // A_UNROLL

    def body(i, carry):
        steady(i * A_UNROLL, A_UNROLL)
        return carry

    if n_loop > 1:
        lax.fori_loop(0, n_loop, body, 0)
    else:
        steady(0, n_loop * A_UNROLL)
    steady(n_loop * A_UNROLL, (n_chunks - 2) % A_UNROLL)
    accumulate(n_chunks - 2, 0)
    probs(1)
    accumulate(n_chunks - 1, 1)
    oT = jnp.concatenate(
        [acc_scr[h, :HEAD_DIM, :] / acc_scr[h, HEAD_DIM:HEAD_DIM + 1, :] for h in range(A_GROUP)], axis=0)
    o_ref[...] = oT.T.astype(BF16)


def _gqa_bounded_kernel(qT_ref, k_ref, vT_ref, shift_ref, o_ref, p_scr, acc_scr):
    tq = qT_ref.shape[1]
    tk = A_TK
    n_chunks = vT_ref.shape[0]
    ones = jnp.ones((A_V_ROWS - HEAD_DIM, tk), BF16)

    def probs(chunk, slot):
        k = k_ref[pl.ds(pl.multiple_of(chunk * tk, tk), tk), :]
        for h in range(A_GROUP):
            s = _dot(k, qT_ref[h * HEAD_DIM:(h + 1) * HEAD_DIM, :])
            p_scr[slot, h] = jnp.exp2(s - shift_ref[h:h + 1, :]).astype(BF16)

    def accumulate(chunk, slot):
        vT_aug = jnp.concatenate([vT_ref[chunk], ones], axis=0)
        for h in range(A_GROUP):
            acc_scr[h] += _dot(vT_aug, p_scr[slot, h])

    def steady(c, count):
        for u in range(count):
            accumulate(c + u, u % 2)
            probs(c + u + 2, u % 2)

    acc_scr[...] = jnp.zeros(acc_scr.shape, F32)
    probs(0, 0)
    probs(1, 1)
    n_loop = (n_chunks - 2) // A_UNROLL_BOUNDED

    def body(i, carry):
        steady(i * A_UNROLL_BOUNDED, A_UNROLL_BOUNDED)
        return carry

    if n_loop > 1:
        lax.fori_loop(0, n_loop, body, 0)
    else:
        steady(0, n_loop * A_UNROLL_BOUNDED)
    steady(n_loop * A_UNROLL_BOUNDED, (n_chunks - 2) % A_UNROLL_BOUNDED)
    accumulate(n_chunks - 2, 0)
    accumulate(n_chunks - 1, 1)
    oT = jnp.concatenate(
        [acc_scr[h, :HEAD_DIM, :] / acc_scr[h, HEAD_DIM:HEAD_DIM + 1, :] for h in range(A_GROUP)], axis=0)
    o_ref[...] = oT.T.astype(BF16)


def _gqa_attention(qT, k, vT, stats, batch, seq, tq=A_TQ):
    t = batch * seq
    nq = seq // tq
    k4 = k.reshape(A_KV_HEADS, batch, seq, HEAD_DIM)
    v5 = vT.reshape(A_KV_HEADS, batch, seq // A_TK, HEAD_DIM, A_TK)
    kmax = jnp.max(stats[:, A_GROUP].reshape(A_KV_HEADS, batch, seq), axis=-1)
    bound = (stats.reshape(A_KV_HEADS, 8, batch, seq) * kmax[:, None, :, None]
             * A_BOUND_SLACK).reshape(A_KV_HEADS, 8, t)
    in_specs = [
        pl.BlockSpec((A_GROUP * HEAD_DIM, tq), lambda b, g, i: (g, b * nq + i)),
        pl.BlockSpec((None, None, seq, HEAD_DIM), lambda b, g, i: (g, b, 0, 0)),
        pl.BlockSpec((None, None, seq // A_TK, HEAD_DIM, A_TK), lambda b, g, i: (g, b, 0, 0, 0)),
    ]
    common = dict(
        grid=(batch, A_KV_HEADS, nq),
        out_specs=pl.BlockSpec((tq, A_GROUP * HEAD_DIM), lambda b, g, i: (b * nq + i, g)),
        out_shape=jax.ShapeDtypeStruct((t, A_Q_W), BF16),
        compiler_params=_cparams(3),
    )

    def bounded():
        return pl.pallas_call(
            _gqa_bounded_kernel,
            in_specs=in_specs + [pl.BlockSpec((None, 8, tq), lambda b, g, i: (g, 0, b * nq + i))],
            scratch_shapes=[pltpu.VMEM((2, A_GROUP, A_TK, tq), BF16),
                            pltpu.VMEM((A_GROUP, A_V_ROWS, tq), F32)],
            name="gqa_bounded", **common,
        )(qT, k4, v5, bound - A_SHIFT_MARGIN)

    def online():
        return pl.pallas_call(
            _gqa_kernel,
            in_specs=in_specs,
            scratch_shapes=[pltpu.VMEM((2, A_GROUP, A_TK, tq), F32),
                            pltpu.VMEM((2, A_GROUP, A_TK, tq), BF16),
                            pltpu.VMEM((2, A_GROUP, 8, tq), F32),
                            pltpu.VMEM((2, A_GROUP, 1, tq), F32),
                            pltpu.VMEM((A_GROUP, 1, tq), F32),
                            pltpu.VMEM((A_GROUP, A_V_ROWS, tq), F32)],
            name="gqa_attention", **common,
        )(qT, k4, v5)

    safe = jnp.max(bound[:, :A_GROUP]) <= A_BOUND_LIMIT
    return lax.cond(safe, bounded, online)


def _tile_perm(dil):
    c = TOKEN_TILE // dil
    dst = np.arange(TOKEN_TILE)
    src = (dst % c) * dil + dst // c
    p = np.zeros((TOKEN_TILE, TOKEN_TILE), np.float32)
    p[dst, src] = 1.0
    return p


def _seq_rows(ref, start, n, cols):
    c = ref.shape[1]
    if n <= c:
        return ref[start // c, start % c:start % c + n, cols]
    blocks = ref[start // c:(start + n) // c, :, cols]
    return blocks.reshape(n, blocks.shape[-1])


def _store_seq_rows(ref, start, cols, val):
    c = ref.shape[1]
    n = val.shape[0]
    if n <= c:
        ref[start // c, start % c:start % c + n, cols] = val
    else:
        ref[start // c:(start + n) // c, :, cols] = val.reshape(n // c, c, val.shape[-1])


def _dilated_kernel(main_ref, prev_ref, next_ref, bias_ref, r_ref, kv_scr, *, seq_len, cq):
    n = pl.program_id(2)
    kv_cols = slice(B_W, 3 * B_W)
    pair_w = 2 * HEAD_DIM
    lane = lax.broadcasted_iota(jnp.int32, (1, pair_w), 1)
    lo = lane < HEAD_DIM
    key_col = lax.broadcasted_iota(jnp.int32, (1, B_KEYS), 1)

    for rr in range(main_ref.shape[1]):
        main, res = main_ref.at[:, rr], r_ref.at[:, rr]
        kv_scr[0:B_HALF, :] = prev_ref[..., rr, :, kv_cols].reshape(B_HALF, 2 * B_W)
        for j in range(cq // B_SUB):
            kv_scr[B_HALF + j * B_SUB:B_HALF + (j + 1) * B_SUB, :] = _seq_rows(main, j * B_SUB, B_SUB, kv_cols)
        kv_scr[B_HALF + cq:, :] = next_ref[..., rr, :, kv_cols].reshape(B_HALF, 2 * B_W)

        for j in range(cq // B_SUB):
            key_pos = key_col + (n * cq + j * B_SUB - B_HALF)
            edge = jnp.where((key_pos >= 0) & (key_pos < seq_len), 0.0, NEG).astype(F32)
            krows = slice(j * B_SUB, j * B_SUB + B_KEYS)
            lses = []
            for pr in range(B_HEADS // 2):
                cols = slice(pr * pair_w, (pr + 1) * pair_w)
                q2 = _seq_rows(main, j * B_SUB, B_SUB, cols)
                k2 = kv_scr[krows, cols]
                v2 = kv_scr[krows, B_W + pr * pair_w:B_W + (pr + 1) * pair_w]
                o_pair = []
                for hh in range(2):
                    qm = jnp.where(lo if hh == 0 else ~lo, q2, jnp.zeros_like(q2))
                    s = _dot_nt(qm, k2) + bias_ref[pr * 2 + hh] + edge
                    m = jnp.max(s, axis=-1, keepdims=True)
                    p = jnp.exp(s - m)
                    l = jnp.sum(p, axis=-1, keepdims=True)
                    o_pair.append(_dot(p.astype(BF16), v2) / l)
                    lses.append(m + jnp.log(l))
                _store_seq_rows(res, j * B_SUB, cols, jnp.where(lo, o_pair[0], o_pair[1]).astype(BF16))
            tile = lses[B_HEADS - 1]
            for head in range(B_HEADS - 2, -1, -1):
                tile = jnp.where(lane < (head + 1) * LSE_REP, lses[head], tile)
            hi = tile.astype(BF16)
            _store_seq_rows(res, j * B_SUB, slice(B_W, B_W + pair_w), hi)
            _store_seq_rows(res, j * B_SUB, slice(B_W + pair_w, B_R),
                            (tile - hi.astype(F32)).astype(BF16))


def _dilated_bias(branch):
    _, dil = B_PATTERNS[branch]
    slopes = np.exp2(-8.0 * np.arange(1, B_BRANCHES * B_HEADS + 1, dtype=np.float64)
                     / (B_BRANCHES * B_HEADS)).reshape(B_BRANCHES, B_HEADS)[branch]
    a = np.arange(B_SUB)[:, None]
    c = np.arange(B_KEYS)[None, :]
    rel = (c - B_HALF) - a
    bias = -slopes[:, None, None] * (np.abs(rel) * dil).astype(np.float64)[None]
    bias = np.where((np.abs(rel) <= B_HALF)[None], bias, NEG)
    return jnp.asarray(bias, F32)


def _dilated_branch(zb, branch, batch, seq):
    _, dil = B_PATTERNS[branch]
    t = batch * seq
    seq_len = seq // dil
    c = TOKEN_TILE // dil
    n_tiles = seq // TOKEN_TILE
    cq = min(B_CHUNK, seq_len)
    nchunk = seq_len // cq
    n_halo = seq_len // B_HALF
    per = cq // B_HALF
    rb = min(dil, max(1, B_CHUNK // cq))
    zv = zb.reshape(B_BRANCHES, batch, n_tiles, dil, c, 3 * B_W)

    def halo_spec(pos):
        if c >= B_HALF:
            sub = c // B_HALF
            z6 = zb.reshape(B_BRANCHES, batch, n_tiles, dil, sub, B_HALF, 3 * B_W)
            return z6, pl.BlockSpec((None, None, None, rb, None, B_HALF, 3 * B_W),
                                    lambda b, r, n: (branch, b, pos(n) // sub, r, pos(n) % sub, 0, 0))
        return zv, pl.BlockSpec((None, None, B_HALF // c, rb, c, 3 * B_W),
                                lambda b, r, n: (branch, b, pos(n), r, 0, 0))

    prev_arr, prev_spec = halo_spec(lambda n: jnp.maximum(n * per - 1, 0))
    next_arr, next_spec = halo_spec(lambda n: jnp.minimum((n + 1) * per, n_halo - 1))
    res = pl.pallas_call(
        functools.partial(_dilated_kernel, seq_len=seq_len, cq=cq),
        grid=(batch, dil // rb, nchunk),
        in_specs=[pl.BlockSpec((None, None, cq // c, rb, c, 3 * B_W), lambda b, r, n: (branch, b, n, r, 0, 0)),
                  prev_spec, next_spec, _const_spec((B_HEADS, B_SUB, B_KEYS))],
        out_specs=pl.BlockSpec((None, cq // c, rb, c, B_R), lambda b, r, n: (b, n, r, 0, 0)),
        out_shape=jax.ShapeDtypeStruct((batch, n_tiles, dil, c, B_R), BF16),
        scratch_shapes=[pltpu.VMEM((cq + 2 * B_HALF, 2 * B_W), BF16)],
        compiler_params=_cparams(3),
        name=f"dilated_branch{branch}",
    )(zv, prev_arr, next_arr, _dilated_bias(branch))
    return res.reshape(t, B_R)


def _merge_branches(rs, permT_ref):
    pair_w = 2 * HEAD_DIM
    lo = lax.broadcasted_iota(jnp.int32, (1, pair_w), 1) < HEAD_DIM
    tiles = [rs[0].astype(F32)] + [_dot(permT_ref[g - 1], rs[g]) for g in range(1, B_BRANCHES)]
    lses = [tl[:, B_W:B_W + pair_w] + tl[:, B_W + pair_w:] for tl in tiles]
    mx = functools.reduce(jnp.maximum, lses)
    es = [jnp.exp(l - mx) for l in lses]
    inv = 1.0 / functools.reduce(lambda u, v: u + v, es)
    pairs = []
    for pr in range(B_HEADS // 2):
        acc = None
        for tl, e in zip(tiles, es):
            w = e * inv
            w2 = jnp.where(lo, w[:, 2 * pr * LSE_REP:2 * pr * LSE_REP + 1],
                           w[:, (2 * pr + 1) * LSE_REP:(2 * pr + 1) * LSE_REP + 1])
            term = w2 * tl[:, pr * pair_w:(pr + 1) * pair_w]
            acc = term if acc is None else acc + term
        pairs.append(acc)
    return jnp.concatenate(pairs, axis=-1).astype(BF16)


def _inproj_kernel(x_ref, g_ref, w_ref, z_ref, stats_ref):
    hb = _rms(x_ref[...], g_ref[...]).astype(BF16)
    z = _dot(hb, w_ref[...])
    z_ref[...] = z.astype(BF16)
    pair_w = 2 * HEAD_DIM
    lane = lax.broadcasted_iota(jnp.int32, (1, pair_w), 1)
    lo = lane < HEAD_DIM
    out = jnp.zeros((1, pair_w), F32)
    for b in range(2 * C_W // pair_w):
        sq = z[:, b * pair_w:(b + 1) * pair_w]
        sq = sq * sq
        tot = jnp.sum(sq, axis=-1, keepdims=True)
        first = jnp.sum(jnp.where(lo, sq, 0.0), axis=-1, keepdims=True)
        m_first = jnp.max(first, axis=0, keepdims=True)
        m_second = jnp.max(tot - first, axis=0, keepdims=True)
        out = jnp.where(lane == 2 * b, m_first, jnp.where(lane == 2 * b + 1, m_second, out))
    stats_ref[...] = jnp.broadcast_to(out, stats_ref.shape)


def _inproj(x, g, w):
    t = x.shape[0]
    tm = FFN_TILE
    assert t % tm == 0
    n_out = w.shape[1]
    return pl.pallas_call(
        _inproj_kernel,
        grid=(t // tm,),
        in_specs=[pl.BlockSpec((tm, D_MODEL), lambda i: (i, 0)),
                  _const_spec((1, D_MODEL)),
                  _resident_spec((D_MODEL, n_out))],
        out_specs=[pl.BlockSpec((tm, n_out), lambda i: (i, 0)),
                   pl.BlockSpec((None, 8, 2 * HEAD_DIM), lambda i: (i, 0, 0))],
        out_shape=[jax.ShapeDtypeStruct((t, n_out), BF16),
                   jax.ShapeDtypeStruct((t // tm, 8, 2 * HEAD_DIM), F32)],
        compiler_params=_cparams(1),
        name="inproj_c",
    )(x, g, w)


def _natten_kernel(q_ref, k0_ref, k1_ref, k2_ref, v0_ref, v1_ref, v2_ref, bias_ref, o_ref):
    k_refs = (k0_ref, k1_ref, k2_ref)
    v_refs = (v0_ref, v1_ref, v2_ref)
    lane = lax.broadcasted_iota(jnp.int32, (1, 2 * HEAD_DIM), 1)
    lo = lane < HEAD_DIM
    kb = k0_ref.shape[0]
    for pr in range(C_HEADS // 2):
        cols = slice(pr * 2 * HEAD_DIM, (pr + 1) * 2 * HEAD_DIM)
        q2 = q_ref[:, cols]
        ks = [r[:, cols] for r in k_refs]
        vs = [r[:, cols] for r in v_refs]
        o_pair = []
        for hh in range(2):
            qm = jnp.where(lo if hh == 0 else ~lo, q2, jnp.zeros_like(q2))
            s = jnp.concatenate([_dot_nt(qm, kk) for kk in ks], axis=-1) + bias_ref[pr * 2 + hh]
            m = jnp.max(s, axis=-1, keepdims=True)
            p = jnp.exp(s - m)
            l = jnp.sum(p, axis=-1, keepdims=True)
            pb = p.astype(BF16)
            o = _dot(pb[:, :kb], vs[0])
            for i in range(1, len(vs)):
                o = o + _dot(pb[:, i * kb:(i + 1) * kb], vs[i])
            o_pair.append(o / l)
        o_ref[:, cols] = jnp.where(lo, o_pair[0], o_pair[1]).astype(BF16)


def _natten_bounded_kernel(q_ref, k0_ref, k1_ref, k2_ref, v0_ref, v1_ref, v2_ref, bias_ref, shift_ref, o_ref):
    k_refs = (k0_ref, k1_ref, k2_ref)
    v_refs = (v0_ref, v1_ref, v2_ref)
    pair_w = 2 * HEAD_DIM
    lo = lax.broadcasted_iota(jnp.int32, (1, pair_w), 1) < HEAD_DIM
    kb = k0_ref.shape[0]
    one = jnp.ones((), BF16)

    def cols_of(head):
        return slice((head // 2) * pair_w, (head // 2 + 1) * pair_w)

    def own_lanes(head):
        return lo if head % 2 == 0 else ~lo

    def scores(head):
        q2 = q_ref[:, cols_of(head)]
        qm = jnp.where(own_lanes(head), q2, jnp.zeros_like(q2))
        return jnp.concatenate([_dot_nt(qm, r[:, cols_of(head)]) for r in k_refs], axis=-1)

    def probs(head, s):
        return jnp.exp(s + (bias_ref[head] - shift_ref[head:head + 1, :])).astype(BF16)

    def output(head, p):
        r = None
        for i, v_ref in enumerate(v_refs):
            t = _dot(p[:, i * kb:(i + 1) * kb], jnp.where(own_lanes(head), v_ref[:, cols_of(head)], one))
            r = t if r is None else r + t
        l = r[:, HEAD_DIM:HEAD_DIM + 1] if head % 2 == 0 else r[:, 0:1]
        return r / l

    s = {0: scores(0), 1: scores(1)}
    p = {0: probs(0, s[0])}
    o = {}
    for h in range(C_HEADS):
        if h + 2 < C_HEADS:
            s[h + 2] = scores(h + 2)
        if h + 1 < C_HEADS:
            p[h + 1] = probs(h + 1, s[h + 1])
        o[h] = output(h, p[h])
        if h % 2 == 1:
            o_ref[:, cols_of(h)] = jnp.where(lo, o[h - 1], o[h]).astype(BF16)


def _natten_bias(rpb):
    c = np.arange(GRID_W)[:, None]
    kc = np.arange(GRID_W)[None, :]
    cstart = np.clip(c - C_WIN_COLS // 2, 0, GRID_W - C_WIN_COLS)
    col_valid = (kc >= cstart) & (kc < cstart + C_WIN_COLS)
    padded = jnp.pad(rpb.astype(F32), ((0, 0), (0, 0), (GRID_W, GRID_W)))
    base = GRID_W + C_WIN_COLS - 1
    blocks = jnp.stack([padded[:, :, base - q:base - q + GRID_W] for q in range(GRID_W)], axis=2)
    blocks = jnp.where(col_valid[None, None], blocks, NEG)
    masked = jnp.full((rpb.shape[0], GRID_W, GRID_W), NEG, F32)
    tables = []
    for off, first in ((0, lambda dr: 0), (-4, lambda dr: dr), (-8, lambda dr: 4)):
        rows = []
        for dr in range(C_ROWS_Q):
            rows.append(jnp.concatenate(
                [blocks[:, off + kri - dr + C_WIN_ROWS - 1]
                 if first(dr) <= kri < first(dr) + C_WIN_ROWS else masked
                 for kri in range(C_ROWS_K)], axis=-1))
        tables.append(jnp.concatenate(rows, axis=1))
    return jnp.stack(tables)


def _natten(z, stats, rpb, bias, batch, seq):
    t = batch * seq
    qb = C_ROWS_Q * GRID_W
    nrb = seq // qb
    nkb = C_ROWS_K // C_ROWS_Q

    def kspec(i, col):
        return pl.BlockSpec(
            (qb, C_W), lambda rb, b: (b * nrb + jnp.clip(rb - 1, 0, nrb - nkb) + i, col))

    def variant(rb):
        return jnp.where(rb == 0, 0, jnp.where(rb == nrb - 1, 2, 1))

    def call(body, *extra):
        return pl.pallas_call(
            body,
            grid=(nrb, batch),
            in_specs=[pl.BlockSpec((qb, C_W), lambda rb, b: (b * nrb + rb, 0))]
            + [kspec(i, 1) for i in range(nkb)]
            + [kspec(i, 2) for i in range(nkb)]
            + [pl.BlockSpec((None, C_HEADS, qb, C_ROWS_K * GRID_W), lambda rb, b: (variant(rb), 0, 0, 0))]
            + [_const_spec(e.shape) for e in extra],
            out_specs=pl.BlockSpec((qb, C_W), lambda rb, b: (b * nrb + rb, 0)),
            out_shape=jax.ShapeDtypeStruct((t, C_W), BF16),
            compiler_params=_cparams(2),
            name="natten",
        )(z, z, z, z, z, z, z, bias, *extra)

    norms = jnp.sqrt(jnp.max(stats[:, 0, :2 * C_HEADS], axis=0)) * C_BOUND_SLACK
    qk_bound = norms[:C_HEADS] * norms[C_HEADS:]
    rpb_max = jnp.max(rpb, axis=(1, 2))
    rpb_range = rpb_max - jnp.min(rpb, axis=(1, 2))
    safe = jnp.max(2.0 * qk_bound + rpb_range) <= C_BOUND_LIMIT
    shift = jnp.broadcast_to((qk_bound + rpb_max - C_SHIFT_MARGIN)[:, None], (C_HEADS, C_ROWS_K * GRID_W))
    return lax.cond(safe, lambda: call(_natten_bounded_kernel, shift), lambda: call(_natten_kernel))


def _mem_kv_kernel(mem_ref, g_ref, w_ref, kv_ref):
    mb = _rms(mem_ref[...], g_ref[...]).astype(BF16)
    kv_ref[...] = _dot(mb, w_ref[...]).astype(BF16)


def _mem_kv(mem, g, w):
    batch, n_mem, _ = mem.shape
    return pl.pallas_call(
        _mem_kv_kernel,
        grid=(batch,),
        in_specs=[pl.BlockSpec((None, n_mem, D_MODEL), lambda b: (b, 0, 0)),
                  _const_spec((1, D_MODEL)),
                  _const_spec((D_MODEL, 2 * D_MODEL))],
        out_specs=pl.BlockSpec((None, n_mem, 2 * D_MODEL), lambda b: (b, 0, 0)),
        out_shape=jax.ShapeDtypeStruct((batch, n_mem, 2 * D_MODEL), BF16),
        compiler_params=_cparams(1),
        name="mem_kv",
    )(mem, g, w)


def _xattn_tail(x, g_ref, wq_ref, kv_ref, wo_ref, y_ref):
    q = _dot(_rms(x, g_ref[...]).astype(BF16), wq_ref[...]).astype(BF16)
    def score(h):
        cols = slice(h * X_HEAD_DIM, (h + 1) * X_HEAD_DIM)
        return _dot_nt(q[:, cols], kv_ref[:, cols])

    def finish(h, s):
        m = jnp.max(s, axis=-1, keepdims=True)
        p = jnp.exp(s - m)
        l = jnp.sum(p, axis=-1, keepdims=True)
        o = _dot(p.astype(BF16), kv_ref[:, D_MODEL + h * X_HEAD_DIM:D_MODEL + (h + 1) * X_HEAD_DIM])
        return (o / l).astype(BF16)

    outs = []
    s = score(0)
    for h in range(X_HEADS):
        nxt = score(h + 1) if h + 1 < X_HEADS else None
        outs.append(finish(h, s))
        s = nxt
    y_ref[...] = x + _dot(jnp.concatenate(outs, axis=-1), wo_ref[...])


def _mix_ab_xattn_kernel(x_ref, oa_ref, r0_ref, r1_ref, r2_ref, permT_ref, woa_ref, wob_ref,
                         g_ref, wq_ref, kv_ref, wo_ref, y_ref):
    ob = _merge_branches([r0_ref[...], r1_ref[...], r2_ref[...]], permT_ref)
    x = x_ref[...] + _dot(oa_ref[...], woa_ref[...]) + _dot(ob, wob_ref[...])
    _xattn_tail(x, g_ref, wq_ref, kv_ref, wo_ref, y_ref)


def _mix_c_xattn_kernel(x_ref, oc_ref, woc_ref, g_ref, wq_ref, kv_ref, wo_ref, y_ref):
    x = x_ref[...] + _dot(oc_ref[...], woc_ref[...])
    _xattn_tail(x, g_ref, wq_ref, kv_ref, wo_ref, y_ref)


def _mix_xattn(body, x, token_args, const_args, g, wq, kv, wo, seq, tm):
    t = x.shape[0]
    assert seq % tm == 0
    n_seq_tiles = seq // tm
    n_mem = kv.shape[1]
    token_spec = lambda a: pl.BlockSpec((tm, a.shape[1]), lambda i: (i, 0))
    in_specs = ([token_spec(x)] + [token_spec(a) for a in token_args]
                + [_resident_spec(a.shape) for a in const_args]
                + [_const_spec((1, D_MODEL)),
                   _resident_spec((D_MODEL, D_MODEL)),
                   pl.BlockSpec((None, n_mem, 2 * D_MODEL), lambda i: (i // n_seq_tiles, 0, 0)),
                   _resident_spec((D_MODEL, D_MODEL))])
    return pl.pallas_call(
        body,
        grid=(t // tm,),
        in_specs=in_specs,
        out_specs=pl.BlockSpec((tm, D_MODEL), lambda i: (i, 0)),
        out_shape=jax.ShapeDtypeStruct((t, D_MODEL), F32),
        compiler_params=_cparams(1),
        name="mix_xattn",
    )(x, *token_args, *const_args, g, wq, kv, wo)


def _swiglu_kernel(x_ref, g_ref, wg_ref, wu_ref, wd_ref, gf_ref, y_ref, *, final_norm):
    sub = x_ref.shape[0] // FFN_SPLIT
    for i in range(FFN_SPLIT):
        rows = slice(i * sub, (i + 1) * sub)
        x = x_ref[rows, :]
        hb = _rms(x, g_ref[...]).astype(BF16)
        gate = _dot(hb, wg_ref[...])
        up = _dot(hb, wu_ref[...])
        act = (gate / (1.0 + jnp.exp(-gate)) * up).astype(BF16)
        y = x + _dot(act, wd_ref[...])
        if final_norm:
            y = _rms(y, gf_ref[...])
        y_ref[rows, :] = y


def _swiglu(x, g, wg, wu, wd, g_final, final_norm):
    t = x.shape[0]
    tm = FFN_TILE
    assert t % tm == 0
    return pl.pallas_call(
        functools.partial(_swiglu_kernel, final_norm=final_norm),
        grid=(t // tm,),
        in_specs=[pl.BlockSpec((tm, D_MODEL), lambda i: (i, 0)),
                  _const_spec((1, D_MODEL)),
                  _resident_spec((D_MODEL, D_FF)),
                  _resident_spec((D_MODEL, D_FF)),
                  _resident_spec((D_FF, D_MODEL)),
                  _const_spec((1, D_MODEL))],
        out_specs=pl.BlockSpec((tm, D_MODEL), lambda i: (i, 0)),
        out_shape=jax.ShapeDtypeStruct((t, D_MODEL), F32),
        compiler_params=_cparams(1),
        name="swiglu",
    )(x, g, wg, wu, wd, g_final)


def _rope_tables(seq):
    tok = jnp.arange(seq, dtype=jnp.int32)
    row = (tok // GRID_W).astype(F32)
    col = (tok % GRID_W).astype(F32)
    axis_dim = HEAD_DIM // 2
    inv_freq = ROPE_THETA ** (-jnp.arange(0, axis_dim, 2, dtype=F32) / axis_dim)
    ang = jnp.concatenate([row[:, None] * inv_freq, col[:, None] * inv_freq], axis=-1)
    return jnp.cos(ang).T, jnp.sin(ang).T


def _prepare_ab(w_in, g_qn, g_kn, w_out):
    perm = np.concatenate([np.arange(0, HEAD_DIM, 2), np.arange(1, HEAD_DIM, 2)])
    cols = np.concatenate(
        [h * HEAD_DIM + perm for h in range(A_Q_HEADS + A_KV_HEADS)]
        + [np.arange(A_Q_W + A_KV_W, A_W)])
    waT = w_in[:, cols].T.astype(BF16)
    zbw = w_in[:, A_W:].reshape(D_MODEL, 3, B_BRANCHES, B_W)
    qscale = HEAD_DIM ** -0.5
    wb = jnp.stack([jnp.concatenate([zbw[:, 0, g] * qscale, zbw[:, 1, g], zbw[:, 2, g]], axis=1)
                    for g in range(B_BRANCHES)]).astype(BF16)
    gq = (g_qn[perm] * (qscale * LOG2E)).reshape(HEAD_DIM, 1).astype(F32)
    gk = g_kn[perm].reshape(HEAD_DIM, 1).astype(F32)
    perms = np.stack([_tile_perm(dil) for _, dil in B_PATTERNS[1:]])
    tile_perm = jnp.asarray(perms, BF16)
    tile_perm_t = jnp.asarray(perms.transpose(0, 2, 1), BF16)
    return waT, wb, tile_perm, tile_perm_t, gq, gk, w_out[:A_Q_W].astype(BF16), w_out[A_Q_W:].astype(BF16)


def _prepare_layers(p):
    depth = p["g_mix"].shape[0]
    row = lambda g: g.reshape(1, D_MODEL)
    layers = []
    for l in range(depth):
        lay = dict(g_mix=row(p["g_mix"][l]), g_xattn=row(p["g_xattn"][l]), g_mem=row(p["g_mem"][l]),
                   g_ffn=row(p["g_ffn"][l]),
                   wq=(p["wq_x"][l] * X_HEAD_DIM ** -0.5).astype(BF16), wkv=p["wkv_x"][l].astype(BF16),
                   wo=p["wo_x"][l].astype(BF16), wg=p["w_gu"][l][:, :D_FF].astype(BF16),
                   wu=p["w_gu"][l][:, D_FF:].astype(BF16), wd=p["w_down"][l].astype(BF16))
        i = l // 2
        if l % 2 == 0:
            lay["ab"] = _prepare_ab(p["w_in_ab"][i], p["g_qn"][i], p["g_kn"][i], p["w_out_ab"][i])
        else:
            w_in = p["w_in_c"][i]
            lay["w_in_c"] = jnp.concatenate(
                [w_in[:, :C_W] * HEAD_DIM ** -0.5, w_in[:, C_W:]], axis=1).astype(BF16)
            lay["rpb_c"] = p["rpb_c"][i]
            lay["bias_c"] = _natten_bias(p["rpb_c"][i])
            lay["w_out_c"] = p["w_out_c"][i].astype(BF16)
        layers.append(lay)
    return layers


def _trunk(x, mem, layers, g_final):
    batch, seq, d_model = x.shape
    assert d_model == D_MODEL and mem.shape[0] == batch and mem.shape[2] == D_MODEL
    assert seq % A_TQ == 0 and seq % TOKEN_TILE == 0 and seq >= 2 * A_TK
    assert seq % (C_ROWS_Q * GRID_W) == 0 and seq >= C_ROWS_K * GRID_W
    assert all(seq % (dil * B_SUB) == 0 for _, dil in B_PATTERNS)
    x = x.reshape(batch * seq, D_MODEL)
    for l, lay in enumerate(layers):
        if l % 2 == 0:
            waT, wb, perm, permT, gq, gk, wo_a, wo_b = lay["ab"]
            cosT, sinT = _rope_tables(seq)
            qT, k, vT, stats, zb = _inproj_ab(x, lay["g_mix"], waT, wb, perm, gq, gk, cosT, sinT, seq)
            oa = _gqa_attention(qT, k, vT, stats, batch, seq)
            rs = [_dilated_branch(zb, g, batch, seq) for g in range(B_BRANCHES)]
            body, token_args, const_args = _mix_ab_xattn_kernel, [oa] + rs, [permT, wo_a, wo_b]
            tm = TOKEN_TILE
        else:
            z, stats = _inproj(x, lay["g_mix"], lay["w_in_c"])
            oc = _natten(z, stats, lay["rpb_c"], lay["bias_c"], batch, seq)
            body, token_args, const_args = _mix_c_xattn_kernel, [oc], [lay["w_out_c"]]
            tm = FFN_TILE
        kv = _mem_kv(mem, lay["g_mem"], lay["wkv"])
        x = _mix_xattn(body, x, token_args, const_args, lay["g_xattn"], lay["wq"], kv, lay["wo"], seq, tm)
        x = _swiglu(x, lay["g_ffn"], lay["wg"], lay["wu"], lay["wd"], g_final,
                    final_norm=(l == len(layers) - 1))
    return x.reshape(batch, seq, D_MODEL)


def kernel(x_prompt, x_sample, mem_prompt, mem_sample, g_mix, w_in_ab, g_qn, g_kn, w_out_ab, w_in_c,
           rpb_c, w_out_c, g_xattn, g_mem, wq_x, wkv_x, wo_x, g_ffn, w_gu, w_down, g_final):
    layers = _prepare_layers(dict(
        g_mix=g_mix, w_in_ab=w_in_ab, g_qn=g_qn, g_kn=g_kn, w_out_ab=w_out_ab, w_in_c=w_in_c,
        rpb_c=rpb_c, w_out_c=w_out_c, g_xattn=g_xattn, g_mem=g_mem, wq_x=wq_x, wkv_x=wkv_x,
        wo_x=wo_x, g_ffn=g_ffn, w_gu=w_gu, w_down=w_down))
    g_final = g_final.reshape(1, D_MODEL)
    return (_trunk(x_prompt, mem_prompt, layers, g_final), _trunk(x_sample, mem_sample, layers, g_final))
```

```python
import functools
import math

import numpy as np
import jax
import jax.numpy as jnp
from jax import lax
from jax.experimental import pallas as pl
from jax.experimental.pallas import tpu as pltpu

F32 = jnp.float32
BF16 = jnp.bfloat16

D_MODEL = 1024
GRID_W = 64
HEAD_DIM = 64
EPS = 1e-6
NEG = -1e30
LOG2E = math.log2(math.e)
TOKEN_TILE = 512

A_Q_HEADS = 8
A_KV_HEADS = 2
A_GROUP = A_Q_HEADS // A_KV_HEADS
ROPE_THETA = 10000.0
A_Q_W = A_Q_HEADS * HEAD_DIM
A_KV_W = A_KV_HEADS * HEAD_DIM
A_W = A_Q_W + 2 * A_KV_W
A_TQ = 1024
A_TK = 256
A_V_ROWS = HEAD_DIM + 16
A_UNROLL = 6
A_UNROLL_BOUNDED = 16
A_SHIFT_MARGIN = 60.0
A_BOUND_LIMIT = 90.0
A_BOUND_SLACK = 1.0 + 2.0 ** -6

B_PATTERNS = ((128, 1), (512, 4), (2048, 16))
B_BRANCHES = len(B_PATTERNS)
B_HEADS = 4
B_HALF = 64
B_W = B_HEADS * HEAD_DIM
B_SUB = 2 * B_HALF
B_KEYS = 4 * B_HALF
B_CHUNK = 1024
LSE_REP = 2 * HEAD_DIM // B_HEADS
B_R = B_W + 4 * HEAD_DIM

C_HEADS = 16
C_WIN_ROWS = 8
C_WIN_COLS = 16
C_ROWS_Q = 4
C_ROWS_K = 12
C_W = C_HEADS * HEAD_DIM
C_SHIFT_MARGIN = 40.0
C_BOUND_LIMIT = 120.0
C_BOUND_SLACK = 1.0 + 2.0 ** -6

X_HEADS = 4
X_HEAD_DIM = D_MODEL // X_HEADS
D_FF = 2816
FFN_TILE = 1024
FFN_SPLIT = 4

VMEM_LIMIT = 56 * 1024 * 1024


def _cparams(n_axes):
    return pltpu.CompilerParams(dimension_semantics=("arbitrary",) * n_axes,
                                vmem_limit_bytes=VMEM_LIMIT)


def _rms(x, g):
    ms = jnp.mean(x * x, axis=-1, keepdims=True)
    return x * lax.rsqrt(ms + EPS) * g


def _dot(a, b):
    return jnp.dot(a, b, preferred_element_type=F32)


def _dot_nt(a, b):
    return lax.dot_general(a, b, (((1,), (1,)), ((), ())), preferred_element_type=F32)


def _const_spec(shape):
    zeros = (0,) * len(shape)
    return pl.BlockSpec(shape, lambda *_: zeros)


def _resident_spec(shape):
    zeros = (0,) * len(shape)
    return pl.BlockSpec(shape, lambda *_: zeros, pipeline_mode=pl.Buffered(1))


def _inproj_ab_kernel(x_ref, g_ref, waT_ref, wb_ref, perm_ref, gq_ref, gk_ref, cos_ref, sin_ref,
                      qT_ref, k_ref, vT_ref, stats_ref, zb_ref):
    hb = _rms(x_ref[...], g_ref[...]).astype(BF16)
    zT = _dot_nt(waT_ref[...], hb)
    cos = cos_ref[...]
    sin = sin_ref[...]
    half = HEAD_DIM // 2

    def norm_rope(zh, gcol):
        ms = jnp.mean(zh * zh, axis=0, keepdims=True)
        y = zh * lax.rsqrt(ms + EPS) * gcol
        yr, yi = y[:half], y[half:]
        return jnp.concatenate([yr * cos - yi * sin, yr * sin + yi * cos], axis=0)

    def l2(v):
        return jnp.sqrt(jnp.sum(v * v, axis=0, keepdims=True))

    stats_ref[...] = jnp.zeros(stats_ref.shape, F32)
    for h in range(A_Q_HEADS):
        sl = slice(h * HEAD_DIM, (h + 1) * HEAD_DIM)
        q = norm_rope(zT[sl], gq_ref[...])
        qT_ref[sl, :] = q.astype(BF16)
        stats_ref[h // A_GROUP, h % A_GROUP:h % A_GROUP + 1, :] = l2(q)
    ks = [norm_rope(zT[A_Q_W + h * HEAD_DIM:A_Q_W + (h + 1) * HEAD_DIM], gk_ref[...])
          for h in range(A_KV_HEADS)]
    for h in range(A_KV_HEADS):
        stats_ref[h, A_GROUP:A_GROUP + 1, :] = l2(ks[h])
    kT = jnp.concatenate(ks, axis=0)
    k_nat = kT.T
    for h in range(A_KV_HEADS):
        k_ref[h] = k_nat[:, h * HEAD_DIM:(h + 1) * HEAD_DIM].astype(BF16)
    for h in range(A_KV_HEADS):
        for c in range(vT_ref.shape[1]):
            vT_ref[h, c] = zT[A_Q_W + A_KV_W + h * HEAD_DIM:A_Q_W + A_KV_W + (h + 1) * HEAD_DIM,
                              c * A_TK:(c + 1) * A_TK].astype(BF16)
    zb_ref[0] = _dot(hb, wb_ref[0]).astype(BF16)
    for g in range(1, B_BRANCHES):
        zb_ref[g] = _dot(perm_ref[g - 1], _dot(hb, wb_ref[g]).astype(BF16)).astype(BF16)


def _inproj_ab(x, g_mix, waT, wb, perm, gq, gk, cosT, sinT, seq):
    t = x.shape[0]
    tm = TOKEN_TILE
    n_seq_tiles = seq // tm
    return pl.pallas_call(
        _inproj_ab_kernel,
        grid=(t // tm,),
        in_specs=[
            pl.BlockSpec((tm, D_MODEL), lambda i: (i, 0)),
            _const_spec((1, D_MODEL)),
            _const_spec((A_W, D_MODEL)),
            _const_spec((B_BRANCHES, D_MODEL, 3 * B_W)),
            _const_spec((B_BRANCHES - 1, tm, tm)),
            _const_spec((HEAD_DIM, 1)),
            _const_spec((HEAD_DIM, 1)),
            pl.BlockSpec((HEAD_DIM // 2, tm), lambda i: (0, i % n_seq_tiles)),
            pl.BlockSpec((HEAD_DIM // 2, tm), lambda i: (0, i % n_seq_tiles)),
        ],
        out_specs=[
            pl.BlockSpec((A_Q_W, tm), lambda i: (0, i)),
            pl.BlockSpec((A_KV_HEADS, tm, HEAD_DIM), lambda i: (0, i, 0)),
            pl.BlockSpec((A_KV_HEADS, tm // A_TK, HEAD_DIM, A_TK), lambda i: (0, i, 0, 0)),
            pl.BlockSpec((A_KV_HEADS, 8, tm), lambda i: (0, 0, i)),
            pl.BlockSpec((B_BRANCHES, tm, 3 * B_W), lambda i: (0, i, 0)),
        ],
        out_shape=[
            jax.ShapeDtypeStruct((A_Q_W, t), BF16),
            jax.ShapeDtypeStruct((A_KV_HEADS, t, HEAD_DIM), BF16),
            jax.ShapeDtypeStruct((A_KV_HEADS, t // A_TK, HEAD_DIM, A_TK), BF16),
            jax.ShapeDtypeStruct((A_KV_HEADS, 8, t), F32),
            jax.ShapeDtypeStruct((B_BRANCHES, t, 3 * B_W), BF16),
        ],
        compiler_params=_cparams(1),
        name="inproj_ab",
    )(x, g_mix, waT, wb, perm, gq, gk, cosT, sinT)


def _gqa_kernel(qT_ref, k_ref, vT_ref, o_ref, s_scr, p_scr, cm_scr, alpha_scr, m_scr, acc_scr):
    tq = qT_ref.shape[1]
    tk = A_TK
    n_chunks = vT_ref.shape[0]
    ones = jnp.ones((A_V_ROWS - HEAD_DIM, tk), BF16)

    def scores(chunk, slot):
        k = k_ref[pl.ds(pl.multiple_of(chunk * tk, tk), tk), :]
        for h in range(A_GROUP):
            s = _dot(k, qT_ref[h * HEAD_DIM:(h + 1) * HEAD_DIM, :])
            s_scr[slot, h] = s
            cm_scr[slot, h] = jnp.max(s.reshape(tk // 8, 8, tq), axis=0)

    def probs(slot):
        for h in range(A_GROUP):
            m_old = m_scr[h]
            m_new = jnp.maximum(m_old, jnp.max(cm_scr[slot, h], axis=0, keepdims=True))
            alpha_scr[slot, h] = jnp.exp2(m_old - m_new)
            m_scr[h] = m_new
            p_scr[slot, h] = jnp.exp2(s_scr[slot, h] - m_new).astype(BF16)

    def accumulate(chunk, slot):
        vT_aug = jnp.concatenate([vT_ref[chunk], ones], axis=0)
        for h in range(A_GROUP):
            acc_scr[h] = alpha_scr[slot, h] * acc_scr[h] + _dot(vT_aug, p_scr[slot, h])

    def steady(c, count):
        for u in range(count):
            accumulate(c + u, u % 2)
            probs((u + 1) % 2)
            scores(c + u + 2, u % 2)

    m_scr[...] = jnp.full(m_scr.shape, NEG, F32)
    acc_scr[...] = jnp.zeros(acc_scr.shape, F32)
    scores(0, 0)
    scores(1, 1)
    probs(0)
    n_loop = (n_chunks - 2) // A_UNROLL

    def body(i, carry):
        steady(i * A_UNROLL, A_UNROLL)
        return carry

    if n_loop > 1:
        lax.fori_loop(0, n_loop, body, 0)
    else:
        steady(0, n_loop * A_UNROLL)
    steady(n_loop * A_UNROLL, (n_chunks - 2) % A_UNROLL)
    accumulate(n_chunks - 2, 0)
    probs(1)
    accumulate(n_chunks - 1, 1)
    oT = jnp.concatenate(
        [acc_scr[h, :HEAD_DIM, :] / acc_scr[h, HEAD_DIM:HEAD_DIM + 1, :] for h in range(A_GROUP)], axis=0)
    o_ref[...] = oT.T.astype(BF16)


def _gqa_bounded_kernel(qT_ref, k_ref, vT_ref, shift_ref, o_ref, p_scr, acc_scr):
    tq = qT_ref.shape[1]
    tk = A_TK
    n_chunks = vT_ref.shape[0]
    ones = jnp.ones((A_V_ROWS - HEAD_DIM, tk), BF16)

    def probs(chunk, slot):
        k = k_ref[pl.ds(pl.multiple_of(chunk * tk, tk), tk), :]
        for h in range(A_GROUP):
            s = _dot(k, qT_ref[h * HEAD_DIM:(h + 1) * HEAD_DIM, :])
            p_scr[slot, h] = jnp.exp2(s - shift_ref[h:h + 1, :]).astype(BF16)

    def accumulate(chunk, slot):
        vT_aug = jnp.concatenate([vT_ref[chunk], ones], axis=0)
        for h in range(A_GROUP):
            acc_scr[h] += _dot(vT_aug, p_scr[slot, h])

    def steady(c, count):
        for u in range(count):
            accumulate(c + u, u % 2)
            probs(c + u + 2, u % 2)

    acc_scr[...] = jnp.zeros(acc_scr.shape, F32)
    probs(0, 0)
    probs(1, 1)
    n_loop = (n_chunks - 2) // A_UNROLL_BOUNDED

    def body(i, carry):
        steady(i * A_UNROLL_BOUNDED, A_UNROLL_BOUNDED)
        return carry

    if n_loop > 1:
        lax.fori_loop(0, n_loop, body, 0)
    else:
        steady(0, n_loop * A_UNROLL_BOUNDED)
    steady(n_loop * A_UNROLL_BOUNDED, (n_chunks - 2) % A_UNROLL_BOUNDED)
    accumulate(n_chunks - 2, 0)
    accumulate(n_chunks - 1, 1)
    oT = jnp.concatenate(
        [acc_scr[h, :HEAD_DIM, :] / acc_scr[h, HEAD_DIM:HEAD_DIM + 1, :] for h in range(A_GROUP)], axis=0)
    o_ref[...] = oT.T.astype(BF16)


def _gqa_attention(qT, k, vT, stats, batch, seq, tq=A_TQ):
    t = batch * seq
    nq = seq // tq
    k4 = k.reshape(A_KV_HEADS, batch, seq, HEAD_DIM)
    v5 = vT.reshape(A_KV_HEADS, batch, seq // A_TK, HEAD_DIM, A_TK)
    kmax = jnp.max(stats[:, A_GROUP].reshape(A_KV_HEADS, batch, seq), axis=-1)
    bound = (stats.reshape(A_KV_HEADS, 8, batch, seq) * kmax[:, None, :, None]
             * A_BOUND_SLACK).reshape(A_KV_HEADS, 8, t)
    in_specs = [
        pl.BlockSpec((A_GROUP * HEAD_DIM, tq), lambda b, g, i: (g, b * nq + i)),
        pl.BlockSpec((None, None, seq, HEAD_DIM), lambda b, g, i: (g, b, 0, 0)),
        pl.BlockSpec((None, None, seq // A_TK, HEAD_DIM, A_TK), lambda b, g, i: (g, b, 0, 0, 0)),
    ]
    common = dict(
        grid=(batch, A_KV_HEADS, nq),
        out_specs=pl.BlockSpec((tq, A_GROUP * HEAD_DIM), lambda b, g, i: (b * nq + i, g)),
        out_shape=jax.ShapeDtypeStruct((t, A_Q_W), BF16),
        compiler_params=_cparams(3),
    )

    def bounded():
        return pl.pallas_call(
            _gqa_bounded_kernel,
            in_specs=in_specs + [pl.BlockSpec((None, 8, tq), lambda b, g, i: (g, 0, b * nq + i))],
            scratch_shapes=[pltpu.VMEM((2, A_GROUP, A_TK, tq), BF16),
                            pltpu.VMEM((A_GROUP, A_V_ROWS, tq), F32)],
            name="gqa_bounded", **common,
        )(qT, k4, v5, bound - A_SHIFT_MARGIN)

    def online():
        return pl.pallas_call(
            _gqa_kernel,
            in_specs=in_specs,
            scratch_shapes=[pltpu.VMEM((2, A_GROUP, A_TK, tq), F32),
                            pltpu.VMEM((2, A_GROUP, A_TK, tq), BF16),
                            pltpu.VMEM((2, A_GROUP, 8, tq), F32),
                            pltpu.VMEM((2, A_GROUP, 1, tq), F32),
                            pltpu.VMEM((A_GROUP, 1, tq), F32),
                            pltpu.VMEM((A_GROUP, A_V_ROWS, tq), F32)],
            name="gqa_attention", **common,
        )(qT, k4, v5)

    safe = jnp.max(bound[:, :A_GROUP]) <= A_BOUND_LIMIT
    return lax.cond(safe, bounded, online)


def _tile_perm(dil):
    c = TOKEN_TILE // dil
    dst = np.arange(TOKEN_TILE)
    src = (dst % c) * dil + dst // c
    p = np.zeros((TOKEN_TILE, TOKEN_TILE), np.float32)
    p[dst, src] = 1.0
    return p


def _seq_rows(ref, start, n, cols):
    c = ref.shape[1]
    if n <= c:
        return ref[start // c, start % c:start % c + n, cols]
    blocks = ref[start // c:(start + n) // c, :, cols]
    return blocks.reshape(n, blocks.shape[-1])


def _store_seq_rows(ref, start, cols, val):
    c = ref.shape[1]
    n = val.shape[0]
    if n <= c:
        ref[start // c, start % c:start % c + n, cols] = val
    else:
        ref[start // c:(start + n) // c, :, cols] = val.reshape(n // c, c, val.shape[-1])


def _dilated_kernel(main_ref, prev_ref, next_ref, bias_ref, r_ref, kv_scr, *, seq_len, cq):
    n = pl.program_id(2)
    kv_cols = slice(B_W, 3 * B_W)
    pair_w = 2 * HEAD_DIM
    lane = lax.broadcasted_iota(jnp.int32, (1, pair_w), 1)
    lo = lane < HEAD_DIM
    key_col = lax.broadcasted_iota(jnp.int32, (1, B_KEYS), 1)

    for rr in range(main_ref.shape[1]):
        main, res = main_ref.at[:, rr], r_ref.at[:, rr]
        kv_scr[0:B_HALF, :] = prev_ref[..., rr, :, kv_cols].reshape(B_HALF, 2 * B_W)
        for j in range(cq // B_SUB):
            kv_scr[B_HALF + j * B_SUB:B_HALF + (j + 1) * B_SUB, :] = _seq_rows(main, j * B_SUB, B_SUB, kv_cols)
        kv_scr[B_HALF + cq:, :] = next_ref[..., rr, :, kv_cols].reshape(B_HALF, 2 * B_W)

        for j in range(cq // B_SUB):
            key_pos = key_col + (n * cq + j * B_SUB - B_HALF)
            edge = jnp.where((key_pos >= 0) & (key_pos < seq_len), 0.0, NEG).astype(F32)
            krows = slice(j * B_SUB, j * B_SUB + B_KEYS)
            lses = []
            for pr in range(B_HEADS // 2):
                cols = slice(pr * pair_w, (pr + 1) * pair_w)
                q2 = _seq_rows(main, j * B_SUB, B_SUB, cols)
                k2 = kv_scr[krows, cols]
                v2 = kv_scr[krows, B_W + pr * pair_w:B_W + (pr + 1) * pair_w]
                o_pair = []
                for hh in range(2):
                    qm = jnp.where(lo if hh == 0 else ~lo, q2, jnp.zeros_like(q2))
                    s = _dot_nt(qm, k2) + bias_ref[pr * 2 + hh] + edge
                    m = jnp.max(s, axis=-1, keepdims=True)
                    p = jnp.exp(s - m)
                    l = jnp.sum(p, axis=-1, keepdims=True)
                    o_pair.append(_dot(p.astype(BF16), v2) / l)
                    lses.append(m + jnp.log(l))
                _store_seq_rows(res, j * B_SUB, cols, jnp.where(lo, o_pair[0], o_pair[1]).astype(BF16))
            tile = lses[B_HEADS - 1]
            for head in range(B_HEADS - 2, -1, -1):
                tile = jnp.where(lane < (head + 1) * LSE_REP, lses[head], tile)
            hi = tile.astype(BF16)
            _store_seq_rows(res, j * B_SUB, slice(B_W, B_W + pair_w), hi)
            _store_seq_rows(res, j * B_SUB, slice(B_W + pair_w, B_R),
                            (tile - hi.astype(F32)).astype(BF16))


def _dilated_bias(branch):
    _, dil = B_PATTERNS[branch]
    slopes = np.exp2(-8.0 * np.arange(1, B_BRANCHES * B_HEADS + 1, dtype=np.float64)
                     / (B_BRANCHES * B_HEADS)).reshape(B_BRANCHES, B_HEADS)[branch]
    a = np.arange(B_SUB)[:, None]
    c = np.arange(B_KEYS)[None, :]
    rel = (c - B_HALF) - a
    bias = -slopes[:, None, None] * (np.abs(rel) * dil).astype(np.float64)[None]
    bias = np.where((np.abs(rel) <= B_HALF)[None], bias, NEG)
    return jnp.asarray(bias, F32)


def _dilated_branch(zb, branch, batch, seq):
    _, dil = B_PATTERNS[branch]
    t = batch * seq
    seq_len = seq // dil
    c = TOKEN_TILE // dil
    n_tiles = seq // TOKEN_TILE
    cq = min(B_CHUNK, seq_len)
    nchunk = seq_len // cq
    n_halo = seq_len // B_HALF
    per = cq // B_HALF
    rb = min(dil, max(1, B_CHUNK // cq))
    zv = zb.reshape(B_BRANCHES, batch, n_tiles, dil, c, 3 * B_W)

    def halo_spec(pos):
        if c >= B_HALF:
            sub = c // B_HALF
            z6 = zb.reshape(B_BRANCHES, batch, n_tiles, dil, sub, B_HALF, 3 * B_W)
            return z6, pl.BlockSpec((None, None, None, rb, None, B_HALF, 3 * B_W),
                                    lambda b, r, n: (branch, b, pos(n) // sub, r, pos(n) % sub, 0, 0))
        return zv, pl.BlockSpec((None, None, B_HALF // c, rb, c, 3 * B_W),
                                lambda b, r, n: (branch, b, pos(n), r, 0, 0))

    prev_arr, prev_spec = halo_spec(lambda n: jnp.maximum(n * per - 1, 0))
    next_arr, next_spec = halo_spec(lambda n: jnp.minimum((n + 1) * per, n_halo - 1))
    res = pl.pallas_call(
        functools.partial(_dilated_kernel, seq_len=seq_len, cq=cq),
        grid=(batch, dil // rb, nchunk),
        in_specs=[pl.BlockSpec((None, None, cq // c, rb, c, 3 * B_W), lambda b, r, n: (branch, b, n, r, 0, 0)),
                  prev_spec, next_spec, _const_spec((B_HEADS, B_SUB, B_KEYS))],
        out_specs=pl.BlockSpec((None, cq // c, rb, c, B_R), lambda b, r, n: (b, n, r, 0, 0)),
        out_shape=jax.ShapeDtypeStruct((batch, n_tiles, dil, c, B_R), BF16),
        scratch_shapes=[pltpu.VMEM((cq + 2 * B_HALF, 2 * B_W), BF16)],
        compiler_params=_cparams(3),
        name=f"dilated_branch{branch}",
    )(zv, prev_arr, next_arr, _dilated_bias(branch))
    return res.reshape(t, B_R)


def _merge_branches(rs, permT_ref):
    pair_w = 2 * HEAD_DIM
    lo = lax.broadcasted_iota(jnp.int32, (1, pair_w), 1) < HEAD_DIM
    tiles = [rs[0].astype(F32)] + [_dot(permT_ref[g - 1], rs[g]) for g in range(1, B_BRANCHES)]
    lses = [tl[:, B_W:B_W + pair_w] + tl[:, B_W + pair_w:] for tl in tiles]
    mx = functools.reduce(jnp.maximum, lses)
    es = [jnp.exp(l - mx) for l in lses]
    inv = 1.0 / functools.reduce(lambda u, v: u + v, es)
    pairs = []
    for pr in range(B_HEADS // 2):
        acc = None
        for tl, e in zip(tiles, es):
            w = e * inv
            w2 = jnp.where(lo, w[:, 2 * pr * LSE_REP:2 * pr * LSE_REP + 1],
                           w[:, (2 * pr + 1) * LSE_REP:(2 * pr + 1) * LSE_REP + 1])
            term = w2 * tl[:, pr * pair_w:(pr + 1) * pair_w]
            acc = term if acc is None else acc + term
        pairs.append(acc)
    return jnp.concatenate(pairs, axis=-1).astype(BF16)


def _inproj_kernel(x_ref, g_ref, w_ref, z_ref, stats_ref):
    hb = _rms(x_ref[...], g_ref[...]).astype(BF16)
    z = _dot(hb, w_ref[...])
    z_ref[...] = z.astype(BF16)
    pair_w = 2 * HEAD_DIM
    lane = lax.broadcasted_iota(jnp.int32, (1, pair_w), 1)
    lo = lane < HEAD_DIM
    out = jnp.zeros((1, pair_w), F32)
    for b in range(2 * C_W // pair_w):
        sq = z[:, b * pair_w:(b + 1) * pair_w]
        sq = sq * sq
        tot = jnp.sum(sq, axis=-1, keepdims=True)
        first = jnp.sum(jnp.where(lo, sq, 0.0), axis=-1, keepdims=True)
        m_first = jnp.max(first, axis=0, keepdims=True)
        m_second = jnp.max(tot - first, axis=0, keepdims=True)
        out = jnp.where(lane == 2 * b, m_first, jnp.where(lane == 2 * b + 1, m_second, out))
    stats_ref[...] = jnp.broadcast_to(out, stats_ref.shape)


def _inproj(x, g, w):
    t = x.shape[0]
    tm = FFN_TILE
    assert t % tm == 0
    n_out = w.shape[1]
    return pl.pallas_call(
        _inproj_kernel,
        grid=(t // tm,),
        in_specs=[pl.BlockSpec((tm, D_MODEL), lambda i: (i, 0)),
                  _const_spec((1, D_MODEL)),
                  _resident_spec((D_MODEL, n_out))],
        out_specs=[pl.BlockSpec((tm, n_out), lambda i: (i, 0)),
                   pl.BlockSpec((None, 8, 2 * HEAD_DIM), lambda i: (i, 0, 0))],
        out_shape=[jax.ShapeDtypeStruct((t, n_out), BF16),
                   jax.ShapeDtypeStruct((t // tm, 8, 2 * HEAD_DIM), F32)],
        compiler_params=_cparams(1),
        name="inproj_c",
    )(x, g, w)


def _natten_kernel(q_ref, k0_ref, k1_ref, k2_ref, v0_ref, v1_ref, v2_ref, bias_ref, o_ref):
    k_refs = (k0_ref, k1_ref, k2_ref)
    v_refs = (v0_ref, v1_ref, v2_ref)
    lane = lax.broadcasted_iota(jnp.int32, (1, 2 * HEAD_DIM), 1)
    lo = lane < HEAD_DIM
    kb = k0_ref.shape[0]
    for pr in range(C_HEADS // 2):
        cols = slice(pr * 2 * HEAD_DIM, (pr + 1) * 2 * HEAD_DIM)
        q2 = q_ref[:, cols]
        ks = [r[:, cols] for r in k_refs]
        vs = [r[:, cols] for r in v_refs]
        o_pair = []
        for hh in range(2):
            qm = jnp.where(lo if hh == 0 else ~lo, q2, jnp.zeros_like(q2))
            s = jnp.concatenate([_dot_nt(qm, kk) for kk in ks], axis=-1) + bias_ref[pr * 2 + hh]
            m = jnp.max(s, axis=-1, keepdims=True)
            p = jnp.exp(s - m)
            l = jnp.sum(p, axis=-1, keepdims=True)
            pb = p.astype(BF16)
            o = _dot(pb[:, :kb], vs[0])
            for i in range(1, len(vs)):
                o = o + _dot(pb[:, i * kb:(i + 1) * kb], vs[i])
            o_pair.append(o / l)
        o_ref[:, cols] = jnp.where(lo, o_pair[0], o_pair[1]).astype(BF16)


def _natten_bounded_kernel(q_ref, k0_ref, k1_ref, k2_ref, v0_ref, v1_ref, v2_ref, bias_ref, shift_ref, o_ref):
    k_refs = (k0_ref, k1_ref, k2_ref)
    v_refs = (v0_ref, v1_ref, v2_ref)
    pair_w = 2 * HEAD_DIM
    lo = lax.broadcasted_iota(jnp.int32, (1, pair_w), 1) < HEAD_DIM
    kb = k0_ref.shape[0]
    one = jnp.ones((), BF16)

    def cols_of(head):
        return slice((head // 2) * pair_w, (head // 2 + 1) * pair_w)

    def own_lanes(head):
        return lo if head % 2 == 0 else ~lo

    def scores(head):
        q2 = q_ref[:, cols_of(head)]
        qm = jnp.where(own_lanes(head), q2, jnp.zeros_like(q2))
        return jnp.concatenate([_dot_nt(qm, r[:, cols_of(head)]) for r in k_refs], axis=-1)

    def probs(head, s):
        return jnp.exp(s + (bias_ref[head] - shift_ref[head:head + 1, :])).astype(BF16)

    def output(head, p):
        r = None
        for i, v_ref in enumerate(v_refs):
            t = _dot(p[:, i * kb:(i + 1) * kb], jnp.where(own_lanes(head), v_ref[:, cols_of(head)], one))
            r = t if r is None else r + t
        l = r[:, HEAD_DIM:HEAD_DIM + 1] if head % 2 == 0 else r[:, 0:1]
        return r / l

    s = {0: scores(0), 1: scores(1)}
    p = {0: probs(0, s[0])}
    o = {}
    for h in range(C_HEADS):
        if h + 2 < C_HEADS:
            s[h + 2] = scores(h + 2)
        if h + 1 < C_HEADS:
            p[h + 1] = probs(h + 1, s[h + 1])
        o[h] = output(h, p[h])
        if h % 2 == 1:
            o_ref[:, cols_of(h)] = jnp.where(lo, o[h - 1], o[h]).astype(BF16)


def _natten_bias(rpb):
    c = np.arange(GRID_W)[:, None]
    kc = np.arange(GRID_W)[None, :]
    cstart = np.clip(c - C_WIN_COLS // 2, 0, GRID_W - C_WIN_COLS)
    col_valid = (kc >= cstart) & (kc < cstart + C_WIN_COLS)
    padded = jnp.pad(rpb.astype(F32), ((0, 0), (0, 0), (GRID_W, GRID_W)))
    base = GRID_W + C_WIN_COLS - 1
    blocks = jnp.stack([padded[:, :, base - q:base - q + GRID_W] for q in range(GRID_W)], axis=2)
    blocks = jnp.where(col_valid[None, None], blocks, NEG)
    masked = jnp.full((rpb.shape[0], GRID_W, GRID_W), NEG, F32)
    tables = []
    for off, first in ((0, lambda dr: 0), (-4, lambda dr: dr), (-8, lambda dr: 4)):
        rows = []
        for dr in range(C_ROWS_Q):
            rows.append(jnp.concatenate(
                [blocks[:, off + kri - dr + C_WIN_ROWS - 1]
                 if first(dr) <= kri < first(dr) + C_WIN_ROWS else masked
                 for kri in range(C_ROWS_K)], axis=-1))
        tables.append(jnp.concatenate(rows, axis=1))
    return jnp.stack(tables)


def _natten(z, stats, rpb, bias, batch, seq):
    t = batch * seq
    qb = C_ROWS_Q * GRID_W
    nrb = seq // qb
    nkb = C_ROWS_K // C_ROWS_Q

    def kspec(i, col):
        return pl.BlockSpec(
            (qb, C_W), lambda rb, b: (b * nrb + jnp.clip(rb - 1, 0, nrb - nkb) + i, col))

    def variant(rb):
        return jnp.where(rb == 0, 0, jnp.where(rb == nrb - 1, 2, 1))

    def call(body, *extra):
        return pl.pallas_call(
            body,
            grid=(nrb, batch),
            in_specs=[pl.BlockSpec((qb, C_W), lambda rb, b: (b * nrb + rb, 0))]
            + [kspec(i, 1) for i in range(nkb)]
            + [kspec(i, 2) for i in range(nkb)]
            + [pl.BlockSpec((None, C_HEADS, qb, C_ROWS_K * GRID_W), lambda rb, b: (variant(rb), 0, 0, 0))]
            + [_const_spec(e.shape) for e in extra],
            out_specs=pl.BlockSpec((qb, C_W), lambda rb, b: (b * nrb + rb, 0)),
            out_shape=jax.ShapeDtypeStruct((t, C_W), BF16),
            compiler_params=_cparams(2),
            name="natten",
        )(z, z, z, z, z, z, z, bias, *extra)

    norms = jnp.sqrt(jnp.max(stats[:, 0, :2 * C_HEADS], axis=0)) * C_BOUND_SLACK
    qk_bound = norms[:C_HEADS] * norms[C_HEADS:]
    rpb_max = jnp.max(rpb, axis=(1, 2))
    rpb_range = rpb_max - jnp.min(rpb, axis=(1, 2))
    safe = jnp.max(2.0 * qk_bound + rpb_range) <= C_BOUND_LIMIT
    shift = jnp.broadcast_to((qk_bound + rpb_max - C_SHIFT_MARGIN)[:, None], (C_HEADS, C_ROWS_K * GRID_W))
    return lax.cond(safe, lambda: call(_natten_bounded_kernel, shift), lambda: call(_natten_kernel))


def _mem_kv_kernel(mem_ref, g_ref, w_ref, kv_ref):
    mb = _rms(mem_ref[...], g_ref[...]).astype(BF16)
    kv_ref[...] = _dot(mb, w_ref[...]).astype(BF16)


def _mem_kv(mem, g, w):
    batch, n_mem, _ = mem.shape
    return pl.pallas_call(
        _mem_kv_kernel,
        grid=(batch,),
        in_specs=[pl.BlockSpec((None, n_mem, D_MODEL), lambda b: (b, 0, 0)),
                  _const_spec((1, D_MODEL)),
                  _const_spec((D_MODEL, 2 * D_MODEL))],
        out_specs=pl.BlockSpec((None, n_mem, 2 * D_MODEL), lambda b: (b, 0, 0)),
        out_shape=jax.ShapeDtypeStruct((batch, n_mem, 2 * D_MODEL), BF16),
        compiler_params=_cparams(1),
        name="mem_kv",
    )(mem, g, w)


def _xattn_tail(x, g_ref, wq_ref, kv_ref, wo_ref, y_ref):
    q = _dot(_rms(x, g_ref[...]).astype(BF16), wq_ref[...]).astype(BF16)
    def score(h):
        cols = slice(h * X_HEAD_DIM, (h + 1) * X_HEAD_DIM)
        return _dot_nt(q[:, cols], kv_ref[:, cols])

    def finish(h, s):
        m = jnp.max(s, axis=-1, keepdims=True)
        p = jnp.exp(s - m)
        l = jnp.sum(p, axis=-1, keepdims=True)
        o = _dot(p.astype(BF16), kv_ref[:, D_MODEL + h * X_HEAD_DIM:D_MODEL + (h + 1) * X_HEAD_DIM])
        return (o / l).astype(BF16)

    outs = []
    s = score(0)
    for h in range(X_HEADS):
        nxt = score(h + 1) if h + 1 < X_HEADS else None
        outs.append(finish(h, s))
        s = nxt
    y_ref[...] = x + _dot(jnp.concatenate(outs, axis=-1), wo_ref[...])


def _mix_ab_xattn_kernel(x_ref, oa_ref, r0_ref, r1_ref, r2_ref, permT_ref, woa_ref, wob_ref,
                         g_ref, wq_ref, kv_ref, wo_ref, y_ref):
    r_refs = (r0_ref, r1_ref, r2_ref)
    ob = jnp.concatenate(
        [_merge_branches([r[i * TOKEN_TILE:(i + 1) * TOKEN_TILE, :] for r in r_refs], permT_ref)
         for i in range(x_ref.shape[0] // TOKEN_TILE)], axis=0)
    x = x_ref[...] + _dot(oa_ref[...], woa_ref[...]) + _dot(ob, wob_ref[...])
    _xattn_tail(x, g_ref, wq_ref, kv_ref, wo_ref, y_ref)


def _mix_c_xattn_kernel(x_ref, oc_ref, woc_ref, g_ref, wq_ref, kv_ref, wo_ref, y_ref):
    x = x_ref[...] + _dot(oc_ref[...], woc_ref[...])
    _xattn_tail(x, g_ref, wq_ref, kv_ref, wo_ref, y_ref)


def _mix_xattn(body, x, token_args, const_args, g, wq, kv, wo, seq, tm):
    t = x.shape[0]
    assert seq % tm == 0
    n_seq_tiles = seq // tm
    n_mem = kv.shape[1]
    token_spec = lambda a: pl.BlockSpec((tm, a.shape[1]), lambda i: (i, 0))
    in_specs = ([token_spec(x)] + [token_spec(a) for a in token_args]
                + [_resident_spec(a.shape) for a in const_args]
                + [_const_spec((1, D_MODEL)),
                   _resident_spec((D_MODEL, D_MODEL)),
                   pl.BlockSpec((None, n_mem, 2 * D_MODEL), lambda i: (i // n_seq_tiles, 0, 0)),
                   _resident_spec((D_MODEL, D_MODEL))])
    return pl.pallas_call(
        body,
        grid=(t // tm,),
        in_specs=in_specs,
        out_specs=pl.BlockSpec((tm, D_MODEL), lambda i: (i, 0)),
        out_shape=jax.ShapeDtypeStruct((t, D_MODEL), F32),
        compiler_params=_cparams(1),
        name="mix_xattn",
    )(x, *token_args, *const_args, g, wq, kv, wo)


def _swiglu_kernel(x_ref, g_ref, wg_ref, wu_ref, wd_ref, gf_ref, y_ref, *, final_norm):
    sub = x_ref.shape[0] // FFN_SPLIT
    for i in range(FFN_SPLIT):
        rows = slice(i * sub, (i + 1) * sub)
        x = x_ref[rows, :]
        hb = _rms(x, g_ref[...]).astype(BF16)
        gate = _dot(hb, wg_ref[...])
        up = _dot(hb, wu_ref[...])
        act = (gate / (1.0 + jnp.exp(-gate)) * up).astype(BF16)
        y = x + _dot(act, wd_ref[...])
        if final_norm:
            y = _rms(y, gf_ref[...])
        y_ref[rows, :] = y


def _swiglu(x, g, wg, wu, wd, g_final, final_norm):
    t = x.shape[0]
    tm = FFN_TILE
    assert t % tm == 0
    return pl.pallas_call(
        functools.partial(_swiglu_kernel, final_norm=final_norm),
        grid=(t // tm,),
        in_specs=[pl.BlockSpec((tm, D_MODEL), lambda i: (i, 0)),
                  _const_spec((1, D_MODEL)),
                  _resident_spec((D_MODEL, D_FF)),
                  _resident_spec((D_MODEL, D_FF)),
                  _resident_spec((D_FF, D_MODEL)),
                  _const_spec((1, D_MODEL))],
        out_specs=pl.BlockSpec((tm, D_MODEL), lambda i: (i, 0)),
        out_shape=jax.ShapeDtypeStruct((t, D_MODEL), F32),
        compiler_params=_cparams(1),
        name="swiglu",
    )(x, g, wg, wu, wd, g_final)


def _rope_tables(seq):
    tok = jnp.arange(seq, dtype=jnp.int32)
    row = (tok // GRID_W).astype(F32)
    col = (tok % GRID_W).astype(F32)
    axis_dim = HEAD_DIM // 2
    inv_freq = ROPE_THETA ** (-jnp.arange(0, axis_dim, 2, dtype=F32) / axis_dim)
    ang = jnp.concatenate([row[:, None] * inv_freq, col[:, None] * inv_freq], axis=-1)
    return jnp.cos(ang).T, jnp.sin(ang).T


def _prepare_ab(w_in, g_qn, g_kn, w_out):
    perm = np.concatenate([np.arange(0, HEAD_DIM, 2), np.arange(1, HEAD_DIM, 2)])
    cols = np.concatenate(
        [h * HEAD_DIM + perm for h in range(A_Q_HEADS + A_KV_HEADS)]
        + [np.arange(A_Q_W + A_KV_W, A_W)])
    waT = w_in[:, cols].T.astype(BF16)
    zbw = w_in[:, A_W:].reshape(D_MODEL, 3, B_BRANCHES, B_W)
    qscale = HEAD_DIM ** -0.5
    wb = jnp.stack([jnp.concatenate([zbw[:, 0, g] * qscale, zbw[:, 1, g], zbw[:, 2, g]], axis=1)
                    for g in range(B_BRANCHES)]).astype(BF16)
    gq = (g_qn[perm] * (qscale * LOG2E)).reshape(HEAD_DIM, 1).astype(F32)
    gk = g_kn[perm].reshape(HEAD_DIM, 1).astype(F32)
    perms = np.stack([_tile_perm(dil) for _, dil in B_PATTERNS[1:]])
    tile_perm = jnp.asarray(perms, BF16)
    tile_perm_t = jnp.asarray(perms.transpose(0, 2, 1), BF16)
    return waT, wb, tile_perm, tile_perm_t, gq, gk, w_out[:A_Q_W].astype(BF16), w_out[A_Q_W:].astype(BF16)


def _prepare_layers(p):
    depth = p["g_mix"].shape[0]
    row = lambda g: g.reshape(1, D_MODEL)
    layers = []
    for l in range(depth):
        lay = dict(g_mix=row(p["g_mix"][l]), g_xattn=row(p["g_xattn"][l]), g_mem=row(p["g_mem"][l]),
                   g_ffn=row(p["g_ffn"][l]),
                   wq=(p["wq_x"][l] * X_HEAD_DIM ** -0.5).astype(BF16), wkv=p["wkv_x"][l].astype(BF16),
                   wo=p["wo_x"][l].astype(BF16), wg=p["w_gu"][l][:, :D_FF].astype(BF16),
                   wu=p["w_gu"][l][:, D_FF:].astype(BF16), wd=p["w_down"][l].astype(BF16))
        i = l // 2
        if l % 2 == 0:
            lay["ab"] = _prepare_ab(p["w_in_ab"][i], p["g_qn"][i], p["g_kn"][i], p["w_out_ab"][i])
        else:
            w_in = p["w_in_c"][i]
            lay["w_in_c"] = jnp.concatenate(
                [w_in[:, :C_W] * HEAD_DIM ** -0.5, w_in[:, C_W:]], axis=1).astype(BF16)
            lay["rpb_c"] = p["rpb_c"][i]
            lay["bias_c"] = _natten_bias(p["rpb_c"][i])
            lay["w_out_c"] = p["w_out_c"][i].astype(BF16)
        layers.append(lay)
    return layers


def _trunk(x, mem, layers, g_final):
    batch, seq, d_model = x.shape
    assert d_model == D_MODEL and mem.shape[0] == batch and mem.shape[2] == D_MODEL
    assert seq % A_TQ == 0 and seq % TOKEN_TILE == 0 and seq >= 2 * A_TK
    assert seq % (C_ROWS_Q * GRID_W) == 0 and seq >= C_ROWS_K * GRID_W
    assert all(seq % (dil * B_SUB) == 0 for _, dil in B_PATTERNS)
    x = x.reshape(batch * seq, D_MODEL)
    for l, lay in enumerate(layers):
        if l % 2 == 0:
            waT, wb, perm, permT, gq, gk, wo_a, wo_b = lay["ab"]
            cosT, sinT = _rope_tables(seq)
            qT, k, vT, stats, zb = _inproj_ab(x, lay["g_mix"], waT, wb, perm, gq, gk, cosT, sinT, seq)
            oa = _gqa_attention(qT, k, vT, stats, batch, seq)
            rs = [_dilated_branch(zb, g, batch, seq) for g in range(B_BRANCHES)]
            body, token_args, const_args = _mix_ab_xattn_kernel, [oa] + rs, [permT, wo_a, wo_b]
            tm = FFN_TILE
        else:
            z, stats = _inproj(x, lay["g_mix"], lay["w_in_c"])
            oc = _natten(z, stats, lay["rpb_c"], lay["bias_c"], batch, seq)
            body, token_args, const_args = _mix_c_xattn_kernel, [oc], [lay["w_out_c"]]
            tm = FFN_TILE
        kv = _mem_kv(mem, lay["g_mem"], lay["wkv"])
        x = _mix_xattn(body, x, token_args, const_args, lay["g_xattn"], lay["wq"], kv, lay["wo"], seq, tm)
        x = _swiglu(x, lay["g_ffn"], lay["wg"], lay["wu"], lay["wd"], g_final,
                    final_norm=(l == len(layers) - 1))
    return x.reshape(batch, seq, D_MODEL)


def kernel(x_prompt, x_sample, mem_prompt, mem_sample, g_mix, w_in_ab, g_qn, g_kn, w_out_ab, w_in_c,
           rpb_c, w_out_c, g_xattn, g_mem, wq_x, wkv_x, wo_x, g_ffn, w_gu, w_down, g_final):
    layers = _prepare_layers(dict(
        g_mix=g_mix, w_in_ab=w_in_ab, g_qn=g_qn, g_kn=g_kn, w_out_ab=w_out_ab, w_in_c=w_in_c,
        rpb_c=rpb_c, w_out_c=w_out_c, g_xattn=g_xattn, g_mem=g_mem, wq_x=wq_x, wkv_x=wkv_x,
        wo_x=wo_x, g_ffn=g_ffn, w_gu=w_gu, w_down=w_down))
    g_final = g_final.reshape(1, D_MODEL)
    return (_trunk(x_prompt, mem_prompt, layers, g_final), _trunk(x_sample, mem_sample, layers, g_final))
```
